```python
import math
import jax
import jax.numpy as jnp
from jax import lax
import numpy as np

D_MODEL = 2048
BATCH = 1
SEQ = 8192
DEPTH = 1

CHUNK = 64
N_QK_HEADS = 16
N_V_HEADS = 32
HEAD_K = 128
HEAD_V = 128
QK_WIDTH = N_QK_HEADS * HEAD_K
V_WIDTH = N_V_HEADS * HEAD_V
SHORT_CONV = 4
CONV_CH = D_MODEL
CONV_WIDTH = 31
N_EXPERTS = 256
TOP_K = 8
N_GROUPS = 8
TOPK_GROUPS = 4
D_EXPERT = 512
D_SHARED = 512
ROUTE_SCALE = 2.5
EXPERT_BLOCK = 128
DN_ALPHA = (2 * DEPTH) ** 0.25
DN_BETA = (8 * DEPTH) ** -0.25
LN_EPS = 1e-5
NORM_EPS = 1e-6
QKV_COLS = 2 * QK_WIDTH + V_WIDTH
SPLIT_POINTS = (QKV_COLS,
                QKV_COLS + V_WIDTH,
                QKV_COLS + V_WIDTH + N_V_HEADS,
                QKV_COLS + V_WIDTH + 2 * N_V_HEADS,
                QKV_COLS + V_WIDTH + 2 * N_V_HEADS + 2 * CONV_CH)
IN_COLS = SPLIT_POINTS[-1] + 2 * D_MODEL

kernel_name = 'hybrid_gdn_conformer_moe_deepnorm'


def layer_norm(x, g, b):
    xf = x.astype(jnp.float32)
    mu = xf.mean(-1, keepdims=True)
    var = jnp.square(xf - mu).mean(-1, keepdims=True)
    y = (xf - mu) * lax.rsqrt(var + LN_EPS) * g.astype(jnp.float32) + b.astype(jnp.float32)
    return y.astype(x.dtype)


def l2_normalize(x):
    return x * lax.rsqrt(jnp.sum(x * x, -1, keepdims=True) + NORM_EPS)


def causal_depthwise_conv(x, w):
    width, ch = w.shape
    return lax.conv_general_dilated(x, w[:, None, :].astype(x.dtype), window_strides=(1,),
                                    padding=[(width - 1, 0)],
                                    dimension_numbers=('NWC', 'WIO', 'NWC'),
                                    feature_group_count=ch)


def gated_delta_rule_chunked(q, k, v, g, beta):
    B, H, L, dk = q.shape
    dv = v.shape[-1]
    n = L // CHUNK
    q = q.reshape(B, H, n, CHUNK, dk)
    k = k.reshape(B, H, n, CHUNK, dk)
    v = v.reshape(B, H, n, CHUNK, dv)
    g = g.reshape(B, H, n, CHUNK)
    beta = beta.reshape(B, H, n, CHUNK)
    gc = jnp.cumsum(g, axis=-1)
    causal = jnp.tril(jnp.ones((CHUNK, CHUNK), bool))
    strict = jnp.tril(jnp.ones((CHUNK, CHUNK), bool), -1)
    decay = jnp.exp(jnp.where(causal, gc[..., :, None] - gc[..., None, :], -jnp.inf))
    kb = k * beta[..., None]
    a_mat = jnp.where(strict, jnp.einsum('bhncd,bhnsd->bhncs', kb, k) * decay, 0.0)
    lower = a_mat + jnp.eye(CHUNK, dtype=a_mat.dtype)
    rhs = jnp.concatenate([v * beta[..., None], kb * jnp.exp(gc)[..., None]], axis=-1)
    sol = lax.linalg.triangular_solve(lower, rhs, left_side=True, lower=True, unit_diagonal=True)
    u = sol[..., :dv]
    w = sol[..., dv:]
    qk = jnp.einsum('bhncd,bhnsd->bhncs', q, k) * decay
    q_dec = q * jnp.exp(gc)[..., None]
    g_last = gc[..., -1]
    k_dec = k * jnp.exp(g_last[..., None] - gc)[..., None]

    def step(state, xs):
        u_i, w_i, qk_i, qd_i, kd_i, gl_i = xs
        v_new = u_i - jnp.einsum('bhcd,bhde->bhce', w_i, state)
        o_i = jnp.einsum('bhcd,bhde->bhce', qd_i, state) + jnp.einsum('bhcs,bhse->bhce', qk_i, v_new)
        state = state * jnp.exp(gl_i)[..., None, None] + jnp.einsum('bhcd,bhce->bhde', kd_i, v_new)
        return state, o_i

    xs = (jnp.moveaxis(u, 2, 0), jnp.moveaxis(w, 2, 0), jnp.moveaxis(qk, 2, 0),
          jnp.moveaxis(q_dec, 2, 0), jnp.moveaxis(k_dec, 2, 0), jnp.moveaxis(g_last, 2, 0))
    s0 = jnp.zeros((B, H, dk, dv), jnp.float32)
    _, o = lax.scan(step, s0, xs)
    return jnp.moveaxis(o, 0, 2).reshape(B, H, L, dv)


def token_mixing_sublayer(x, w_in, w_short_conv, a_log, dt_bias, delta_norm_w, w_o_delta,
                          conv_dw_w, conv_dw_b, conv_ln_g, conv_ln_b, w_pw2, b_pw2,
                          w_out, ln1_g, ln1_b):
    B, L, _ = x.shape
    f32 = jnp.float32
    proj = x @ w_in
    qkv, z, a, b, c_glu, gates = jnp.split(proj, SPLIT_POINTS, axis=-1)

    qkv = jax.nn.silu(causal_depthwise_conv(qkv, w_short_conv))
    q, k, v = jnp.split(qkv, [QK_WIDTH, 2 * QK_WIDTH], axis=-1)
    rep = N_V_HEADS // N_QK_HEADS
    q = l2_normalize(q.reshape(B, L, N_QK_HEADS, HEAD_K).astype(f32)) * (HEAD_K ** -0.5)
    k = l2_normalize(k.reshape(B, L, N_QK_HEADS, HEAD_K).astype(f32))
    q = jnp.repeat(q, rep, axis=2)
    k = jnp.repeat(k, rep, axis=2)
    v = v.reshape(B, L, N_V_HEADS, HEAD_V).astype(f32)
    beta = jax.nn.sigmoid(b.astype(f32))
    g = -jnp.exp(a_log.astype(f32)) * jax.nn.softplus(a.astype(f32) + dt_bias.astype(f32))
    o = gated_delta_rule_chunked(q.transpose(0, 2, 1, 3), k.transpose(0, 2, 1, 3),
                                 v.transpose(0, 2, 1, 3), g.transpose(0, 2, 1),
                                 beta.transpose(0, 2, 1))
    o = o.transpose(0, 2, 1, 3)
    o = o * lax.rsqrt(jnp.mean(o * o, -1, keepdims=True) + NORM_EPS) * delta_norm_w.astype(f32)
    o = o * jax.nn.silu(z.reshape(B, L, N_V_HEADS, HEAD_V).astype(f32))
    y_delta = o.reshape(B, L, V_WIDTH).astype(x.dtype) @ w_o_delta

    c_val, c_gate = jnp.split(c_glu, 2, axis=-1)
    c = c_val * jax.nn.sigmoid(c_gate)
    c = causal_depthwise_conv(c, conv_dw_w) + conv_dw_b
    c = jax.nn.silu(layer_norm(c, conv_ln_g, conv_ln_b))
    y_conv = c @ w_pw2 + b_pw2

    gate_a, gate_c = jnp.split(gates, 2, axis=-1)
    mixed = jax.nn.sigmoid(gate_a) * y_delta + jax.nn.sigmoid(gate_c) * y_conv
    return layer_norm(DN_ALPHA * x + mixed @ w_out, ln1_g, ln1_b)


def route(t, w_router, router_bias):
    n_tok = t.shape[0]
    per_group = N_EXPERTS // N_GROUPS
    scores = jax.nn.sigmoid((t @ w_router).astype(jnp.float32))
    biased = scores + router_bias.astype(jnp.float32)
    grp_score = lax.top_k(biased.reshape(n_tok, N_GROUPS, per_group), 2)[0].sum(-1)
    _, top_grp = lax.top_k(grp_score, TOPK_GROUPS)
    grp_mask = jax.nn.one_hot(top_grp, N_GROUPS, dtype=jnp.float32).sum(1) > 0
    masked = jnp.where(jnp.repeat(grp_mask, per_group, axis=1), biased, -jnp.inf)
    _, idx = lax.top_k(masked, TOP_K)
    wts = jnp.take_along_axis(scores, idx, axis=1)
    wts = wts / jnp.sum(wts, -1, keepdims=True) * ROUTE_SCALE
    return idx, wts


def routed_experts(h, idx, wts, w_gate, w_up, w_down):
    n_tok, d = h.shape
    n_assign = n_tok * TOP_K
    flat_e = idx.reshape(-1).astype(jnp.int32)
    flat_tok = jnp.repeat(jnp.arange(n_tok, dtype=jnp.int32), TOP_K)
    flat_w = wts.reshape(-1).astype(h.dtype)
    order = jnp.argsort(flat_e)
    se = flat_e[order]
    counts = jnp.bincount(flat_e, length=N_EXPERTS).astype(jnp.int32)
    padded = (counts + EXPERT_BLOCK - 1) // EXPERT_BLOCK * EXPERT_BLOCK
    pad_end = jnp.cumsum(padded)
    pad_start = pad_end - padded
    grp_start = jnp.cumsum(counts) - counts
    dest = pad_start[se] + jnp.arange(n_assign, dtype=jnp.int32) - grp_start[se]
    n_blocks = -(-(n_assign + N_EXPERTS * (EXPERT_BLOCK - 1)) // EXPERT_BLOCK)
    n_rows = n_blocks * EXPERT_BLOCK
    row_tok = jnp.full((n_rows,), n_tok, jnp.int32).at[dest].set(flat_tok[order])
    row_w = jnp.zeros((n_rows,), h.dtype).at[dest].set(flat_w[order])
    block_start = jnp.arange(n_blocks, dtype=jnp.int32) * EXPERT_BLOCK
    block_e = jnp.minimum(jnp.searchsorted(pad_end, block_start, side='right'), N_EXPERTS - 1)
    h_pad = jnp.concatenate([h, jnp.zeros((1, d), h.dtype)], axis=0)

    def step(acc, blk):
        tok, w_row, e = blk
        xb = h_pad[tok]
        yb = (jax.nn.silu(xb @ w_gate[e]) * (xb @ w_up[e])) @ w_down[e]
        return acc.at[tok].add(yb * w_row[:, None]), None

    acc0 = jnp.zeros((n_tok + 1, d), h.dtype)
    acc, _ = lax.scan(step, acc0, (row_tok.reshape(n_blocks, EXPERT_BLOCK),
                                   row_w.reshape(n_blocks, EXPERT_BLOCK), block_e))
    return acc[:n_tok]


def moe_sublayer(x, w_router, router_bias, w_gate, w_up, w_down,
                 w_sh_gate, w_sh_up, w_sh_down, ln2_g, ln2_b):
    B, L, D = x.shape
    t = x.reshape(B * L, D)
    idx, wts = route(t, w_router, router_bias)
    y = routed_experts(t, idx, wts, w_gate, w_up, w_down)
    y = y + (jax.nn.silu(t @ w_sh_gate) * (t @ w_sh_up)) @ w_sh_down
    return layer_norm(DN_ALPHA * x + y.reshape(B, L, D), ln2_g, ln2_b)


def setup_inputs(seed: int = 0) -> dict:
    key = jax.random.key(seed)
    ks = jax.random.split(key, 26)
    f32 = jnp.float32
    nl = DEPTH

    def nrm(k, shape, scale):
        return jax.random.normal(k, shape, f32) * scale

    dt = jnp.exp(jax.random.uniform(ks[4], (nl, N_V_HEADS), f32, math.log(1e-3), math.log(1e-1)))
    return {
        'x': nrm(ks[0], (BATCH, SEQ, D_MODEL), 1.0),
        'w_in': nrm(ks[1], (nl, D_MODEL, IN_COLS), D_MODEL ** -0.5),
        'w_short_conv': nrm(ks[2], (nl, SHORT_CONV, QKV_COLS), SHORT_CONV ** -0.5),
        'a_log': jnp.log(jax.random.uniform(ks[3], (nl, N_V_HEADS), f32, 1.0, 16.0)),
        'dt_bias': dt + jnp.log(-jnp.expm1(-dt)),
        'delta_norm_w': 1.0 + nrm(ks[5], (nl, HEAD_V), 0.02),
        'w_o_delta': nrm(ks[6], (nl, V_WIDTH, D_MODEL), V_WIDTH ** -0.5),
        'conv_dw_w': nrm(ks[7], (nl, CONV_WIDTH, CONV_CH), CONV_WIDTH ** -0.5),
        'conv_dw_b': nrm(ks[8], (nl, CONV_CH), 0.02),
        'conv_ln_g': 1.0 + nrm(ks[9], (nl, CONV_CH), 0.02),
        'conv_ln_b': nrm(ks[10], (nl, CONV_CH), 0.02),
        'w_pw2': nrm(ks[11], (nl, CONV_CH, D_MODEL), CONV_CH ** -0.5),
        'b_pw2': nrm(ks[12], (nl, D_MODEL), 0.02),
        'w_out': nrm(ks[13], (nl, D_MODEL, D_MODEL), D_MODEL ** -0.5 * DN_BETA),
        'ln1_g': 1.0 + nrm(ks[14], (nl, D_MODEL), 0.02),
        'ln1_b': nrm(ks[15], (nl, D_MODEL), 0.02),
        'w_router': nrm(ks[16], (nl, D_MODEL, N_EXPERTS), D_MODEL ** -0.5),
        'router_bias': nrm(ks[17], (nl, N_EXPERTS), 0.01),
        'w_gate': nrm(ks[18], (nl, N_EXPERTS, D_MODEL, D_EXPERT), D_MODEL ** -0.5),
        'w_up': nrm(ks[19], (nl, N_EXPERTS, D_MODEL, D_EXPERT), D_MODEL ** -0.5),
        'w_down': nrm(ks[20], (nl, N_EXPERTS, D_EXPERT, D_MODEL), D_EXPERT ** -0.5 * DN_BETA),
        'w_sh_gate': nrm(ks[21], (nl, D_MODEL, D_SHARED), D_MODEL ** -0.5),
        'w_sh_up': nrm(ks[22], (nl, D_MODEL, D_SHARED), D_MODEL ** -0.5),
        'w_sh_down': nrm(ks[23], (nl, D_SHARED, D_MODEL), D_SHARED ** -0.5 * DN_BETA),
        'ln2_g': 1.0 + nrm(ks[24], (nl, D_MODEL), 0.02),
        'ln2_b': nrm(ks[25], (nl, D_MODEL), 0.02),
    }


def reference(x, w_in, w_short_conv, a_log, dt_bias, delta_norm_w, w_o_delta, conv_dw_w,
              conv_dw_b, conv_ln_g, conv_ln_b, w_pw2, b_pw2, w_out, ln1_g, ln1_b, w_router,
              router_bias, w_gate, w_up, w_down, w_sh_gate, w_sh_up, w_sh_down, ln2_g, ln2_b):
    h = x
    for i in range(DEPTH):
        h = token_mixing_sublayer(h, w_in[i], w_short_conv[i], a_log[i], dt_bias[i], delta_norm_w[i],
                                  w_o_delta[i], conv_dw_w[i], conv_dw_b[i], conv_ln_g[i], conv_ln_b[i],
                                  w_pw2[i], b_pw2[i], w_out[i], ln1_g[i], ln1_b[i])
        h = moe_sublayer(h, w_router[i], router_bias[i], w_gate[i], w_up[i], w_down[i],
                         w_sh_gate[i], w_sh_up[i], w_sh_down[i], ln2_g[i], ln2_b[i])
    return h
```

```python
import functools

import jax
import jax.numpy as jnp
from jax import lax
from jax.experimental import pallas as pl
from jax.experimental.pallas import tpu as pltpu

F32 = jnp.float32
BF16 = jnp.bfloat16

D_MODEL = 2048
CHUNK = 64
N_QK_HEADS = 16
N_V_HEADS = 32
HEAD = 128
QK_WIDTH = N_QK_HEADS * HEAD
V_WIDTH = N_V_HEADS * HEAD
SHORT_CONV = 4
CONV_WIDTH = 31
N_EXPERTS = 256
TOP_K = 8
N_GROUPS = 8
TOPK_GROUPS = 4
D_EXPERT = 512
ROUTE_SCALE = 2.5
DN_ALPHA = 2.0 ** 0.25
LN_EPS = 1e-5
NORM_EPS = 1e-6
COL_Z = 2 * QK_WIDTH + V_WIDTH
COL_AB = COL_Z + V_WIDTH
COL_GLU = COL_AB + 2 * N_V_HEADS
COL_GATES = COL_GLU + 2 * D_MODEL

LANES = 128
SUBLANES = 8
VMEM_LIMIT = 56 * 1024 * 1024
TM = 512
TN = 512
GDN_ROWS = 256
CONV_TM = 256
HALO = 32
EXPERT_BLOCK = 128
COMBINE_TM = 32


def _params(sem, vmem=VMEM_LIMIT):
    return pltpu.CompilerParams(dimension_semantics=sem, vmem_limit_bytes=vmem)


def _sigmoid(x):
    return jax.nn.sigmoid(x)


def _silu(x):
    return x * jax.nn.sigmoid(x)


def _softplus(x):
    return jnp.maximum(x, 0.0) + jnp.log1p(jnp.exp(-jnp.abs(x)))


def _dot(a, b):
    return jnp.dot(a, b, preferred_element_type=F32)


def _proj_call(name, x, ws, w_offs, n_tiles, epilogue, extras, extra_specs, out_shapes, out_specs,
               scratch=(), tm=TM, tn=TN):
    m, k = x.shape
    single_out = not isinstance(out_shapes, (tuple, list))
    if single_out:
        out_shapes, out_specs = (out_shapes,), (out_specs,)
    nw, ne, no = len(ws), len(extras), len(out_shapes)
    needs_cast = [w.dtype != BF16 for w in ws]

    def body(*refs):
        x_ref = refs[0]
        w_refs = refs[1:1 + nw]
        ex_refs = refs[1 + nw:1 + nw + ne]
        out_refs = refs[1 + nw + ne:1 + nw + ne + no]
        scr = refs[1 + nw + ne + no:]
        wb_refs = scr[:sum(needs_cast)]
        rest = scr[sum(needs_cast):]
        i = pl.program_id(1)
        wsrc, c = [], 0
        for kk in range(nw):
            if needs_cast[kk]:
                wsrc.append(wb_refs[c])
                c += 1
            else:
                wsrc.append(w_refs[kk])

        @pl.when(i == 0)
        def _():
            cc = 0
            for kk in range(nw):
                if needs_cast[kk]:
                    wb_refs[cc][...] = w_refs[kk][...].astype(BF16)
                    cc += 1

        xv = x_ref[...]
        accs = [_dot(xv, wr[...]) for wr in wsrc]
        epilogue(i, accs, ex_refs, out_refs, rest)

    in_specs = [pl.BlockSpec((tm, k), lambda j, i: (i, 0))]
    for off in w_offs:
        in_specs.append(pl.BlockSpec((k, tn), lambda j, i, off=off: (0, j + off)))
    in_specs += list(extra_specs)
    scratch_shapes = [pltpu.VMEM((k, tn), BF16) for c in needs_cast if c] + list(scratch)
    res = pl.pallas_call(
        body,
        grid=(n_tiles, m // tm),
        in_specs=in_specs,
        out_specs=tuple(out_specs),
        out_shape=tuple(out_shapes),
        scratch_shapes=scratch_shapes,
        compiler_params=_params(("parallel", "arbitrary")),
        name=name,
    )(x, *ws, *extras)
    return res[0] if single_out else res


def _short_conv_silu(i, acc, cw_ref, buf, tm):
    @pl.when(i == 0)
    def _():
        buf[0:SUBLANES, :] = jnp.zeros((SUBLANES, buf.shape[1]), F32)

    buf[SUBLANES:SUBLANES + tm, :] = acc
    cw = cw_ref[...]
    y = acc * cw[SHORT_CONV - 1:SHORT_CONV, :]
    for s in range(SHORT_CONV - 1):
        y = y + buf[pl.ds(SUBLANES - (SHORT_CONV - 1) + s, tm), :] * cw[s:s + 1, :]
    buf[0:SUBLANES, :] = buf[tm:tm + SUBLANES, :]
    return _silu(y)


def _qk_epilogue(i, accs, ex, outs, scr):
    cw_ref, scale_ref = ex
    (buf,) = scr
    y = _short_conv_silu(i, accs[0], cw_ref, buf, TM)
    parts = []
    for g in range(TN // HEAD):
        yg = y[:, g * HEAD:(g + 1) * HEAD]
        parts.append(yg * lax.rsqrt(jnp.sum(yg * yg, axis=-1, keepdims=True) + NORM_EPS))
    outs[0][...] = jnp.concatenate(parts, axis=1) * scale_ref[...]


def _v_epilogue(i, accs, ex, outs, scr):
    (cw_ref,) = ex
    (buf,) = scr
    outs[0][...] = _short_conv_silu(i, accs[0], cw_ref, buf, TM)


def _z_epilogue(i, accs, ex, outs, scr):
    outs[0][...] = accs[0]


def _glu_epilogue(i, accs, ex, outs, scr):
    outs[0][...] = accs[0] * _sigmoid(accs[1])


def _gates_epilogue(i, accs, ex, outs, scr):
    outs[0][...] = _sigmoid(accs[0])


def _ab_epilogue(i, accs, ex, outs, scr):
    alog_ref, dtb_ref = ex
    gc_ref, bt_ref, gt_ref = outs
    acc = accs[0]
    g = -jnp.exp(alog_ref[...]) * _softplus(acc + dtb_ref[...])
    row = lax.broadcasted_iota(jnp.int32, g.shape, 0) % CHUNK
    s = 1
    while s < CHUNK:
        g = g + jnp.where(row >= s, pltpu.roll(g, s, axis=0), 0.0)
        s *= 2
    n_chunks = g.shape[0] // CHUNK
    tot = g.reshape(n_chunks, CHUNK, LANES)[:, CHUNK - 1:CHUNK, :]
    gc_ref[...] = g
    bt_ref[...] = _sigmoid(acc)
    gt_ref[...] = jnp.broadcast_to(tot, (n_chunks, CHUNK, LANES)).reshape(g.shape)


def _odelta_epilogue(i, accs, ex, outs, scr):
    (gate_ref,) = ex
    outs[0][...] = accs[0] * gate_ref[...]


def _pw2_epilogue(i, accs, ex, outs, scr):
    bias_ref, gate_ref, ya_ref = ex
    outs[0][...] = (ya_ref[...] + gate_ref[...] * (accs[0] + bias_ref[...])).astype(BF16)


def _gdn_kernel(q_ref, k_ref, v_ref, z_ref, gc_ref, bt_ref, gt_ref, gcrow_ref, gtrow_ref, nw_ref,
                o_ref, s_ref):
    p = pl.program_id(0)
    tb = pl.program_id(1)
    r = GDN_ROWS
    n_sub = r // CHUNK

    @pl.when(tb == 0)
    def _():
        s_ref[...] = jnp.zeros(s_ref.shape, F32)

    q = q_ref[...]
    k = k_ref[...]
    k16 = k.astype(BF16)
    nt = (((1,), (1,)), ((), ()))
    kk = lax.dot_general(k16, k16, nt, preferred_element_type=F32)
    qk = lax.dot_general(q.astype(BF16), k16, nt, preferred_element_type=F32)
    row = lax.broadcasted_iota(jnp.int32, (r, r), 0)
    col = lax.broadcasted_iota(jnp.int32, (r, r), 1)
    same = (row // CHUNK) == (col // CHUNK)
    causal = same & (col <= row)
    strict = same & (col < row)
    lane = lax.broadcasted_iota(jnp.int32, (r, LANES), 1)
    gc_all = gc_ref[...]
    bt_all = bt_ref[...]
    gt_all = gt_ref[...]

    for hh in range(2):
        h = 2 * p + hh
        gcol = jnp.sum(jnp.where(lane == h, gc_all, 0.0), axis=1, keepdims=True)
        bcol = jnp.sum(jnp.where(lane == h + N_V_HEADS, bt_all, 0.0), axis=1, keepdims=True)
        gtcol = jnp.sum(jnp.where(lane == h, gt_all, 0.0), axis=1, keepdims=True)
        grow = gcrow_ref[hh, 0]
        dm = jnp.exp(jnp.where(causal, gcol - grow, -jnp.inf))
        a = jnp.where(strict, bcol * kk * dm, 0.0)
        n = -a
        ap = a
        for _ in range(5):
            ap16 = ap.astype(BF16)
            ap = _dot(ap16, ap16)
            n = n + ap + _dot(n.astype(BF16), ap.astype(BF16))
        v = v_ref[:, hh * HEAD:(hh + 1) * HEAD]
        eg = jnp.exp(gcol)
        rhs = jnp.concatenate([v * bcol, k * (bcol * eg)], axis=1)
        sol = rhs + _dot(n.astype(BF16), rhs.astype(BF16))
        u = sol[:, :HEAD]
        w = sol[:, HEAD:]
        qkd = qk * dm
        qd = q * eg
        kd = k * jnp.exp(gtcol - gcol)
        state = s_ref[hh]
        o_parts = []
        for j in range(n_sub):
            sl = slice(j * CHUNK, (j + 1) * CHUNK)
            lhs = jnp.concatenate([w[sl], qd[sl]], axis=0).astype(BF16)
            ws_qs = _dot(lhs, state.astype(BF16))
            v_new = u[sl] - ws_qs[:CHUNK]
            v16 = v_new.astype(BF16)
            o_parts.append(ws_qs[CHUNK:] + _dot(qkd[sl, sl].astype(BF16), v16))
            ds = lax.dot_general(kd[sl].astype(BF16), v16, (((0,), (0,)), ((), ())),
                                 preferred_element_type=F32)
            state = state * jnp.exp(gtrow_ref[hh, 0, j:j + 1, :]) + ds
        s_ref[hh] = state
        o = jnp.concatenate(o_parts, axis=0)
        o = o * lax.rsqrt(jnp.mean(o * o, axis=-1, keepdims=True) + NORM_EPS) * nw_ref[...]
        o = o * _silu(z_ref[:, hh * HEAD:(hh + 1) * HEAD])
        o_ref[:, hh * HEAD:(hh + 1) * HEAD] = o.astype(o_ref.dtype)


def _gdn_call(qk, v, z, gc, bt, gt, gcrow, gtrow, nw):
    t = qk.shape[0]
    r = GDN_ROWS
    return pl.pallas_call(
        _gdn_kernel,
        grid=(N_QK_HEADS, t // r),
        in_specs=[
            pl.BlockSpec((r, HEAD), lambda p, tb: (tb, p)),
            pl.BlockSpec((r, HEAD), lambda p, tb: (tb, N_QK_HEADS + p)),
            pl.BlockSpec((r, 2 * HEAD), lambda p, tb: (tb, p)),
            pl.BlockSpec((r, 2 * HEAD), lambda p, tb: (tb, p)),
            pl.BlockSpec((r, LANES), lambda p, tb: (tb, 0)),
            pl.BlockSpec((r, LANES), lambda p, tb: (tb, 0)),
            pl.BlockSpec((r, LANES), lambda p, tb: (tb, 0)),
            pl.BlockSpec((2, 1, 1, r), lambda p, tb: (p, tb, 0, 0)),
            pl.BlockSpec((2, 1, r // CHUNK, LANES), lambda p, tb: (p, tb, 0, 0)),
            pl.BlockSpec((1, HEAD), lambda p, tb: (0, 0)),
        ],
        out_specs=pl.BlockSpec((r, 2 * HEAD), lambda p, tb: (tb, p)),
        out_shape=jax.ShapeDtypeStruct((t, V_WIDTH), BF16),
        scratch_shapes=[pltpu.VMEM((2, HEAD, HEAD), F32)],
        compiler_params=_params(("parallel", "arbitrary")),
        name="gdn",
    )(qk, qk, v, z, gc, bt, gt, gcrow, gtrow, nw)


def _dwconv_kernel(c_ref, w_ref, b_ref, g_ref, beta_ref, o_ref, buf):
    i = pl.program_id(0)
    tm = CONV_TM

    @pl.when(i == 0)
    def _():
        buf[0:HALO, :] = jnp.zeros((HALO, buf.shape[1]), F32)

    x = c_ref[...]
    buf[HALO:HALO + tm, :] = x
    acc = x * w_ref[CONV_WIDTH - 1:CONV_WIDTH, :] + b_ref[...]
    for s in range(CONV_WIDTH - 1):
        acc = acc + buf[pl.ds(HALO - (CONV_WIDTH - 1) + s, tm), :] * w_ref[s:s + 1, :]
    buf[0:HALO, :] = buf[tm:tm + HALO, :]
    mu = jnp.mean(acc, axis=-1, keepdims=True)
    xc = acc - mu
    var = jnp.mean(xc * xc, axis=-1, keepdims=True)
    y = xc * lax.rsqrt(var + LN_EPS) * g_ref[...] + beta_ref[...]
    o_ref[...] = _silu(y).astype(o_ref.dtype)


def _dwconv_call(c, w, b, g, beta):
    t, ch = c.shape
    tm = CONV_TM
    vec = pl.BlockSpec((1, ch), lambda i: (0, 0))
    return pl.pallas_call(
        _dwconv_kernel,
        grid=(t // tm,),
        in_specs=[pl.BlockSpec((tm, ch), lambda i: (i, 0)),
                  pl.BlockSpec((CONV_WIDTH, ch), lambda i: (0, 0)), vec, vec, vec],
        out_specs=pl.BlockSpec((tm, ch), lambda i: (i, 0)),
        out_shape=jax.ShapeDtypeStruct((t, ch), BF16),
        scratch_shapes=[pltpu.VMEM((HALO + tm, ch), F32)],
        compiler_params=_params(("arbitrary",)),
        name="dwconv_ln",
    )(c, w, b, g, beta)


def _layer_norm(r, g, b):
    mu = jnp.mean(r, axis=-1, keepdims=True)
    rc = r - mu
    var = jnp.mean(rc * rc, axis=-1, keepdims=True)
    return rc * lax.rsqrt(var + LN_EPS) * g + b


def _outproj_kernel(m_ref, w_ref, x_ref, g_ref, b_ref, h_ref):
    y = _dot(m_ref[...], w_ref[...])
    h_ref[...] = _layer_norm(DN_ALPHA * x_ref[...] + y, g_ref[...], b_ref[...])


def _outproj_call(mixed, w16, x, g, b):
    t, d = x.shape
    tm = 256
    vec = pl.BlockSpec((1, d), lambda i: (0, 0))
    return pl.pallas_call(
        _outproj_kernel,
        grid=(t // tm,),
        in_specs=[pl.BlockSpec((tm, d), lambda i: (i, 0)), pl.BlockSpec((d, d), lambda i: (0, 0)),
                  pl.BlockSpec((tm, d), lambda i: (i, 0)), vec, vec],
        out_specs=pl.BlockSpec((tm, d), lambda i: (i, 0)),
        out_shape=jax.ShapeDtypeStruct((t, d), F32),
        compiler_params=_params(("parallel",)),
        name="outproj_ln1",
    )(mixed, w16, x, g, b)


def _route_kernel(h_ref, w_ref, bias_ref, idx_ref, wts_ref):
    logits = jnp.dot(h_ref[...], w_ref[...], precision=lax.Precision.HIGHEST,
                     preferred_element_type=F32)
    scores = _sigmoid(logits)
    biased = scores + bias_ref[...]
    shape = biased.shape
    lane_i = lax.broadcasted_iota(jnp.int32, shape, 1)
    lane = lane_i.astype(F32)
    per_group = N_EXPERTS // N_GROUPS
    grp = (lane_i // per_group).astype(F32)
    neg = -jnp.inf
    big = 1e9

    def rmax(x):
        return jnp.max(x, axis=1, keepdims=True)

    def rmin(x):
        return jnp.min(x, axis=1, keepdims=True)

    gs = jnp.zeros(shape, F32)
    for g in range(N_GROUPS):
        in_g = grp == float(g)
        m = jnp.where(in_g, biased, neg)
        m1 = rmax(m)
        i1 = rmin(jnp.where(m == m1, lane, big))
        m2 = rmax(jnp.where(lane == i1, neg, m))
        gs = jnp.where(in_g, m1 + m2, gs)
    sel = jnp.zeros(shape, jnp.bool_)
    cur = gs
    for _ in range(TOPK_GROUPS):
        mx = rmax(cur)
        gi = rmin(jnp.where(cur == mx, grp, big))
        hit = grp == gi
        sel = sel | hit
        cur = jnp.where(hit, neg, cur)
    masked = jnp.where(sel, biased, neg)
    ids, ws = [], []
    for _ in range(TOP_K):
        mx = rmax(masked)
        ik = rmin(jnp.where(masked == mx, lane, big))
        hit = lane == ik
        ws.append(jnp.sum(jnp.where(hit, scores, 0.0), axis=1, keepdims=True))
        ids.append(ik)
        masked = jnp.where(hit, neg, masked)
    w = jnp.concatenate(ws, axis=1)
    wts_ref[...] = w / jnp.sum(w, axis=1, keepdims=True) * ROUTE_SCALE
    idx_ref[...] = jnp.concatenate(ids, axis=1).astype(jnp.int32)


def _route_call(h, w_router, bias):
    t, d = h.shape
    tm = 256
    return pl.pallas_call(
        _route_kernel,
        grid=(t // tm,),
        in_specs=[pl.BlockSpec((tm, d), lambda i: (i, 0)),
                  pl.BlockSpec((d, N_EXPERTS), lambda i: (0, 0)),
                  pl.BlockSpec((1, N_EXPERTS), lambda i: (0, 0))],
        out_specs=(pl.BlockSpec((tm, TOP_K), lambda i: (i, 0)),
                   pl.BlockSpec((tm, TOP_K), lambda i: (i, 0))),
        out_shape=(jax.ShapeDtypeStruct((t, TOP_K), jnp.int32),
                   jax.ShapeDtypeStruct((t, TOP_K), F32)),
        compiler_params=_params(("parallel",)),
        name="route",
    )(h, w_router, bias)


def _row_copy(src_hbm, row, dst, slot, sem):
    return pltpu.make_async_copy(src_hbm.at[pl.ds(row, 1), :], dst.at[pl.ds(slot, 1), :],
                                 sem.at[slot])


def _expert_kernel(be_ref, nu_ref, rt_ref, h_hbm, wg_ref, wu_ref, wd_ref, y_ref,
                   xbuf, wg16, wu16, wd16, sem):
    b = pl.program_id(0)
    used = b < nu_ref[0]

    @pl.when(used)
    def _():
        def issue(rr, c):
            _row_copy(h_hbm, rt_ref[0, 0, rr], xbuf, rr, sem).start()
            return c

        lax.fori_loop(0, EXPERT_BLOCK, issue, 0)
        prev = be_ref[jnp.maximum(b - 1, 0)]
        changed = (b == 0) | (be_ref[b] != prev)

        @pl.when(changed)
        def _():
            wg16[...] = wg_ref[...].astype(BF16)
            wu16[...] = wu_ref[...].astype(BF16)
            wd16[...] = wd_ref[...].astype(BF16)

        def wait(rr, c):
            _row_copy(h_hbm, 0, xbuf, rr, sem).wait()
            return c

        lax.fori_loop(0, EXPERT_BLOCK, wait, 0)
        x = xbuf[...].astype(BF16)
        g = _dot(x, wg16[...])
        u = _dot(x, wu16[...])
        act = (_silu(g) * u).astype(BF16)
        y_ref[...] = _dot(act, wd16[...])

    @pl.when(jnp.logical_not(used))
    def _():
        y_ref[...] = jnp.zeros(y_ref.shape, F32)


def _expert_call(block_e, n_used, row_tok, h, w_gate, w_up, w_down):
    n_blocks = block_e.shape[0]
    d = h.shape[1]
    blk = EXPERT_BLOCK
    grid_spec = pltpu.PrefetchScalarGridSpec(
        num_scalar_prefetch=2,
        grid=(n_blocks,),
        in_specs=[
            pl.BlockSpec((1, 1, blk), lambda b, be, nu: (b, 0, 0), memory_space=pltpu.SMEM),
            pl.BlockSpec(memory_space=pl.ANY),
            pl.BlockSpec((None, d, D_EXPERT), lambda b, be, nu: (be[b], 0, 0)),
            pl.BlockSpec((None, d, D_EXPERT), lambda b, be, nu: (be[b], 0, 0)),
            pl.BlockSpec((None, D_EXPERT, d), lambda b, be, nu: (be[b], 0, 0)),
        ],
        out_specs=pl.BlockSpec((blk, d), lambda b, be, nu: (b, 0)),
        scratch_shapes=[
            pltpu.VMEM((blk, d), F32),
            pltpu.VMEM((d, D_EXPERT), BF16),
            pltpu.VMEM((d, D_EXPERT), BF16),
            pltpu.VMEM((D_EXPERT, d), BF16),
            pltpu.SemaphoreType.DMA((blk,)),
        ],
    )
    return pl.pallas_call(
        _expert_kernel,
        grid_spec=grid_spec,
        out_shape=jax.ShapeDtypeStruct((n_blocks * blk, d), F32),
        compiler_params=_params(("arbitrary",)),
        name="experts",
    )(block_e, n_used, row_tok.reshape(n_blocks, 1, blk), h, w_gate, w_up, w_down)


def _shared_kernel(h_ref, wg_ref, wu_ref, wd_ref, y_ref, wg16, wu16, wd16):
    @pl.when(pl.program_id(0) == 0)
    def _():
        wg16[...] = wg_ref[...].astype(BF16)
        wu16[...] = wu_ref[...].astype(BF16)
        wd16[...] = wd_ref[...].astype(BF16)

    x = h_ref[...].astype(BF16)
    g = _dot(x, wg16[...])
    u = _dot(x, wu16[...])
    y_ref[...] = _dot((_silu(g) * u).astype(BF16), wd16[...])


def _shared_call(h, wg, wu, wd):
    t, d = h.shape
    ds = wg.shape[1]
    tm = 512
    return pl.pallas_call(
        _shared_kernel,
        grid=(t // tm,),
        in_specs=[pl.BlockSpec((tm, d), lambda i: (i, 0)),
                  pl.BlockSpec((d, ds), lambda i: (0, 0)),
                  pl.BlockSpec((d, ds), lambda i: (0, 0)),
                  pl.BlockSpec((ds, d), lambda i: (0, 0))],
        out_specs=pl.BlockSpec((tm, d), lambda i: (i, 0)),
        out_shape=jax.ShapeDtypeStruct((t, d), F32),
        scratch_shapes=[pltpu.VMEM((d, ds), BF16), pltpu.VMEM((d, ds), BF16),
                        pltpu.VMEM((ds, d), BF16)],
        compiler_params=_params(("arbitrary",)),
        name="shared_expert",
    )(h, wg, wu, wd)


def _combine_kernel(dest_ref, ys_hbm, wts_ref, h_ref, ysh_ref, g_ref, b_ref, o_ref, buf, sem):
    tm = COMBINE_TM
    n = tm * TOP_K

    def issue(a, c):
        _row_copy(ys_hbm, dest_ref[0, 0, a], buf, a, sem).start()
        return c

    lax.fori_loop(0, n, issue, 0)
    acc = DN_ALPHA * h_ref[...] + ysh_ref[...]
    wts = wts_ref[...]

    def wait(a, c):
        _row_copy(ys_hbm, 0, buf, a, sem).wait()
        return c

    lax.fori_loop(0, n, wait, 0)
    for kk in range(TOP_K):
        acc = acc + wts[:, kk:kk + 1] * buf[kk * tm:(kk + 1) * tm, :]
    o_ref[...] = _layer_norm(acc, g_ref[...], b_ref[...])


def _combine_call(dest_km, ys, wts, h, ysh, g, b):
    t, d = h.shape
    tm = COMBINE_TM
    vec = pl.BlockSpec((1, d), lambda i: (0, 0))
    return pl.pallas_call(
        _combine_kernel,
        grid=(t // tm,),
        in_specs=[
            pl.BlockSpec((1, 1, tm * TOP_K), lambda i: (i, 0, 0), memory_space=pltpu.SMEM),
            pl.BlockSpec(memory_space=pl.ANY),
            pl.BlockSpec((tm, TOP_K), lambda i: (i, 0)),
            pl.BlockSpec((tm, d), lambda i: (i, 0)),
            pl.BlockSpec((tm, d), lambda i: (i, 0)),
            vec, vec,
        ],
        out_specs=pl.BlockSpec((tm, d), lambda i: (i, 0)),
        out_shape=jax.ShapeDtypeStruct((t, d), F32),
        scratch_shapes=[pltpu.VMEM((tm * TOP_K, d), F32), pltpu.SemaphoreType.DMA((tm * TOP_K,))],
        compiler_params=_params(("arbitrary",)),
        name="combine_ln2",
    )(dest_km, ys, wts, h, ysh, g, b)


def _token_mixing(x, w_in, w_short_conv, a_log, dt_bias, delta_norm_w, w_o_delta, conv_dw_w,
                  conv_dw_b, conv_ln_g, conv_ln_b, w_pw2, b_pw2, w_out, ln1_g, ln1_b):
    t, d = x.shape
    x16 = x.astype(BF16)
    row = lambda a: a.reshape(1, -1)
    tile = lambda cols: pl.BlockSpec((TM, TN), lambda j, i: (i, j))
    conv_buf = pltpu.VMEM((TM + SUBLANES, TN), F32)

    scale = jnp.concatenate([jnp.full((1, QK_WIDTH), HEAD ** -0.5, F32), jnp.ones((1, QK_WIDTH), F32)], axis=1)
    qk = _proj_call(
        "proj_qk", x16, [w_in], [0], 2 * QK_WIDTH // TN, _qk_epilogue,
        [w_short_conv, scale],
        [pl.BlockSpec((SHORT_CONV, TN), lambda j, i: (0, j)), pl.BlockSpec((1, TN), lambda j, i: (0, j))],
        jax.ShapeDtypeStruct((t, 2 * QK_WIDTH), F32), tile(None), scratch=[conv_buf])
    v_off = 2 * QK_WIDTH // TN
    v = _proj_call(
        "proj_v", x16, [w_in], [v_off], V_WIDTH // TN, _v_epilogue,
        [w_short_conv], [pl.BlockSpec((SHORT_CONV, TN), lambda j, i: (0, j + v_off))],
        jax.ShapeDtypeStruct((t, V_WIDTH), F32), tile(None), scratch=[conv_buf])
    z = _proj_call(
        "proj_z", x16, [w_in], [COL_Z // TN], V_WIDTH // TN, _z_epilogue, [], [],
        jax.ShapeDtypeStruct((t, V_WIDTH), F32), tile(None))

    w_ab = jnp.pad(w_in[:, COL_AB:COL_GLU], ((0, 0), (0, LANES - 2 * N_V_HEADS)))
    pad_h = lambda a: jnp.pad(a.reshape(1, -1), ((0, 0), (0, LANES - N_V_HEADS)))
    lane_vec = pl.BlockSpec((1, LANES), lambda j, i: (0, 0))
    lane_tile = pl.BlockSpec((TM, LANES), lambda j, i: (i, 0))
    lane_shape = jax.ShapeDtypeStruct((t, LANES), F32)
    gc, bt, gt = _proj_call(
        "proj_ab", x16, [w_ab], [0], 1, _ab_epilogue, [pad_h(a_log), pad_h(dt_bias)],
        [lane_vec, lane_vec], (lane_shape, lane_shape, lane_shape), (lane_tile, lane_tile, lane_tile),
        tn=LANES)

    w_glu = w_in[:, COL_GLU:COL_GATES].astype(BF16)
    w_gates = w_in[:, COL_GATES:].astype(BF16)
    c = _proj_call(
        "proj_glu", x16, [w_glu, w_glu], [0, d // TN], d // TN, _glu_epilogue, [], [],
        jax.ShapeDtypeStruct((t, d), F32), tile(None))
    gates = _proj_call(
        "proj_gates", x16, [w_gates], [0], 2 * d // TN, _gates_epilogue, [], [],
        jax.ShapeDtypeStruct((t, 2 * d), F32), tile(None))

    r = GDN_ROWS
    gcrow = gc[:, :N_V_HEADS].T.reshape(N_V_HEADS, t // r, 1, r)
    gt_chunk = gt[CHUNK - 1::CHUNK, :N_V_HEADS].T
    gtrow = jnp.broadcast_to(gt_chunk[:, :, None], (N_V_HEADS, t // CHUNK, LANES))
    gtrow = gtrow.reshape(N_V_HEADS, t // r, r // CHUNK, LANES)
    og = _gdn_call(qk, v, z, gc, bt, gt, gcrow, gtrow, row(delta_norm_w))

    ya = _proj_call(
        "proj_odelta", og, [w_o_delta], [0], d // TN, _odelta_epilogue, [gates],
        [pl.BlockSpec((TM, TN), lambda j, i: (i, j))],
        jax.ShapeDtypeStruct((t, d), F32), tile(None))

    c_act = _dwconv_call(c, conv_dw_w, row(conv_dw_b), row(conv_ln_g), row(conv_ln_b))
    g_off = d // TN
    mixed = _proj_call(
        "proj_pw2", c_act, [w_pw2], [0], d // TN, _pw2_epilogue, [row(b_pw2), gates, ya],
        [pl.BlockSpec((1, TN), lambda j, i: (0, j)),
         pl.BlockSpec((TM, TN), lambda j, i: (i, j + g_off)),
         pl.BlockSpec((TM, TN), lambda j, i: (i, j))],
        jax.ShapeDtypeStruct((t, d), BF16), tile(None))
    return _outproj_call(mixed, w_out.astype(BF16), x, row(ln1_g), row(ln1_b))


def _dispatch_metadata(idx):
    t = idx.shape[0]
    blk = EXPERT_BLOCK
    n_assign = t * TOP_K
    n_blocks = -(-(n_assign + N_EXPERTS * (blk - 1)) // blk)
    flat_e = idx.reshape(-1)
    flat_tok = jnp.repeat(jnp.arange(t, dtype=jnp.int32), TOP_K)
    order = jnp.argsort(flat_e)
    se = flat_e[order]
    counts = jnp.bincount(flat_e, length=N_EXPERTS).astype(jnp.int32)
    padded = (counts + blk - 1) // blk * blk
    pad_end = jnp.cumsum(padded)
    pad_start = pad_end - padded
    grp_start = jnp.cumsum(counts) - counts
    dest_sorted = pad_start[se] + jnp.arange(n_assign, dtype=jnp.int32) - grp_start[se]
    row_tok = jnp.zeros((n_blocks * blk,), jnp.int32).at[dest_sorted].set(flat_tok[order])
    dest = jnp.zeros((n_assign,), jnp.int32).at[order].set(dest_sorted)
    n_used = (pad_end[-1] // blk).astype(jnp.int32)
    block_start = jnp.arange(n_blocks, dtype=jnp.int32) * blk
    block_e = jnp.minimum(jnp.searchsorted(pad_end, block_start, side='right'), N_EXPERTS - 1)
    block_e = block_e.astype(jnp.int32)
    last_e = block_e[jnp.maximum(n_used - 1, 0)]
    block_e = jnp.where(jnp.arange(n_blocks) < n_used, block_e, last_e)
    return block_e, n_used.reshape(1), row_tok, dest.reshape(t, TOP_K)


def _moe(h, w_router, router_bias, w_gate, w_up, w_down, w_sh_gate, w_sh_up, w_sh_down, ln2_g, ln2_b):
    t, d = h.shape
    idx, wts = _route_call(h, w_router, router_bias.reshape(1, -1))
    block_e, n_used, row_tok, dest = _dispatch_metadata(idx)
    ys = _expert_call(block_e, n_used, row_tok, h, w_gate, w_up, w_down)
    ysh = _shared_call(h, w_sh_gate, w_sh_up, w_sh_down)
    tm = COMBINE_TM
    dest_km = dest.reshape(t // tm, tm, TOP_K).transpose(0, 2, 1).reshape(t // tm, 1, tm * TOP_K)
    return _combine_call(dest_km, ys, wts, h, ysh, ln2_g.reshape(1, -1), ln2_b.reshape(1, -1))


def kernel(x, w_in, w_short_conv, a_log, dt_bias, delta_norm_w, w_o_delta, conv_dw_w, conv_dw_b,
           conv_ln_g, conv_ln_b, w_pw2, b_pw2, w_out, ln1_g, ln1_b, w_router, router_bias, w_gate,
           w_up, w_down, w_sh_gate, w_sh_up, w_sh_down, ln2_g, ln2_b):
    batch, seq, d = x.shape
    depth = w_in.shape[0]
    outs = []
    for bi in range(batch):
        h = x[bi]
        for li in range(depth):
            h = _token_mixing(h, w_in[li], w_short_conv[li], a_log[li], dt_bias[li], delta_norm_w[li],
                              w_o_delta[li], conv_dw_w[li], conv_dw_b[li], conv_ln_g[li], conv_ln_b[li],
                              w_pw2[li], b_pw2[li], w_out[li], ln1_g[li], ln1_b[li])
            h = _moe(h, w_router[li], router_bias[li], w_gate[li], w_up[li], w_down[li],
                     w_sh_gate[li], w_sh_up[li], w_sh_down[li], ln2_g[li], ln2_b[li])
        outs.append(h)
    return jnp.stack(outs, axis=0)
```

```python
import jax
import jax.numpy as jnp
from jax import lax
from jax.experimental import pallas as pl
from jax.experimental.pallas import tpu as pltpu

F32 = jnp.float32
BF16 = jnp.bfloat16
I32 = jnp.int32

D_MODEL = 2048
CHUNK = 64
N_QK_HEADS = 16
N_V_HEADS = 32
HEAD = 128
QK_WIDTH = N_QK_HEADS * HEAD
V_WIDTH = N_V_HEADS * HEAD
SHORT_CONV = 4
CONV_WIDTH = 31
N_EXPERTS = 256
TOP_K = 8
N_GROUPS = 8
TOPK_GROUPS = 4
D_EXPERT = 512
ROUTE_SCALE = 2.5
DN_ALPHA = 2.0 ** 0.25
LN_EPS = 1e-5
NORM_EPS = 1e-6
COL_Z = 2 * QK_WIDTH + V_WIDTH
COL_AB = COL_Z + V_WIDTH
COL_GLU = COL_AB + 2 * N_V_HEADS
COL_GATES = COL_GLU + 2 * D_MODEL

LANES = 128
SUBLANES = 8
VMEM_LIMIT = 56 * 1024 * 1024
TM = 512
TN = 512
GDN_ROWS = 256
GDN_PAIRS = 2
CONV_TM = 256
HALO = 32
ROUTE_TM = 256
EXPERT_BLOCK = 128
SCATTER_TM = 64
COMBINE_TM = 32


def _params(sem, vmem=VMEM_LIMIT):
    return pltpu.CompilerParams(dimension_semantics=sem, vmem_limit_bytes=vmem)


def _sigmoid(x):
    return jax.nn.sigmoid(x)


def _silu(x):
    return x * jax.nn.sigmoid(x)


def _softplus(x):
    return jnp.maximum(x, 0.0) + jnp.log1p(jnp.exp(-jnp.abs(x)))


def _dot(a, b):
    return jnp.dot(a, b, preferred_element_type=F32)


def _proj_call(name, x, ws, w_offs, n_tiles, epilogue, extras, extra_specs, out_shapes, out_specs,
               scratch=(), tm=TM, tn=TN):
    m, k = x.shape
    single_out = not isinstance(out_shapes, (tuple, list))
    if single_out:
        out_shapes, out_specs = (out_shapes,), (out_specs,)
    nw, ne, no = len(ws), len(extras), len(out_shapes)
    needs_cast = [w.dtype != BF16 for w in ws]

    def body(*refs):
        x_ref = refs[0]
        w_refs = refs[1:1 + nw]
        ex_refs = refs[1 + nw:1 + nw + ne]
        out_refs = refs[1 + nw + ne:1 + nw + ne + no]
        scr = refs[1 + nw + ne + no:]
        wb_refs = scr[:sum(needs_cast)]
        rest = scr[sum(needs_cast):]
        i = pl.program_id(1)
        wsrc, c = [], 0
        for kk in range(nw):
            if needs_cast[kk]:
                wsrc.append(wb_refs[c])
                c += 1
            else:
                wsrc.append(w_refs[kk])

        @pl.when(i == 0)
        def _():
            cc = 0
            for kk in range(nw):
                if needs_cast[kk]:
                    wb_refs[cc][...] = w_refs[kk][...].astype(BF16)
                    cc += 1

        xv = x_ref[...]
        accs = [_dot(xv, wr[...]) for wr in wsrc]
        epilogue(i, accs, ex_refs, out_refs, rest)

    in_specs = [pl.BlockSpec((tm, k), lambda j, i: (i, 0))]
    for off in w_offs:
        in_specs.append(pl.BlockSpec((k, tn), lambda j, i, off=off: (0, j + off)))
    in_specs += list(extra_specs)
    scratch_shapes = [pltpu.VMEM((k, tn), BF16) for c in needs_cast if c] + list(scratch)
    res = pl.pallas_call(
        body,
        grid=(n_tiles, m // tm),
        in_specs=in_specs,
        out_specs=tuple(out_specs),
        out_shape=tuple(out_shapes),
        scratch_shapes=scratch_shapes,
        compiler_params=_params(("parallel", "arbitrary")),
        name=name,
    )(x, *ws, *extras)
    return res[0] if single_out else res


def _short_conv_silu(i, acc, cw_ref, buf, tm):
    @pl.when(i == 0)
    def _():
        buf[0:SUBLANES, :] = jnp.zeros((SUBLANES, buf.shape[1]), F32)

    buf[SUBLANES:SUBLANES + tm, :] = acc
    cw = cw_ref[...]
    y = acc * cw[SHORT_CONV - 1:SHORT_CONV, :]
    for s in range(SHORT_CONV - 1):
        y = y + buf[pl.ds(SUBLANES - (SHORT_CONV - 1) + s, tm), :] * cw[s:s + 1, :]
    buf[0:SUBLANES, :] = buf[tm:tm + SUBLANES, :]
    return _silu(y)


def _qk_epilogue(i, accs, ex, outs, scr):
    cw_ref, scale_ref = ex
    (buf,) = scr
    y = _short_conv_silu(i, accs[0], cw_ref, buf, TM)
    parts = []
    for g in range(TN // HEAD):
        yg = y[:, g * HEAD:(g + 1) * HEAD]
        parts.append(yg * lax.rsqrt(jnp.sum(yg * yg, axis=-1, keepdims=True) + NORM_EPS))
    outs[0][...] = jnp.concatenate(parts, axis=1) * scale_ref[...]


def _v_epilogue(i, accs, ex, outs, scr):
    (cw_ref,) = ex
    (buf,) = scr
    outs[0][...] = _short_conv_silu(i, accs[0], cw_ref, buf, TM)


def _z_epilogue(i, accs, ex, outs, scr):
    outs[0][...] = accs[0]


def _glu_epilogue(i, accs, ex, outs, scr):
    outs[0][...] = accs[0] * _sigmoid(accs[1])


def _gates_epilogue(i, accs, ex, outs, scr):
    outs[0][...] = _sigmoid(accs[0])


def _ab_epilogue(i, accs, ex, outs, scr):
    alog_ref, dtb_ref = ex
    gc_ref, bt_ref, gt_ref = outs
    acc = accs[0]
    g = -jnp.exp(alog_ref[...]) * _softplus(acc + dtb_ref[...])
    row = lax.broadcasted_iota(I32, g.shape, 0) % CHUNK
    s = 1
    while s < CHUNK:
        g = g + jnp.where(row >= s, pltpu.roll(g, s, axis=0), 0.0)
        s *= 2
    n_chunks = g.shape[0] // CHUNK
    tot = g.reshape(n_chunks, CHUNK, LANES)[:, CHUNK - 1:CHUNK, :]
    gc_ref[...] = g
    bt_ref[...] = _sigmoid(acc)
    gt_ref[...] = jnp.broadcast_to(tot, (n_chunks, CHUNK, LANES)).reshape(g.shape)


def _odelta_epilogue(i, accs, ex, outs, scr):
    (gate_ref,) = ex
    outs[0][...] = accs[0] * gate_ref[...]


def _pw2_epilogue(i, accs, ex, outs, scr):
    bias_ref, gate_ref, ya_ref = ex
    outs[0][...] = (ya_ref[...] + gate_ref[...] * (accs[0] + bias_ref[...])).astype(BF16)


def _gdn_head(q, k, kk, qk, causal, strict, gcol, bcol, gtcol, grow, gtrow, v, state):
    r = GDN_ROWS
    dm = jnp.exp(jnp.where(causal, gcol - grow, -jnp.inf))
    a = jnp.where(strict, bcol * kk * dm, 0.0)
    n = -a
    ap = a
    for _ in range(5):
        ap16 = ap.astype(BF16)
        ap = _dot(ap16, ap16)
        n = n + ap + _dot(n.astype(BF16), ap.astype(BF16))
    eg = jnp.exp(gcol)
    rhs = jnp.concatenate([v * bcol, k * (bcol * eg)], axis=1)
    sol16 = (rhs + _dot(n.astype(BF16), rhs.astype(BF16))).astype(BF16)
    x = _dot((qk * dm).astype(BF16), sol16)
    o_intra = x[:, :HEAD]
    qp16 = (q * eg - x[:, HEAD:]).astype(BF16)
    kd16 = (k * jnp.exp(gtcol - gcol)).astype(BF16)
    tn = (((0,), (0,)), ((), ()))
    o_parts = []
    for j in range(r // CHUNK):
        sl = slice(j * CHUNK, (j + 1) * CHUNK)
        kc = lax.dot_general(kd16[sl], sol16[sl], tn, preferred_element_type=F32)
        lhs = jnp.concatenate([kc[:, HEAD:].astype(BF16), qp16[sl]], axis=0)
        ks_qs = _dot(lhs, state.astype(BF16))
        o_parts.append(ks_qs[HEAD:] + o_intra[sl])
        state = state * jnp.exp(gtrow[j:j + 1, :]) + kc[:, :HEAD] - ks_qs[:HEAD]
    return jnp.concatenate(o_parts, axis=0), state


def _gdn_kernel(q_ref, k_ref, v_ref, z_ref, gc_ref, bt_ref, gt_ref, gcrow_ref, gtrow_ref, nw_ref,
                o_ref, s_ref):
    pp = pl.program_id(0)
    tb = pl.program_id(1)
    r = GDN_ROWS

    @pl.when(tb == 0)
    def _():
        s_ref[...] = jnp.zeros(s_ref.shape, F32)

    row = lax.broadcasted_iota(I32, (r, r), 0)
    col = lax.broadcasted_iota(I32, (r, r), 1)
    same = (row // CHUNK) == (col // CHUNK)
    causal = same & (col <= row)
    strict = same & (col < row)
    lane = lax.broadcasted_iota(I32, (r, LANES), 1)
    gc_all = gc_ref[...]
    bt_all = bt_ref[...]
    gt_all = gt_ref[...]
    nt = (((1,), (1,)), ((), ()))

    for pi in range(GDN_PAIRS):
        q = q_ref[:, pi * HEAD:(pi + 1) * HEAD]
        k = k_ref[:, pi * HEAD:(pi + 1) * HEAD]
        k16 = k.astype(BF16)
        kk = lax.dot_general(k16, k16, nt, preferred_element_type=F32)
        qk = lax.dot_general(q.astype(BF16), k16, nt, preferred_element_type=F32)
        for hh in range(2):
            hl = 2 * pi + hh
            h = 2 * GDN_PAIRS * pp + hl
            gcol = jnp.sum(jnp.where(lane == h, gc_all, 0.0), axis=1, keepdims=True)
            bcol = jnp.sum(jnp.where(lane == h + N_V_HEADS, bt_all, 0.0), axis=1, keepdims=True)
            gtcol = jnp.sum(jnp.where(lane == h, gt_all, 0.0), axis=1, keepdims=True)
            cols = slice(hl * HEAD, (hl + 1) * HEAD)
            o, state = _gdn_head(q, k, kk, qk, causal, strict, gcol, bcol, gtcol,
                                 gcrow_ref[hl, 0], gtrow_ref[hl, 0], v_ref[:, cols], s_ref[hl])
            s_ref[hl] = state
            o = o * lax.rsqrt(jnp.mean(o * o, axis=-1, keepdims=True) + NORM_EPS) * nw_ref[...]
            o_ref[:, cols] = (o * _silu(z_ref[:, cols])).astype(o_ref.dtype)


def _gdn_call(qk, v, z, gc, bt, gt, gcrow, gtrow, nw):
    t = qk.shape[0]
    r = GDN_ROWS
    p = GDN_PAIRS
    k_off = N_QK_HEADS // p
    return pl.pallas_call(
        _gdn_kernel,
        grid=(N_QK_HEADS // p, t // r),
        in_specs=[
            pl.BlockSpec((r, p * HEAD), lambda pp, tb: (tb, pp)),
            pl.BlockSpec((r, p * HEAD), lambda pp, tb: (tb, k_off + pp)),
            pl.BlockSpec((r, 2 * p * HEAD), lambda pp, tb: (tb, pp)),
            pl.BlockSpec((r, 2 * p * HEAD), lambda pp, tb: (tb, pp)),
            pl.BlockSpec((r, LANES), lambda pp, tb: (tb, 0)),
            pl.BlockSpec((r, LANES), lambda pp, tb: (tb, 0)),
            pl.BlockSpec((r, LANES), lambda pp, tb: (tb, 0)),
            pl.BlockSpec((2 * p, 1, 1, r), lambda pp, tb: (pp, tb, 0, 0)),
            pl.BlockSpec((2 * p, 1, r // CHUNK, LANES), lambda pp, tb: (pp, tb, 0, 0)),
            pl.BlockSpec((1, HEAD), lambda pp, tb: (0, 0)),
        ],
        out_specs=pl.BlockSpec((r, 2 * p * HEAD), lambda pp, tb: (tb, pp)),
        out_shape=jax.ShapeDtypeStruct((t, V_WIDTH), BF16),
        scratch_shapes=[pltpu.VMEM((2 * p, HEAD, HEAD), F32)],
        compiler_params=_params(("parallel", "arbitrary")),
        name="gdn",
    )(qk, qk, v, z, gc, bt, gt, gcrow, gtrow, nw)


def _layer_norm(r, g, b):
    mu = jnp.mean(r, axis=-1, keepdims=True)
    rc = r - mu
    var = jnp.mean(rc * rc, axis=-1, keepdims=True)
    return rc * lax.rsqrt(var + LN_EPS) * g + b


def _dwconv_kernel(c_ref, w_ref, b_ref, g_ref, beta_ref, o_ref, buf):
    i = pl.program_id(0)
    tm = CONV_TM

    @pl.when(i == 0)
    def _():
        buf[0:HALO, :] = jnp.zeros((HALO, buf.shape[1]), F32)

    x = c_ref[...]
    buf[HALO:HALO + tm, :] = x
    acc = x * w_ref[CONV_WIDTH - 1:CONV_WIDTH, :] + b_ref[...]
    for s in range(CONV_WIDTH - 1):
        acc = acc + buf[pl.ds(HALO - (CONV_WIDTH - 1) + s, tm), :] * w_ref[s:s + 1, :]
    buf[0:HALO, :] = buf[tm:tm + HALO, :]
    o_ref[...] = _silu(_layer_norm(acc, g_ref[...], beta_ref[...])).astype(o_ref.dtype)


def _dwconv_call(c, w, b, g, beta):
    t, ch = c.shape
    tm = CONV_TM
    vec = pl.BlockSpec((1, ch), lambda i: (0, 0))
    return pl.pallas_call(
        _dwconv_kernel,
        grid=(t // tm,),
        in_specs=[pl.BlockSpec((tm, ch), lambda i: (i, 0)),
                  pl.BlockSpec((CONV_WIDTH, ch), lambda i: (0, 0)), vec, vec, vec],
        out_specs=pl.BlockSpec((tm, ch), lambda i: (i, 0)),
        out_shape=jax.ShapeDtypeStruct((t, ch), BF16),
        scratch_shapes=[pltpu.VMEM((HALO + tm, ch), F32)],
        compiler_params=_params(("arbitrary",)),
        name="dwconv_ln",
    )(c, w, b, g, beta)


def _outproj_kernel(m_ref, w_ref, x_ref, g_ref, b_ref, h_ref):
    y = _dot(m_ref[...], w_ref[...])
    h_ref[...] = _layer_norm(DN_ALPHA * x_ref[...] + y, g_ref[...], b_ref[...])


def _outproj_call(mixed, w16, x, g, b):
    t, d = x.shape
    tm = 256
    vec = pl.BlockSpec((1, d), lambda i: (0, 0))
    return pl.pallas_call(
        _outproj_kernel,
        grid=(t // tm,),
        in_specs=[pl.BlockSpec((tm, d), lambda i: (i, 0)), pl.BlockSpec((d, d), lambda i: (0, 0)),
                  pl.BlockSpec((tm, d), lambda i: (i, 0)), vec, vec],
        out_specs=pl.BlockSpec((tm, d), lambda i: (i, 0)),
        out_shape=jax.ShapeDtypeStruct((t, d), F32),
        compiler_params=_params(("parallel",)),
        name="outproj_ln1",
    )(mixed, w16, x, g, b)


def _route_kernel(h_ref, w_ref, bias_ref, idx_ref, wts_ref, rank_ref, cnt_ref, carry):
    @pl.when(pl.program_id(0) == 0)
    def _():
        carry[...] = jnp.zeros(carry.shape, F32)

    logits = jnp.dot(h_ref[...], w_ref[...], precision=lax.Precision.HIGHEST,
                     preferred_element_type=F32)
    scores = _sigmoid(logits)
    biased = scores + bias_ref[...]
    shape = biased.shape
    tm = shape[0]
    lane_i = lax.broadcasted_iota(I32, shape, 1)
    lane = lane_i.astype(F32)
    per_group = N_EXPERTS // N_GROUPS
    grp = (lane_i // per_group).astype(F32)
    neg = -jnp.inf
    big = 1e9

    def rmax(x):
        return jnp.max(x, axis=1, keepdims=True)

    def rmin(x):
        return jnp.min(x, axis=1, keepdims=True)

    def rsum(x):
        return jnp.sum(x, axis=1, keepdims=True)

    gs = jnp.zeros(shape, F32)
    for g in range(N_GROUPS):
        in_g = grp == float(g)
        m = jnp.where(in_g, biased, neg)
        m1 = rmax(m)
        i1 = rmin(jnp.where(m == m1, lane, big))
        m2 = rmax(jnp.where(lane == i1, neg, m))
        gs = jnp.where(in_g, m1 + m2, gs)
    sel_g = jnp.zeros(shape, jnp.bool_)
    cur = gs
    for _ in range(TOPK_GROUPS):
        mx = rmax(cur)
        gi = rmin(jnp.where(cur == mx, grp, big))
        hit = grp == gi
        sel_g = sel_g | hit
        cur = jnp.where(hit, neg, cur)
    masked = jnp.where(sel_g, biased, neg)
    ids, ws = [], []
    sel = jnp.zeros(shape, F32)
    for _ in range(TOP_K):
        mx = rmax(masked)
        ik = rmin(jnp.where(masked == mx, lane, big))
        hit = lane == ik
        ws.append(rsum(jnp.where(hit, scores, 0.0)))
        ids.append(ik)
        sel = jnp.where(hit, 1.0, sel)
        masked = jnp.where(hit, neg, masked)
    w = jnp.concatenate(ws, axis=1)
    wts_ref[...] = w / rsum(w) * ROUTE_SCALE
    idx_ref[...] = jnp.concatenate(ids, axis=1).astype(I32)

    earlier = (lax.broadcasted_iota(I32, (tm, tm), 1) < lax.broadcasted_iota(I32, (tm, tm), 0))
    before = _dot(earlier.astype(BF16), sel.astype(BF16)) + carry[...]
    ranks = [rsum(jnp.where(lane == ik, before, 0.0)) for ik in ids]
    rank_ref[...] = jnp.concatenate(ranks, axis=1).astype(I32)
    carry[...] = carry[...] + jnp.sum(sel, axis=0, keepdims=True)
    cnt_ref[...] = carry[...]


def _route_call(h, w_router, bias):
    t, d = h.shape
    tm = ROUTE_TM
    tok = pl.BlockSpec((tm, TOP_K), lambda i: (i, 0))
    return pl.pallas_call(
        _route_kernel,
        grid=(t // tm,),
        in_specs=[pl.BlockSpec((tm, d), lambda i: (i, 0)),
                  pl.BlockSpec((d, N_EXPERTS), lambda i: (0, 0)),
                  pl.BlockSpec((1, N_EXPERTS), lambda i: (0, 0))],
        out_specs=(tok, tok, tok, pl.BlockSpec((1, N_EXPERTS), lambda i: (0, 0))),
        out_shape=(jax.ShapeDtypeStruct((t, TOP_K), I32), jax.ShapeDtypeStruct((t, TOP_K), F32),
                   jax.ShapeDtypeStruct((t, TOP_K), I32), jax.ShapeDtypeStruct((1, N_EXPERTS), F32)),
        scratch_shapes=[pltpu.VMEM((1, N_EXPERTS), F32)],
        compiler_params=_params(("arbitrary",)),
        name="route",
    )(h, w_router, bias)


def _dest_kernel(idx_ref, rank_ref, start_ref, dest_ref):
    idx = idx_ref[...]
    shape = (idx.shape[0], N_EXPERTS)
    lane = lax.broadcasted_iota(I32, shape, 1)
    start = start_ref[...]
    cols = []
    for kk in range(TOP_K):
        hit = lane == idx[:, kk:kk + 1]
        cols.append(jnp.sum(jnp.where(hit, start, 0.0), axis=1, keepdims=True))
    dest_ref[...] = jnp.concatenate(cols, axis=1).astype(I32) + rank_ref[...]


def _dest_call(idx, rank, start_f32):
    t = idx.shape[0]
    tm = 512
    tok = pl.BlockSpec((tm, TOP_K), lambda i: (i, 0))
    return pl.pallas_call(
        _dest_kernel,
        grid=(t // tm,),
        in_specs=[tok, tok, pl.BlockSpec((1, N_EXPERTS), lambda i: (0, 0))],
        out_specs=tok,
        out_shape=jax.ShapeDtypeStruct((t, TOP_K), I32),
        compiler_params=_params(("parallel",)),
        name="dest",
    )(idx, rank, start_f32)


def _row_copy(src, src_row, dst, dst_row, sem):
    return pltpu.make_async_copy(src.at[pl.ds(src_row, 1), :], dst.at[pl.ds(dst_row, 1), :], sem)


def _scatter_kernel(dest_ref, h_ref, xs_hbm, sem):
    n = SCATTER_TM * TOP_K

    def issue(a, c):
        tok = lax.shift_right_logical(a, TOP_K.bit_length() - 1)
        _row_copy(h_ref, tok, xs_hbm, dest_ref[0, 0, a], sem.at[a]).start()
        return c

    lax.fori_loop(0, n, issue, 0, unroll=8)

    def wait(a, c):
        _row_copy(h_ref, 0, xs_hbm, 0, sem.at[a]).wait()
        return c

    lax.fori_loop(0, n, wait, 0, unroll=8)


def _scatter_call(dest, h):
    t, d = h.shape
    tm = SCATTER_TM
    n = tm * TOP_K
    return pl.pallas_call(
        _scatter_kernel,
        grid=(t // tm,),
        in_specs=[pl.BlockSpec((1, 1, n), lambda i: (i, 0, 0), memory_space=pltpu.SMEM),
                  pl.BlockSpec((tm, d), lambda i: (i, 0))],
        out_specs=pl.BlockSpec(memory_space=pl.ANY),
        out_shape=jax.ShapeDtypeStruct((t * TOP_K, d), F32),
        scratch_shapes=[pltpu.SemaphoreType.DMA((n,))],
        compiler_params=_params(("arbitrary",)),
        name="dispatch",
    )(dest.reshape(t // tm, 1, n), h)


def _weight_copies(w_hbm, w32, wsem, e, slot):
    return [pltpu.make_async_copy(w_hbm[m].at[e], w32[m].at[slot], wsem.at[slot, m])
            for m in range(3)]


def _expert_kernel(vb_ref, ve_ref, lo_ref, hi_ref, fb_ref, fe_ref, par_ref, nxt_ref, nv_ref,
                   x_ref, wg_hbm, wu_hbm, wd_hbm, y_ref,
                   wg32, wu32, wd32, wg16, wu16, wd16, wsem):
    v = pl.program_id(0)
    w_hbm = (wg_hbm, wu_hbm, wd_hbm)
    w32 = (wg32, wu32, wd32)
    w16 = (wg16, wu16, wd16)

    @pl.when(v < nv_ref[0])
    def _():
        slot = par_ref[v]

        @pl.when(fe_ref[v] == 1)
        def _():
            @pl.when(v == 0)
            def _():
                for cp in _weight_copies(w_hbm, w32, wsem, ve_ref[v], slot):
                    cp.start()

            @pl.when(nxt_ref[v] >= 0)
            def _():
                for cp in _weight_copies(w_hbm, w32, wsem, nxt_ref[v], 1 - slot):
                    cp.start()

            for m, cp in enumerate(_weight_copies(w_hbm, w32, wsem, ve_ref[v], slot)):
                cp.wait()
                w16[m][...] = w32[m][slot].astype(BF16)

        x = x_ref[...].astype(BF16)
        g = _dot(x, wg16[...])
        u = _dot(x, wu16[...])
        y = _dot((_silu(g) * u).astype(BF16), wd16[...])
        rows = lax.broadcasted_iota(I32, y.shape, 0)
        mine = (rows >= lo_ref[v]) & (rows < hi_ref[v])

        @pl.when(fb_ref[v] == 1)
        def _():
            y_ref[...] = jnp.where(mine, y, 0.0)

        @pl.when(fb_ref[v] == 0)
        def _():
            y_ref[...] = jnp.where(mine, y, y_ref[...])


def _expert_call(tables, xs, w_gate, w_up, w_down):
    n_rows, d = xs.shape
    blk = EXPERT_BLOCK
    n_visits = tables[0].shape[0]
    any_spec = pl.BlockSpec(memory_space=pl.ANY)
    grid_spec = pltpu.PrefetchScalarGridSpec(
        num_scalar_prefetch=len(tables),
        grid=(n_visits,),
        in_specs=[pl.BlockSpec((blk, d), lambda v, vb, *_: (vb[v], 0)), any_spec, any_spec, any_spec],
        out_specs=pl.BlockSpec((blk, d), lambda v, vb, *_: (vb[v], 0)),
        scratch_shapes=[
            pltpu.VMEM((2, d, D_EXPERT), F32), pltpu.VMEM((2, d, D_EXPERT), F32),
            pltpu.VMEM((2, D_EXPERT, d), F32),
            pltpu.VMEM((d, D_EXPERT), BF16), pltpu.VMEM((d, D_EXPERT), BF16),
            pltpu.VMEM((D_EXPERT, d), BF16),
            pltpu.SemaphoreType.DMA((2, 3)),
        ],
    )
    return pl.pallas_call(
        _expert_kernel,
        grid_spec=grid_spec,
        out_shape=jax.ShapeDtypeStruct((n_rows, d), F32),
        compiler_params=_params(("arbitrary",)),
        name="experts",
    )(*tables, xs, w_gate, w_up, w_down)


def _visit_tables(counts, n_rows):
    blk = EXPERT_BLOCK
    n_visits = n_rows // blk + N_EXPERTS - 1
    end = jnp.cumsum(counts)
    start = end - counts
    nonempty = counts > 0
    first_blk = start // blk
    last_blk = jnp.maximum(end - 1, 0) // blk
    nvis = jnp.where(nonempty, last_blk - first_blk + 1, 0)
    vis_end = jnp.cumsum(nvis)
    vis_start = vis_end - nvis
    n_used = vis_end[-1]
    v = jnp.arange(n_visits, dtype=I32)
    vc = jnp.minimum(v, n_used - 1)
    ve = jnp.minimum(jnp.searchsorted(vis_end, vc, side='right'), N_EXPERTS - 1).astype(I32)
    vb = first_blk[ve] + (vc - vis_start[ve])
    lo = jnp.clip(start[ve] - vb * blk, 0, blk)
    hi = jnp.clip(end[ve] - vb * blk, 0, blk)
    fb = jnp.concatenate([jnp.ones((1,), I32), (vb[1:] != vb[:-1]).astype(I32)])
    fe = (vc == vis_start[ve]).astype(I32)
    ordinal = jnp.cumsum(nonempty.astype(I32)) - 1
    par = ordinal[ve] % 2
    ids = jnp.where(nonempty, jnp.arange(N_EXPERTS, dtype=I32), N_EXPERTS)
    nxt_incl = lax.cummin(ids, axis=0, reverse=True)
    nxt_e = jnp.concatenate([nxt_incl[1:], jnp.full((1,), N_EXPERTS, I32)])
    nxt_e = jnp.where(nxt_e >= N_EXPERTS, -1, nxt_e)
    cast = lambda a: a.astype(I32)
    tables = (cast(vb), ve, cast(lo), cast(hi), fb, fe, cast(par), cast(nxt_e[ve]),
              cast(n_used).reshape(1))
    return tables, start


def _shared_kernel(h_ref, wg_ref, wu_ref, wd_ref, y_ref, wg16, wu16, wd16):
    @pl.when(pl.program_id(0) == 0)
    def _():
        wg16[...] = wg_ref[...].astype(BF16)
        wu16[...] = wu_ref[...].astype(BF16)
        wd16[...] = wd_ref[...].astype(BF16)

    x = h_ref[...].astype(BF16)
    g = _dot(x, wg16[...])
    u = _dot(x, wu16[...])
    y_ref[...] = _dot((_silu(g) * u).astype(BF16), wd16[...])


def _shared_call(h, wg, wu, wd):
    t, d = h.shape
    ds = wg.shape[1]
    tm = 512
    return pl.pallas_call(
        _shared_kernel,
        grid=(t // tm,),
        in_specs=[pl.BlockSpec((tm, d), lambda i: (i, 0)),
                  pl.BlockSpec((d, ds), lambda i: (0, 0)),
                  pl.BlockSpec((d, ds), lambda i: (0, 0)),
                  pl.BlockSpec((ds, d), lambda i: (0, 0))],
        out_specs=pl.BlockSpec((tm, d), lambda i: (i, 0)),
        out_shape=jax.ShapeDtypeStruct((t, d), F32),
        scratch_shapes=[pltpu.VMEM((d, ds), BF16), pltpu.VMEM((d, ds), BF16),
                        pltpu.VMEM((ds, d), BF16)],
        compiler_params=_params(("arbitrary",)),
        name="shared_expert",
    )(h, wg, wu, wd)


def _combine_kernel(dest_ref, dnext_ref, ys_hbm, wts_ref, h_ref, ysh_ref, g_ref, b_ref, o_ref,
                    buf, sem):
    tm = COMBINE_TM
    n = tm * TOP_K
    i = pl.program_id(0)
    slot = i % 2

    def gather(d_ref, s):
        def issue(a, c):
            _row_copy(ys_hbm, d_ref[0, 0, a], buf.at[s], a, sem.at[s, a]).start()
            return c

        lax.fori_loop(0, n, issue, 0, unroll=8)

    @pl.when(i == 0)
    def _():
        gather(dest_ref, slot)

    @pl.when(i + 1 < pl.num_programs(0))
    def _():
        gather(dnext_ref, 1 - slot)

    acc = DN_ALPHA * h_ref[...] + ysh_ref[...]
    wts = wts_ref[...]

    def wait(a, c):
        _row_copy(ys_hbm, 0, buf.at[slot], a, sem.at[slot, a]).wait()
        return c

    lax.fori_loop(0, n, wait, 0, unroll=8)
    for kk in range(TOP_K):
        acc = acc + wts[:, kk:kk + 1] * buf[slot, kk * tm:(kk + 1) * tm, :]
    o_ref[...] = _layer_norm(acc, g_ref[...], b_ref[...])


def _combine_call(dest_km, ys, wts, h, ysh, g, b):
    t, d = h.shape
    tm = COMBINE_TM
    n = tm * TOP_K
    n_tiles = t // tm
    vec = pl.BlockSpec((1, d), lambda i: (0, 0))
    return pl.pallas_call(
        _combine_kernel,
        grid=(n_tiles,),
        in_specs=[
            pl.BlockSpec((1, 1, n), lambda i: (i, 0, 0), memory_space=pltpu.SMEM),
            pl.BlockSpec((1, 1, n), lambda i: (jnp.minimum(i + 1, n_tiles - 1), 0, 0),
                         memory_space=pltpu.SMEM),
            pl.BlockSpec(memory_space=pl.ANY),
            pl.BlockSpec((tm, TOP_K), lambda i: (i, 0)),
            pl.BlockSpec((tm, d), lambda i: (i, 0)),
            pl.BlockSpec((tm, d), lambda i: (i, 0)),
            vec, vec,
        ],
        out_specs=pl.BlockSpec((tm, d), lambda i: (i, 0)),
        out_shape=jax.ShapeDtypeStruct((t, d), F32),
        scratch_shapes=[pltpu.VMEM((2, n, d), F32), pltpu.SemaphoreType.DMA((2, n))],
        compiler_params=_params(("arbitrary",)),
        name="combine_ln2",
    )(dest_km, dest_km, ys, wts, h, ysh, g, b)


def _token_mixing(x, w_in, w_short_conv, a_log, dt_bias, delta_norm_w, w_o_delta, conv_dw_w,
                  conv_dw_b, conv_ln_g, conv_ln_b, w_pw2, b_pw2, w_out, ln1_g, ln1_b):
    t, d = x.shape
    x16 = x.astype(BF16)
    row = lambda a: a.reshape(1, -1)
    tile = pl.BlockSpec((TM, TN), lambda j, i: (i, j))
    conv_buf = pltpu.VMEM((TM + SUBLANES, TN), F32)

    scale = jnp.concatenate([jnp.full((1, QK_WIDTH), HEAD ** -0.5, F32), jnp.ones((1, QK_WIDTH), F32)], axis=1)
    qk = _proj_call(
        "proj_qk", x16, [w_in], [0], 2 * QK_WIDTH // TN, _qk_epilogue,
        [w_short_conv, scale],
        [pl.BlockSpec((SHORT_CONV, TN), lambda j, i: (0, j)), pl.BlockSpec((1, TN), lambda j, i: (0, j))],
        jax.ShapeDtypeStruct((t, 2 * QK_WIDTH), F32), tile, scratch=[conv_buf])
    v_off = 2 * QK_WIDTH // TN
    v = _proj_call(
        "proj_v", x16, [w_in], [v_off], V_WIDTH // TN, _v_epilogue,
        [w_short_conv], [pl.BlockSpec((SHORT_CONV, TN), lambda j, i: (0, j + v_off))],
        jax.ShapeDtypeStruct((t, V_WIDTH), F32), tile, scratch=[conv_buf])
    z = _proj_call(
        "proj_z", x16, [w_in], [COL_Z // TN], V_WIDTH // TN, _z_epilogue, [], [],
        jax.ShapeDtypeStruct((t, V_WIDTH), F32), tile)

    w_ab = jnp.pad(w_in[:, COL_AB:COL_GLU], ((0, 0), (0, LANES - 2 * N_V_HEADS)))
    pad_h = lambda a: jnp.pad(a.reshape(1, -1), ((0, 0), (0, LANES - N_V_HEADS)))
    lane_vec = pl.BlockSpec((1, LANES), lambda j, i: (0, 0))
    lane_tile = pl.BlockSpec((TM, LANES), lambda j, i: (i, 0))
    lane_shape = jax.ShapeDtypeStruct((t, LANES), F32)
    gc, bt, gt = _proj_call(
        "proj_ab", x16, [w_ab], [0], 1, _ab_epilogue, [pad_h(a_log), pad_h(dt_bias)],
        [lane_vec, lane_vec], (lane_shape, lane_shape, lane_shape), (lane_tile, lane_tile, lane_tile),
        tn=LANES)

    w_glu = w_in[:, COL_GLU:COL_GATES].astype(BF16)
    w_gates = w_in[:, COL_GATES:].astype(BF16)
    c = _proj_call(
        "proj_glu", x16, [w_glu, w_glu], [0, d // TN], d // TN, _glu_epilogue, [], [],
        jax.ShapeDtypeStruct((t, d), F32), tile)
    gates = _proj_call(
        "proj_gates", x16, [w_gates], [0], 2 * d // TN, _gates_epilogue, [], [],
        jax.ShapeDtypeStruct((t, 2 * d), F32), tile)

    r = GDN_ROWS
    gcrow = gc[:, :N_V_HEADS].T.reshape(N_V_HEADS, t // r, 1, r)
    gt_chunk = gt[CHUNK - 1::CHUNK, :N_V_HEADS].T
    gtrow = jnp.broadcast_to(gt_chunk[:, :, None], (N_V_HEADS, t // CHUNK, LANES))
    gtrow = gtrow.reshape(N_V_HEADS, t // r, r // CHUNK, LANES)
    og = _gdn_call(qk, v, z, gc, bt, gt, gcrow, gtrow, row(delta_norm_w))

    ya = _proj_call(
        "proj_odelta", og, [w_o_delta], [0], d // TN, _odelta_epilogue, [gates], [tile],
        jax.ShapeDtypeStruct((t, d), F32), tile)

    c_act = _dwconv_call(c, conv_dw_w, row(conv_dw_b), row(conv_ln_g), row(conv_ln_b))
    g_off = d // TN
    mixed = _proj_call(
        "proj_pw2", c_act, [w_pw2], [0], d // TN, _pw2_epilogue, [row(b_pw2), gates, ya],
        [pl.BlockSpec((1, TN), lambda j, i: (0, j)),
         pl.BlockSpec((TM, TN), lambda j, i: (i, j + g_off)), tile],
        jax.ShapeDtypeStruct((t, d), BF16), tile)
    return _outproj_call(mixed, w_out.astype(BF16), x, row(ln1_g), row(ln1_b))


def _moe(h, w_router, router_bias, w_gate, w_up, w_down, w_sh_gate, w_sh_up, w_sh_down, ln2_g, ln2_b):
    t, d = h.shape
    idx, wts, rank, counts = _route_call(h, w_router, router_bias.reshape(1, -1))
    tables, start = _visit_tables(counts.reshape(-1).astype(I32), t * TOP_K)
    dest = _dest_call(idx, rank, start.astype(F32).reshape(1, -1))
    xs = _scatter_call(dest, h)
    ys = _expert_call(tables, xs, w_gate, w_up, w_down)
    ysh = _shared_call(h, w_sh_gate, w_sh_up, w_sh_down)
    tm = COMBINE_TM
    dest_km = dest.reshape(t // tm, tm, TOP_K).transpose(0, 2, 1).reshape(t // tm, 1, tm * TOP_K)
    return _combine_call(dest_km, ys, wts, h, ysh, ln2_g.reshape(1, -1), ln2_b.reshape(1, -1))


def kernel(x, w_in, w_short_conv, a_log, dt_bias, delta_norm_w, w_o_delta, conv_dw_w, conv_dw_b,
           conv_ln_g, conv_ln_b, w_pw2, b_pw2, w_out, ln1_g, ln1_b, w_router, router_bias, w_gate,
           w_up, w_down, w_sh_gate, w_sh_up, w_sh_down, ln2_g, ln2_b):
    batch, seq, d = x.shape
    depth = w_in.shape[0]
    outs = []
    for bi in range(batch):
        h = x[bi]
        for li in range(depth):
            h = _token_mixing(h, w_in[li], w_short_conv[li], a_log[li], dt_bias[li], delta_norm_w[li],
                              w_o_delta[li], conv_dw_w[li], conv_dw_b[li], conv_ln_g[li], conv_ln_b[li],
                              w_pw2[li], b_pw2[li], w_out[li], ln1_g[li], ln1_b[li])
            h = _moe(h, w_router[li], router_bias[li], w_gate[li], w_up[li], w_down[li],
                     w_sh_gate[li], w_sh_up[li], w_sh_down[li], ln2_g[li], ln2_b[li])
        outs.append(h)
    return jnp.stack(outs, axis=0)
```

```python
import jax
import jax.numpy as jnp
from jax import lax
from jax.experimental import pallas as pl
from jax.experimental.pallas import tpu as pltpu

F32 = jnp.float32
BF16 = jnp.bfloat16
I32 = jnp.int32

D_MODEL = 2048
CHUNK = 64
N_QK_HEADS = 16
N_V_HEADS = 32
HEAD = 128
QK_WIDTH = N_QK_HEADS * HEAD
V_WIDTH = N_V_HEADS * HEAD
SHORT_CONV = 4
CONV_WIDTH = 31
N_EXPERTS = 256
TOP_K = 8
N_GROUPS = 8
TOPK_GROUPS = 4
D_EXPERT = 512
ROUTE_SCALE = 2.5
DN_ALPHA = 2.0 ** 0.25
LN_EPS = 1e-5
NORM_EPS = 1e-6
COL_Z = 2 * QK_WIDTH + V_WIDTH
COL_AB = COL_Z + V_WIDTH
COL_GLU = COL_AB + 2 * N_V_HEADS
COL_GATES = COL_GLU + 2 * D_MODEL

LANES = 128
SUBLANES = 8
VMEM_LIMIT = 56 * 1024 * 1024
TM = 512
TN = 512
GDN_ROWS = 256
GDN_SUB = 128
GDN_PAIRS = 2
CONV_TM = 256
HALO = 32
CONV_RC = 64
CONV_CC = 256
REPACK_TN = 512
ROUTE_TM = 256
EXPERT_BLOCK = 128
SCATTER_TM = 64
COMBINE_TM = 32


def _params(sem, vmem=VMEM_LIMIT):
    return pltpu.CompilerParams(dimension_semantics=sem, vmem_limit_bytes=vmem)


def _sigmoid(x):
    return jax.nn.sigmoid(x)


def _silu(x):
    return x * jax.nn.sigmoid(x)


def _softplus(x):
    return jnp.maximum(x, 0.0) + jnp.log1p(jnp.exp(-jnp.abs(x)))


def _dot(a, b):
    return jnp.dot(a, b, preferred_element_type=F32)


def _proj_call(name, x, ws, w_offs, n_tiles, epilogue, extras, extra_specs, out_shapes, out_specs,
               scratch=(), tm=TM, tn=TN):
    m, k = x.shape
    single_out = not isinstance(out_shapes, (tuple, list))
    if single_out:
        out_shapes, out_specs = (out_shapes,), (out_specs,)
    nw, ne, no = len(ws), len(extras), len(out_shapes)
    needs_cast = [w.dtype != BF16 for w in ws]

    def body(*refs):
        x_ref = refs[0]
        w_refs = refs[1:1 + nw]
        ex_refs = refs[1 + nw:1 + nw + ne]
        out_refs = refs[1 + nw + ne:1 + nw + ne + no]
        scr = refs[1 + nw + ne + no:]
        wb_refs = scr[:sum(needs_cast)]
        rest = scr[sum(needs_cast):]
        i = pl.program_id(1)
        wsrc, c = [], 0
        for kk in range(nw):
            if needs_cast[kk]:
                wsrc.append(wb_refs[c])
                c += 1
            else:
                wsrc.append(w_refs[kk])

        @pl.when(i == 0)
        def _():
            cc = 0
            for kk in range(nw):
                if needs_cast[kk]:
                    wb_refs[cc][...] = w_refs[kk][...].astype(BF16)
                    cc += 1

        xv = x_ref[...]
        accs = [_dot(xv, wr[...]) for wr in wsrc]
        epilogue(i, accs, ex_refs, out_refs, rest)

    in_specs = [pl.BlockSpec((tm, k), lambda j, i: (i, 0))]
    for off in w_offs:
        in_specs.append(pl.BlockSpec((k, tn), lambda j, i, off=off: (0, j + off)))
    in_specs += list(extra_specs)
    scratch_shapes = [pltpu.VMEM((k, tn), BF16) for c in needs_cast if c] + list(scratch)
    res = pl.pallas_call(
        body,
        grid=(n_tiles, m // tm),
        in_specs=in_specs,
        out_specs=tuple(out_specs),
        out_shape=tuple(out_shapes),
        scratch_shapes=scratch_shapes,
        compiler_params=_params(("parallel", "arbitrary")),
        name=name,
    )(x, *ws, *extras)
    return res[0] if single_out else res


def _short_conv_silu(i, acc, cw_ref, buf, tm):
    @pl.when(i == 0)
    def _():
        buf[0:SUBLANES, :] = jnp.zeros((SUBLANES, buf.shape[1]), F32)

    buf[SUBLANES:SUBLANES + tm, :] = acc
    cw = cw_ref[...]
    y = acc * cw[SHORT_CONV - 1:SHORT_CONV, :]
    for s in range(SHORT_CONV - 1):
        y = y + buf[pl.ds(SUBLANES - (SHORT_CONV - 1) + s, tm), :] * cw[s:s + 1, :]
    buf[0:SUBLANES, :] = buf[tm:tm + SUBLANES, :]
    return _silu(y)


def _qk_epilogue(i, accs, ex, outs, scr):
    cw_ref, scale_ref = ex
    (buf,) = scr
    y = _short_conv_silu(i, accs[0], cw_ref, buf, TM)
    parts = []
    for g in range(TN // HEAD):
        yg = y[:, g * HEAD:(g + 1) * HEAD]
        parts.append(yg * lax.rsqrt(jnp.sum(yg * yg, axis=-1, keepdims=True) + NORM_EPS))
    outs[0][...] = jnp.concatenate(parts, axis=1) * scale_ref[...]


def _v_epilogue(i, accs, ex, outs, scr):
    (cw_ref,) = ex
    (buf,) = scr
    outs[0][...] = _short_conv_silu(i, accs[0], cw_ref, buf, TM)


def _z_epilogue(i, accs, ex, outs, scr):
    outs[0][...] = accs[0]


def _glu_epilogue(i, accs, ex, outs, scr):
    outs[0][...] = accs[0] * _sigmoid(accs[1])


def _gates_epilogue(i, accs, ex, outs, scr):
    outs[0][...] = _sigmoid(accs[0])


def _ab_epilogue(i, accs, ex, outs, scr):
    alog_ref, dtb_ref = ex
    gc_ref, bt_ref, gt_ref = outs
    acc = accs[0]
    g = -jnp.exp(alog_ref[...]) * _softplus(acc + dtb_ref[...])
    row = lax.broadcasted_iota(I32, g.shape, 0) % CHUNK
    s = 1
    while s < CHUNK:
        g = g + jnp.where(row >= s, pltpu.roll(g, s, axis=0), 0.0)
        s *= 2
    n_chunks = g.shape[0] // CHUNK
    tot = g.reshape(n_chunks, CHUNK, LANES)[:, CHUNK - 1:CHUNK, :]
    gc_ref[...] = g
    bt_ref[...] = _sigmoid(acc)
    gt_ref[...] = jnp.broadcast_to(tot, (n_chunks, CHUNK, LANES)).reshape(g.shape)


def _odelta_epilogue(i, accs, ex, outs, scr):
    (gate_ref,) = ex
    outs[0][...] = accs[0] * gate_ref[...]


def _pw2_epilogue(i, accs, ex, outs, scr):
    bias_ref, gate_ref, ya_ref = ex
    outs[0][...] = (ya_ref[...] + gate_ref[...] * (accs[0] + bias_ref[...])).astype(BF16)


def _repack_kernel(a_ref, b_ref, o_ref):
    half = LANES // 2
    o_ref[...] = jnp.concatenate([a_ref[:, half:], b_ref[:, :half]], axis=1).astype(BF16)


def _repack_call(w_in, n_cols):
    k = w_in.shape[0]
    tn = REPACK_TN
    first = COL_AB // tn
    return pl.pallas_call(
        _repack_kernel,
        grid=(n_cols // tn,),
        in_specs=[pl.BlockSpec((k, tn), lambda j: (0, first + j)),
                  pl.BlockSpec((k, tn), lambda j: (0, first + j + 1))],
        out_specs=pl.BlockSpec((k, tn), lambda j: (0, j)),
        out_shape=jax.ShapeDtypeStruct((k, n_cols), BF16),
        compiler_params=_params(("parallel",)),
        name="repack_w",
    )(w_in, w_in)


def _gdn_wy(q, k, kk, qk, causal, strict, gcol, bcol, gtcol, grow, v):
    dm = jnp.exp(jnp.where(causal, gcol - grow, -jnp.inf))
    a = jnp.where(strict, bcol * kk * dm, 0.0)
    n = -a
    ap = a
    for _ in range(5):
        ap16 = ap.astype(BF16)
        ap = _dot(ap16, ap16)
        n = n + ap + _dot(n.astype(BF16), ap.astype(BF16))
    eg = jnp.exp(gcol)
    rhs = jnp.concatenate([v * bcol, k * (bcol * eg)], axis=1)
    sol16 = (rhs + _dot(n.astype(BF16), rhs.astype(BF16))).astype(BF16)
    x = _dot((qk * dm).astype(BF16), sol16)
    qp16 = (q * eg - x[:, HEAD:]).astype(BF16)
    kd16 = (k * jnp.exp(gtcol - gcol)).astype(BF16)
    return sol16, x[:, :HEAD], qp16, kd16


def _gdn_head(q, k, kks, qks, causal, strict, gcol, bcol, gtcol, grow, gtrow, v, state):
    tn = (((0,), (0,)), ((), ()))
    per_sub = GDN_SUB // CHUNK
    o_parts = []
    for sb in range(GDN_ROWS // GDN_SUB):
        rs = slice(sb * GDN_SUB, (sb + 1) * GDN_SUB)
        sol16, o_intra, qp16, kd16 = _gdn_wy(q[rs], k[rs], kks[sb], qks[sb], causal, strict, gcol[rs],
                                             bcol[rs], gtcol[rs], grow[:, rs], v[rs])
        for jj in range(per_sub):
            sl = slice(jj * CHUNK, (jj + 1) * CHUNK)
            kc = lax.dot_general(kd16[sl], sol16[sl], tn, preferred_element_type=F32)
            lhs = jnp.concatenate([kc[:, HEAD:].astype(BF16), qp16[sl]], axis=0)
            ks_qs = _dot(lhs, state.astype(BF16))
            o_parts.append(ks_qs[HEAD:] + o_intra[sl])
            j = sb * per_sub + jj
            state = state * jnp.exp(gtrow[j:j + 1, :]) + kc[:, :HEAD] - ks_qs[:HEAD]
    return jnp.concatenate(o_parts, axis=0), state


def _gdn_kernel(q_ref, k_ref, v_ref, z_ref, gc_ref, bt_ref, gt_ref, gcrow_ref, gtrow_ref, nw_ref,
                o_ref, s_ref):
    pp = pl.program_id(0)
    tb = pl.program_id(1)
    r = GDN_ROWS

    @pl.when(tb == 0)
    def _():
        s_ref[...] = jnp.zeros(s_ref.shape, F32)

    row = lax.broadcasted_iota(I32, (GDN_SUB, GDN_SUB), 0)
    col = lax.broadcasted_iota(I32, (GDN_SUB, GDN_SUB), 1)
    same = (row // CHUNK) == (col // CHUNK)
    causal = same & (col <= row)
    strict = same & (col < row)
    lane = lax.broadcasted_iota(I32, (r, LANES), 1)
    gc_all = gc_ref[...]
    bt_all = bt_ref[...]
    gt_all = gt_ref[...]
    nt = (((1,), (1,)), ((), ()))

    for pi in range(GDN_PAIRS):
        q = q_ref[:, pi * HEAD:(pi + 1) * HEAD]
        k = k_ref[:, pi * HEAD:(pi + 1) * HEAD]
        k16 = k.astype(BF16)
        q16 = q.astype(BF16)
        kk, qk = [], []
        for sb in range(r // GDN_SUB):
            rs = slice(sb * GDN_SUB, (sb + 1) * GDN_SUB)
            kk.append(lax.dot_general(k16[rs], k16[rs], nt, preferred_element_type=F32))
            qk.append(lax.dot_general(q16[rs], k16[rs], nt, preferred_element_type=F32))
        for hh in range(2):
            hl = 2 * pi + hh
            h = 2 * GDN_PAIRS * pp + hl
            gcol = jnp.sum(jnp.where(lane == h, gc_all, 0.0), axis=1, keepdims=True)
            bcol = jnp.sum(jnp.where(lane == h + N_V_HEADS, bt_all, 0.0), axis=1, keepdims=True)
            gtcol = jnp.sum(jnp.where(lane == h, gt_all, 0.0), axis=1, keepdims=True)
            cols = slice(hl * HEAD, (hl + 1) * HEAD)
            o, state = _gdn_head(q, k, kk, qk, causal, strict, gcol, bcol, gtcol,
                                 gcrow_ref[hl, 0], gtrow_ref[hl, 0], v_ref[:, cols], s_ref[hl])
            s_ref[hl] = state
            o = o * lax.rsqrt(jnp.mean(o * o, axis=-1, keepdims=True) + NORM_EPS) * nw_ref[...]
            o_ref[:, cols] = (o * _silu(z_ref[:, cols])).astype(o_ref.dtype)


def _gdn_call(qk, v, z, gc, bt, gt, gcrow, gtrow, nw):
    t = qk.shape[0]
    r = GDN_ROWS
    p = GDN_PAIRS
    k_off = N_QK_HEADS // p
    return pl.pallas_call(
        _gdn_kernel,
        grid=(N_QK_HEADS // p, t // r),
        in_specs=[
            pl.BlockSpec((r, p * HEAD), lambda pp, tb: (tb, pp)),
            pl.BlockSpec((r, p * HEAD), lambda pp, tb: (tb, k_off + pp)),
            pl.BlockSpec((r, 2 * p * HEAD), lambda pp, tb: (tb, pp)),
            pl.BlockSpec((r, 2 * p * HEAD), lambda pp, tb: (tb, pp)),
            pl.BlockSpec((r, LANES), lambda pp, tb: (tb, 0)),
            pl.BlockSpec((r, LANES), lambda pp, tb: (tb, 0)),
            pl.BlockSpec((r, LANES), lambda pp, tb: (tb, 0)),
            pl.BlockSpec((2 * p, 1, 1, r), lambda pp, tb: (pp, tb, 0, 0)),
            pl.BlockSpec((2 * p, 1, r // CHUNK, LANES), lambda pp, tb: (pp, tb, 0, 0)),
            pl.BlockSpec((1, HEAD), lambda pp, tb: (0, 0)),
        ],
        out_specs=pl.BlockSpec((r, 2 * p * HEAD), lambda pp, tb: (tb, pp)),
        out_shape=jax.ShapeDtypeStruct((t, V_WIDTH), BF16),
        scratch_shapes=[pltpu.VMEM((2 * p, HEAD, HEAD), F32)],
        compiler_params=_params(("parallel", "arbitrary")),
        name="gdn",
    )(qk, qk, v, z, gc, bt, gt, gcrow, gtrow, nw)


def _layer_norm(r, g, b):
    mu = jnp.mean(r, axis=-1, keepdims=True)
    rc = r - mu
    var = jnp.mean(rc * rc, axis=-1, keepdims=True)
    return rc * lax.rsqrt(var + LN_EPS) * g + b


def _dwconv_kernel(c_ref, w_ref, b_ref, g_ref, beta_ref, o_ref, buf, shifted, accbuf):
    i = pl.program_id(0)
    tm = CONV_TM
    ch = buf.shape[1]
    n_shift = HALO + tm - SUBLANES

    @pl.when(i == 0)
    def _():
        buf[0:HALO, :] = jnp.zeros((HALO, ch), F32)

    buf[HALO:HALO + tm, :] = c_ref[...]
    for b in range(1, SUBLANES):
        shifted[b - 1] = buf[pl.ds(b, n_shift), :]

    def row_body(rc, carry):
        r0 = pl.multiple_of(rc * CONV_RC, CONV_RC)
        for cc in range(ch // CONV_CC):
            cs = slice(cc * CONV_CC, (cc + 1) * CONV_CC)
            acc = buf[pl.ds(HALO + r0, CONV_RC), cs] * w_ref[CONV_WIDTH - 1:CONV_WIDTH, cs] + b_ref[:, cs]
            for s in range(CONV_WIDTH - 1):
                a, b = divmod(HALO - (CONV_WIDTH - 1) + s, SUBLANES)
                if b == 0:
                    src = buf[pl.ds(a * SUBLANES + r0, CONV_RC), cs]
                else:
                    src = shifted[b - 1, pl.ds(a * SUBLANES + r0, CONV_RC), cs]
                acc = acc + src * w_ref[s:s + 1, cs]
            accbuf[pl.ds(r0, CONV_RC), cs] = acc
        return carry

    lax.fori_loop(0, tm // CONV_RC, row_body, 0)
    buf[0:HALO, :] = buf[tm:tm + HALO, :]
    o_ref[...] = _silu(_layer_norm(accbuf[...], g_ref[...], beta_ref[...])).astype(o_ref.dtype)


def _dwconv_call(c, w, b, g, beta):
    t, ch = c.shape
    tm = CONV_TM
    vec = pl.BlockSpec((1, ch), lambda i: (0, 0))
    return pl.pallas_call(
        _dwconv_kernel,
        grid=(t // tm,),
        in_specs=[pl.BlockSpec((tm, ch), lambda i: (i, 0)),
                  pl.BlockSpec((CONV_WIDTH, ch), lambda i: (0, 0)), vec, vec, vec],
        out_specs=pl.BlockSpec((tm, ch), lambda i: (i, 0)),
        out_shape=jax.ShapeDtypeStruct((t, ch), BF16),
        scratch_shapes=[pltpu.VMEM((HALO + tm, ch), F32),
                        pltpu.VMEM((SUBLANES - 1, HALO + tm - SUBLANES, ch), F32),
                        pltpu.VMEM((tm, ch), F32)],
        compiler_params=_params(("arbitrary",)),
        name="dwconv_ln",
    )(c, w, b, g, beta)


def _outproj_kernel(m_ref, w_ref, x_ref, g_ref, b_ref, h_ref):
    y = _dot(m_ref[...], w_ref[...])
    h_ref[...] = _layer_norm(DN_ALPHA * x_ref[...] + y, g_ref[...], b_ref[...])


def _outproj_call(mixed, w16, x, g, b):
    t, d = x.shape
    tm = 256
    vec = pl.BlockSpec((1, d), lambda i: (0, 0))
    return pl.pallas_call(
        _outproj_kernel,
        grid=(t // tm,),
        in_specs=[pl.BlockSpec((tm, d), lambda i: (i, 0)), pl.BlockSpec((d, d), lambda i: (0, 0)),
                  pl.BlockSpec((tm, d), lambda i: (i, 0)), vec, vec],
        out_specs=pl.BlockSpec((tm, d), lambda i: (i, 0)),
        out_shape=jax.ShapeDtypeStruct((t, d), F32),
        compiler_params=_params(("parallel",)),
        name="outproj_ln1",
    )(mixed, w16, x, g, b)


def _route_kernel(h_ref, w_ref, bias_ref, idx_ref, wts_ref, rank_ref, cnt_ref, carry):
    @pl.when(pl.program_id(0) == 0)
    def _():
        carry[...] = jnp.zeros(carry.shape, F32)

    logits = jnp.dot(h_ref[...], w_ref[...], precision=lax.Precision.HIGHEST,
                     preferred_element_type=F32)
    scores = _sigmoid(logits)
    biased = scores + bias_ref[...]
    shape = biased.shape
    tm = shape[0]
    lane_i = lax.broadcasted_iota(I32, shape, 1)
    lane = lane_i.astype(F32)
    per_group = N_EXPERTS // N_GROUPS
    grp = (lane_i // per_group).astype(F32)
    neg = -jnp.inf
    big = 1e9

    def rmax(x):
        return jnp.max(x, axis=1, keepdims=True)

    def rmin(x):
        return jnp.min(x, axis=1, keepdims=True)

    def rsum(x):
        return jnp.sum(x, axis=1, keepdims=True)

    gs = jnp.zeros(shape, F32)
    for g in range(N_GROUPS):
        in_g = grp == float(g)
        m = jnp.where(in_g, biased, neg)
        m1 = rmax(m)
        i1 = rmin(jnp.where(m == m1, lane, big))
        m2 = rmax(jnp.where(lane == i1, neg, m))
        gs = jnp.where(in_g, m1 + m2, gs)
    sel_g = jnp.zeros(shape, jnp.bool_)
    cur = gs
    for _ in range(TOPK_GROUPS):
        mx = rmax(cur)
        gi = rmin(jnp.where(cur == mx, grp, big))
        hit = grp == gi
        sel_g = sel_g | hit
        cur = jnp.where(hit, neg, cur)
    masked = jnp.where(sel_g, biased, neg)
    ids, ws = [], []
    sel = jnp.zeros(shape, F32)
    for _ in range(TOP_K):
        mx = rmax(masked)
        ik = rmin(jnp.where(masked == mx, lane, big))
        hit = lane == ik
        ws.append(rsum(jnp.where(hit, scores, 0.0)))
        ids.append(ik)
        sel = jnp.where(hit, 1.0, sel)
        masked = jnp.where(hit, neg, masked)
    w = jnp.concatenate(ws, axis=1)
    wts_ref[...] = w / rsum(w) * ROUTE_SCALE
    idx_ref[...] = jnp.concatenate(ids, axis=1).astype(I32)

    earlier = (lax.broadcasted_iota(I32, (tm, tm), 1) < lax.broadcasted_iota(I32, (tm, tm), 0))
    before = _dot(earlier.astype(BF16), sel.astype(BF16)) + carry[...]
    ranks = [rsum(jnp.where(lane == ik, before, 0.0)) for ik in ids]
    rank_ref[...] = jnp.concatenate(ranks, axis=1).astype(I32)
    carry[...] = carry[...] + jnp.sum(sel, axis=0, keepdims=True)
    cnt_ref[...] = carry[...]


def _route_call(h, w_router, bias):
    t, d = h.shape
    tm = ROUTE_TM
    tok = pl.BlockSpec((tm, TOP_K), lambda i: (i, 0))
    return pl.pallas_call(
        _route_kernel,
        grid=(t // tm,),
        in_specs=[pl.BlockSpec((tm, d), lambda i: (i, 0)),
                  pl.BlockSpec((d, N_EXPERTS), lambda i: (0, 0)),
                  pl.BlockSpec((1, N_EXPERTS), lambda i: (0, 0))],
        out_specs=(tok, tok, tok, pl.BlockSpec((1, N_EXPERTS), lambda i: (0, 0))),
        out_shape=(jax.ShapeDtypeStruct((t, TOP_K), I32), jax.ShapeDtypeStruct((t, TOP_K), F32),
                   jax.ShapeDtypeStruct((t, TOP_K), I32), jax.ShapeDtypeStruct((1, N_EXPERTS), F32)),
        scratch_shapes=[pltpu.VMEM((1, N_EXPERTS), F32)],
        compiler_params=_params(("arbitrary",)),
        name="route",
    )(h, w_router, bias)


def _dest_kernel(idx_ref, rank_ref, start_ref, dest_ref):
    idx = idx_ref[...]
    shape = (idx.shape[0], N_EXPERTS)
    lane = lax.broadcasted_iota(I32, shape, 1)
    start = start_ref[...]
    cols = []
    for kk in range(TOP_K):
        hit = lane == idx[:, kk:kk + 1]
        cols.append(jnp.sum(jnp.where(hit, start, 0.0), axis=1, keepdims=True))
    dest_ref[...] = jnp.concatenate(cols, axis=1).astype(I32) + rank_ref[...]


def _dest_call(idx, rank, start_f32):
    t = idx.shape[0]
    tm = 512
    tok = pl.BlockSpec((tm, TOP_K), lambda i: (i, 0))
    return pl.pallas_call(
        _dest_kernel,
        grid=(t // tm,),
        in_specs=[tok, tok, pl.BlockSpec((1, N_EXPERTS), lambda i: (0, 0))],
        out_specs=tok,
        out_shape=jax.ShapeDtypeStruct((t, TOP_K), I32),
        compiler_params=_params(("parallel",)),
        name="dest",
    )(idx, rank, start_f32)


def _row_copy(src, src_row, dst, dst_row, sem):
    return pltpu.make_async_copy(src.at[pl.ds(src_row, 1), :], dst.at[pl.ds(dst_row, 1), :], sem)


def _rows_wait(src, dst, dst_row, n_rows, sem):
    pltpu.make_async_copy(src.at[pl.ds(0, n_rows), :], dst.at[pl.ds(dst_row, n_rows), :], sem).wait()


def _scatter_kernel(dest_ref, h_ref, xs_hbm, sem):
    def issue(tok, c):
        for kk in range(TOP_K):
            _row_copy(h_ref, tok, xs_hbm, dest_ref[0, 0, tok * TOP_K + kk], sem.at[tok]).start(priority=kk % 2)
        return c

    lax.fori_loop(0, SCATTER_TM, issue, 0, unroll=2)

    def wait(tok, c):
        _rows_wait(h_ref, xs_hbm, 0, TOP_K, sem.at[tok])
        return c

    lax.fori_loop(0, SCATTER_TM, wait, 0, unroll=8)


def _scatter_call(dest, h):
    t, d = h.shape
    tm = SCATTER_TM
    n = tm * TOP_K
    return pl.pallas_call(
        _scatter_kernel,
        grid=(t // tm,),
        in_specs=[pl.BlockSpec((1, 1, n), lambda i: (i, 0, 0), memory_space=pltpu.SMEM),
                  pl.BlockSpec((tm, d), lambda i: (i, 0))],
        out_specs=pl.BlockSpec(memory_space=pl.ANY),
        out_shape=jax.ShapeDtypeStruct((t * TOP_K, d), F32),
        scratch_shapes=[pltpu.SemaphoreType.DMA((tm,))],
        compiler_params=_params(("arbitrary",)),
        name="dispatch",
    )(dest.reshape(t // tm, 1, n), h)


def _weight_copies(w_hbm, w32, wsem, e, slot):
    return [pltpu.make_async_copy(w_hbm[m].at[e], w32[m].at[slot], wsem.at[slot, m])
            for m in range(3)]


def _expert_kernel(vb_ref, ve_ref, lo_ref, hi_ref, fb_ref, fe_ref, par_ref, nxt_ref, nv_ref,
                   x_ref, wg_hbm, wu_hbm, wd_hbm, y_ref,
                   wg32, wu32, wd32, wg16, wu16, wd16, wsem):
    v = pl.program_id(0)
    w_hbm = (wg_hbm, wu_hbm, wd_hbm)
    w32 = (wg32, wu32, wd32)
    w16 = (wg16, wu16, wd16)

    @pl.when(v < nv_ref[0])
    def _():
        slot = par_ref[v]

        @pl.when(fe_ref[v] == 1)
        def _():
            @pl.when(v == 0)
            def _():
                for cp in _weight_copies(w_hbm, w32, wsem, ve_ref[v], slot):
                    cp.start()

            @pl.when(nxt_ref[v] >= 0)
            def _():
                for cp in _weight_copies(w_hbm, w32, wsem, nxt_ref[v], 1 - slot):
                    cp.start()

            for m, cp in enumerate(_weight_copies(w_hbm, w32, wsem, ve_ref[v], slot)):
                cp.wait()
                w16[m][...] = w32[m][slot].astype(BF16)

        x = x_ref[...].astype(BF16)
        g = _dot(x, wg16[...])
        u = _dot(x, wu16[...])
        y = _dot((_silu(g) * u).astype(BF16), wd16[...])
        rows = lax.broadcasted_iota(I32, y.shape, 0)
        mine = (rows >= lo_ref[v]) & (rows < hi_ref[v])

        @pl.when(fb_ref[v] == 1)
        def _():
            y_ref[...] = jnp.where(mine, y, 0.0)

        @pl.when(fb_ref[v] == 0)
        def _():
            y_ref[...] = jnp.where(mine, y, y_ref[...])


def _expert_call(tables, xs, w_gate, w_up, w_down):
    n_rows, d = xs.shape
    blk = EXPERT_BLOCK
    n_visits = tables[0].shape[0]
    any_spec = pl.BlockSpec(memory_space=pl.ANY)
    grid_spec = pltpu.PrefetchScalarGridSpec(
        num_scalar_prefetch=len(tables),
        grid=(n_visits,),
        in_specs=[pl.BlockSpec((blk, d), lambda v, vb, *_: (vb[v], 0)), any_spec, any_spec, any_spec],
        out_specs=pl.BlockSpec((blk, d), lambda v, vb, *_: (vb[v], 0)),
        scratch_shapes=[
            pltpu.VMEM((2, d, D_EXPERT), F32), pltpu.VMEM((2, d, D_EXPERT), F32),
            pltpu.VMEM((2, D_EXPERT, d), F32),
            pltpu.VMEM((d, D_EXPERT), BF16), pltpu.VMEM((d, D_EXPERT), BF16),
            pltpu.VMEM((D_EXPERT, d), BF16),
            pltpu.SemaphoreType.DMA((2, 3)),
        ],
    )
    return pl.pallas_call(
        _expert_kernel,
        grid_spec=grid_spec,
        out_shape=jax.ShapeDtypeStruct((n_rows, d), F32),
        compiler_params=_params(("arbitrary",)),
        name="experts",
    )(*tables, xs, w_gate, w_up, w_down)


def _visit_tables(counts, n_rows):
    blk = EXPERT_BLOCK
    n_visits = n_rows // blk + N_EXPERTS - 1
    end = jnp.cumsum(counts)
    start = end - counts
    nonempty = counts > 0
    first_blk = start // blk
    last_blk = jnp.maximum(end - 1, 0) // blk
    nvis = jnp.where(nonempty, last_blk - first_blk + 1, 0)
    vis_end = jnp.cumsum(nvis)
    vis_start = vis_end - nvis
    n_used = vis_end[-1]
    ids = jnp.arange(N_EXPERTS, dtype=I32)
    ordinal = jnp.cumsum(nonempty.astype(I32)) - 1
    nxt_incl = lax.cummin(jnp.where(nonempty, ids, N_EXPERTS), axis=0, reverse=True)
    nxt_e = jnp.concatenate([nxt_incl[1:], jnp.full((1,), N_EXPERTS, I32)])
    nxt_e = jnp.where(nxt_e >= N_EXPERTS, -1, nxt_e)
    v = jnp.arange(n_visits, dtype=I32)
    vc = jnp.clip(v, 0, jnp.maximum(n_used - 1, 0))
    ve = jnp.minimum(jnp.sum((vis_end[None, :] <= vc[:, None]).astype(I32), axis=1), N_EXPERTS - 1)
    onehot = ve[:, None] == ids[None, :]
    look = lambda tab: jnp.sum(jnp.where(onehot, tab.astype(I32)[None, :], 0), axis=1)
    v_first = look(vis_start)
    vb = look(first_blk) + (vc - v_first)
    lo = jnp.clip(look(start) - vb * blk, 0, blk)
    hi = jnp.clip(look(end) - vb * blk, 0, blk)
    fb = jnp.concatenate([jnp.ones((1,), I32), (vb[1:] != vb[:-1]).astype(I32)])
    fe = (vc == v_first).astype(I32)
    tables = (vb, ve, lo, hi, fb, fe, look(ordinal % 2), look(nxt_e), n_used.astype(I32).reshape(1))
    return tables, start


def _shared_kernel(h_ref, wg_ref, wu_ref, wd_ref, y_ref, wg16, wu16, wd16):
    @pl.when(pl.program_id(0) == 0)
    def _():
        wg16[...] = wg_ref[...].astype(BF16)
        wu16[...] = wu_ref[...].astype(BF16)
        wd16[...] = wd_ref[...].astype(BF16)

    x = h_ref[...].astype(BF16)
    g = _dot(x, wg16[...])
    u = _dot(x, wu16[...])
    y_ref[...] = _dot((_silu(g) * u).astype(BF16), wd16[...])


def _shared_call(h, wg, wu, wd):
    t, d = h.shape
    ds = wg.shape[1]
    tm = 512
    return pl.pallas_call(
        _shared_kernel,
        grid=(t // tm,),
        in_specs=[pl.BlockSpec((tm, d), lambda i: (i, 0)),
                  pl.BlockSpec((d, ds), lambda i: (0, 0)),
                  pl.BlockSpec((d, ds), lambda i: (0, 0)),
                  pl.BlockSpec((ds, d), lambda i: (0, 0))],
        out_specs=pl.BlockSpec((tm, d), lambda i: (i, 0)),
        out_shape=jax.ShapeDtypeStruct((t, d), F32),
        scratch_shapes=[pltpu.VMEM((d, ds), BF16), pltpu.VMEM((d, ds), BF16),
                        pltpu.VMEM((ds, d), BF16)],
        compiler_params=_params(("arbitrary",)),
        name="shared_expert",
    )(h, wg, wu, wd)


def _combine_kernel(dest_ref, dnext_ref, ys_hbm, wts_ref, h_ref, ysh_ref, g_ref, b_ref, o_ref,
                    buf, sem):
    tm = COMBINE_TM
    i = pl.program_id(0)
    slot = i % 2

    def gather(d_ref, s):
        def issue(r, c):
            for kk in range(TOP_K):
                a = kk * tm + r
                _row_copy(ys_hbm, d_ref[0, 0, a], buf.at[s], a, sem.at[s, kk]).start(priority=kk % 2)
            return c

        lax.fori_loop(0, tm, issue, 0, unroll=2)

    @pl.when(i == 0)
    def _():
        gather(dest_ref, slot)

    @pl.when(i + 1 < pl.num_programs(0))
    def _():
        gather(dnext_ref, 1 - slot)

    acc = DN_ALPHA * h_ref[...] + ysh_ref[...]
    wts = wts_ref[...]
    for kk in range(TOP_K):
        _rows_wait(ys_hbm, buf.at[slot], kk * tm, tm, sem.at[slot, kk])
    for kk in range(TOP_K):
        acc = acc + wts[:, kk:kk + 1] * buf[slot, kk * tm:(kk + 1) * tm, :]
    o_ref[...] = _layer_norm(acc, g_ref[...], b_ref[...])


def _combine_call(dest_km, ys, wts, h, ysh, g, b):
    t, d = h.shape
    tm = COMBINE_TM
    n = tm * TOP_K
    n_tiles = t // tm
    vec = pl.BlockSpec((1, d), lambda i: (0, 0))
    return pl.pallas_call(
        _combine_kernel,
        grid=(n_tiles,),
        in_specs=[
            pl.BlockSpec((1, 1, n), lambda i: (i, 0, 0), memory_space=pltpu.SMEM),
            pl.BlockSpec((1, 1, n), lambda i: (jnp.minimum(i + 1, n_tiles - 1), 0, 0),
                         memory_space=pltpu.SMEM),
            pl.BlockSpec(memory_space=pl.ANY),
            pl.BlockSpec((tm, TOP_K), lambda i: (i, 0)),
            pl.BlockSpec((tm, d), lambda i: (i, 0)),
            pl.BlockSpec((tm, d), lambda i: (i, 0)),
            vec, vec,
        ],
        out_specs=pl.BlockSpec((tm, d), lambda i: (i, 0)),
        out_shape=jax.ShapeDtypeStruct((t, d), F32),
        scratch_shapes=[pltpu.VMEM((2, n, d), F32), pltpu.SemaphoreType.DMA((2, TOP_K))],
        compiler_params=_params(("arbitrary",)),
        name="combine_ln2",
    )(dest_km, dest_km, ys, wts, h, ysh, g, b)


def _token_mixing(x, w_in, w_short_conv, a_log, dt_bias, delta_norm_w, w_o_delta, conv_dw_w,
                  conv_dw_b, conv_ln_g, conv_ln_b, w_pw2, b_pw2, w_out, ln1_g, ln1_b):
    t, d = x.shape
    x16 = x.astype(BF16)
    row = lambda a: a.reshape(1, -1)
    tile = pl.BlockSpec((TM, TN), lambda j, i: (i, j))
    conv_buf = pltpu.VMEM((TM + SUBLANES, TN), F32)

    scale = jnp.concatenate([jnp.full((1, QK_WIDTH), HEAD ** -0.5, F32), jnp.ones((1, QK_WIDTH), F32)], axis=1)
    qk = _proj_call(
        "proj_qk", x16, [w_in], [0], 2 * QK_WIDTH // TN, _qk_epilogue,
        [w_short_conv, scale],
        [pl.BlockSpec((SHORT_CONV, TN), lambda j, i: (0, j)), pl.BlockSpec((1, TN), lambda j, i: (0, j))],
        jax.ShapeDtypeStruct((t, 2 * QK_WIDTH), F32), tile, scratch=[conv_buf])
    v_off = 2 * QK_WIDTH // TN
    v = _proj_call(
        "proj_v", x16, [w_in], [v_off], V_WIDTH // TN, _v_epilogue,
        [w_short_conv], [pl.BlockSpec((SHORT_CONV, TN), lambda j, i: (0, j + v_off))],
        jax.ShapeDtypeStruct((t, V_WIDTH), F32), tile, scratch=[conv_buf])
    z = _proj_call(
        "proj_z", x16, [w_in], [COL_Z // TN], V_WIDTH // TN, _z_epilogue, [], [],
        jax.ShapeDtypeStruct((t, V_WIDTH), F32), tile)

    pad_h = lambda a: jnp.pad(a.reshape(1, -1), ((0, 0), (0, LANES - N_V_HEADS)))
    lane_vec = pl.BlockSpec((1, LANES), lambda j, i: (0, 0))
    lane_tile = pl.BlockSpec((TM, LANES), lambda j, i: (i, 0))
    lane_shape = jax.ShapeDtypeStruct((t, LANES), F32)
    gc, bt, gt = _proj_call(
        "proj_ab", x16, [w_in], [COL_AB // LANES], 1, _ab_epilogue, [pad_h(a_log), pad_h(dt_bias)],
        [lane_vec, lane_vec], (lane_shape, lane_shape, lane_shape), (lane_tile, lane_tile, lane_tile),
        tn=LANES)

    w_rest = _repack_call(w_in, 4 * d)
    c = _proj_call(
        "proj_glu", x16, [w_rest, w_rest], [0, d // TN], d // TN, _glu_epilogue, [], [],
        jax.ShapeDtypeStruct((t, d), F32), tile)
    gates = _proj_call(
        "proj_gates", x16, [w_rest], [2 * d // TN], 2 * d // TN, _gates_epilogue, [], [],
        jax.ShapeDtypeStruct((t, 2 * d), F32), tile)

    r = GDN_ROWS
    gcrow = gc[:, :N_V_HEADS].T.reshape(N_V_HEADS, t // r, 1, r)
    gt_chunk = gt[CHUNK - 1::CHUNK, :N_V_HEADS].T
    gtrow = jnp.broadcast_to(gt_chunk[:, :, None], (N_V_HEADS, t // CHUNK, LANES))
    gtrow = gtrow.reshape(N_V_HEADS, t // r, r // CHUNK, LANES)
    og = _gdn_call(qk, v, z, gc, bt, gt, gcrow, gtrow, row(delta_norm_w))

    ya = _proj_call(
        "proj_odelta", og, [w_o_delta], [0], d // TN, _odelta_epilogue, [gates], [tile],
        jax.ShapeDtypeStruct((t, d), F32), tile)

    c_act = _dwconv_call(c, conv_dw_w, row(conv_dw_b), row(conv_ln_g), row(conv_ln_b))
    g_off = d // TN
    mixed = _proj_call(
        "proj_pw2", c_act, [w_pw2], [0], d // TN, _pw2_epilogue, [row(b_pw2), gates, ya],
        [pl.BlockSpec((1, TN), lambda j, i: (0, j)),
         pl.BlockSpec((TM, TN), lambda j, i: (i, j + g_off)), tile],
        jax.ShapeDtypeStruct((t, d), BF16), tile)
    return _outproj_call(mixed, w_out.astype(BF16), x, row(ln1_g), row(ln1_b))


def _moe(h, w_router, router_bias, w_gate, w_up, w_down, w_sh_gate, w_sh_up, w_sh_down, ln2_g, ln2_b):
    t, d = h.shape
    idx, wts, rank, counts = _route_call(h, w_router, router_bias.reshape(1, -1))
    tables, start = _visit_tables(counts.reshape(-1).astype(I32), t * TOP_K)
    dest = _dest_call(idx, rank, start.astype(F32).reshape(1, -1))
    xs = _scatter_call(dest, h)
    ys = _expert_call(tables, xs, w_gate, w_up, w_down)
    ysh = _shared_call(h, w_sh_gate, w_sh_up, w_sh_down)
    tm = COMBINE_TM
    dest_km = dest.reshape(t // tm, tm, TOP_K).transpose(0, 2, 1).reshape(t // tm, 1, tm * TOP_K)
    return _combine_call(dest_km, ys, wts, h, ysh, ln2_g.reshape(1, -1), ln2_b.reshape(1, -1))


def kernel(x, w_in, w_short_conv, a_log, dt_bias, delta_norm_w, w_o_delta, conv_dw_w, conv_dw_b,
           conv_ln_g, conv_ln_b, w_pw2, b_pw2, w_out, ln1_g, ln1_b, w_router, router_bias, w_gate,
           w_up, w_down, w_sh_gate, w_sh_up, w_sh_down, ln2_g, ln2_b):
    batch, seq, d = x.shape
    depth = w_in.shape[0]
    outs = []
    for bi in range(batch):
        h = x[bi]
        for li in range(depth):
            h = _token_mixing(h, w_in[li], w_short_conv[li], a_log[li], dt_bias[li], delta_norm_w[li],
                              w_o_delta[li], conv_dw_w[li], conv_dw_b[li], conv_ln_g[li], conv_ln_b[li],
                              w_pw2[li], b_pw2[li], w_out[li], ln1_g[li], ln1_b[li])
            h = _moe(h, w_router[li], router_bias[li], w_gate[li], w_up[li], w_down[li],
                     w_sh_gate[li], w_sh_up[li], w_sh_down[li], ln2_g[li], ln2_b[li])
        outs.append(h)
    return jnp.stack(outs, axis=0)
```

```python
import jax
import jax.numpy as jnp
from jax import lax
from jax.experimental import pallas as pl
from jax.experimental.pallas import tpu as pltpu

F32 = jnp.float32
BF16 = jnp.bfloat16
I32 = jnp.int32

D_MODEL = 2048
CHUNK = 64
N_QK_HEADS = 16
N_V_HEADS = 32
HEAD = 128
QK_WIDTH = N_QK_HEADS * HEAD
V_WIDTH = N_V_HEADS * HEAD
SHORT_CONV = 4
CONV_WIDTH = 31
N_EXPERTS = 256
TOP_K = 8
N_GROUPS = 8
TOPK_GROUPS = 4
D_EXPERT = 512
ROUTE_SCALE = 2.5
DN_ALPHA = 2.0 ** 0.25
LN_EPS = 1e-5
NORM_EPS = 1e-6
COL_Z = 2 * QK_WIDTH + V_WIDTH
COL_AB = COL_Z + V_WIDTH
COL_GLU = COL_AB + 2 * N_V_HEADS
COL_GATES = COL_GLU + 2 * D_MODEL

LANES = 128
SUBLANES = 8
VMEM_LIMIT = 56 * 1024 * 1024
TM = 512
TM_WIDE = 1024
TN = 512
GDN_ROWS = 256
GDN_PAIRS = 2
CONV_TM = 256
HALO = 32
CONV_RC = 64
CONV_CC = 256
REPACK_TN = 512
ROUTE_TM = 256
EXPERT_BLOCK = 128
SCATTER_TM = 64
COMBINE_TM = 32


def _params(sem, vmem=VMEM_LIMIT):
    return pltpu.CompilerParams(dimension_semantics=sem, vmem_limit_bytes=vmem)


def _sigmoid(x):
    return jax.nn.sigmoid(x)


def _silu(x):
    return x * jax.nn.sigmoid(x)


def _softplus(x):
    return jnp.maximum(x, 0.0) + jnp.log1p(jnp.exp(-jnp.abs(x)))


def _dot(a, b):
    return jnp.dot(a, b, preferred_element_type=F32)


def _proj_call(name, x, ws, w_offs, n_tiles, epilogue, extras, extra_specs, out_shapes, out_specs,
               scratch=(), tm=TM, tn=TN):
    m, k = x.shape
    single_out = not isinstance(out_shapes, (tuple, list))
    if single_out:
        out_shapes, out_specs = (out_shapes,), (out_specs,)
    nw, ne, no = len(ws), len(extras), len(out_shapes)
    needs_cast = [w.dtype != BF16 for w in ws]

    def body(*refs):
        x_ref = refs[0]
        w_refs = refs[1:1 + nw]
        ex_refs = refs[1 + nw:1 + nw + ne]
        out_refs = refs[1 + nw + ne:1 + nw + ne + no]
        scr = refs[1 + nw + ne + no:]
        wb_refs = scr[:sum(needs_cast)]
        rest = scr[sum(needs_cast):]
        i = pl.program_id(1)
        wsrc, c = [], 0
        for kk in range(nw):
            if needs_cast[kk]:
                wsrc.append(wb_refs[c])
                c += 1
            else:
                wsrc.append(w_refs[kk])

        @pl.when(i == 0)
        def _():
            cc = 0
            for kk in range(nw):
                if needs_cast[kk]:
                    wb_refs[cc][...] = w_refs[kk][...].astype(BF16)
                    cc += 1

        xv = x_ref[...]
        accs = [_dot(xv, wr[...]) for wr in wsrc]
        epilogue(i, accs, ex_refs, out_refs, rest)

    in_specs = [pl.BlockSpec((tm, k), lambda j, i: (i, 0))]
    for off in w_offs:
        in_specs.append(pl.BlockSpec((k, tn), lambda j, i, off=off: (0, j + off)))
    in_specs += list(extra_specs)
    scratch_shapes = [pltpu.VMEM((k, tn), BF16) for c in needs_cast if c] + list(scratch)
    res = pl.pallas_call(
        body,
        grid=(n_tiles, m // tm),
        in_specs=in_specs,
        out_specs=tuple(out_specs),
        out_shape=tuple(out_shapes),
        scratch_shapes=scratch_shapes,
        compiler_params=_params(("parallel", "arbitrary")),
        name=name,
    )(x, *ws, *extras)
    return res[0] if single_out else res


def _short_conv_silu(i, acc, cw_ref, buf, tm):
    @pl.when(i == 0)
    def _():
        buf[0:SUBLANES, :] = jnp.zeros((SUBLANES, buf.shape[1]), F32)

    buf[SUBLANES:SUBLANES + tm, :] = acc
    cw = cw_ref[...]
    y = acc * cw[SHORT_CONV - 1:SHORT_CONV, :]
    for s in range(SHORT_CONV - 1):
        y = y + buf[pl.ds(SUBLANES - (SHORT_CONV - 1) + s, tm), :] * cw[s:s + 1, :]
    buf[0:SUBLANES, :] = buf[tm:tm + SUBLANES, :]
    return _silu(y)


def _qk_epilogue(i, accs, ex, outs, scr):
    cw_ref, scale_ref = ex
    (buf,) = scr
    y = _short_conv_silu(i, accs[0], cw_ref, buf, accs[0].shape[0])
    parts = []
    for g in range(TN // HEAD):
        yg = y[:, g * HEAD:(g + 1) * HEAD]
        parts.append(yg * lax.rsqrt(jnp.sum(yg * yg, axis=-1, keepdims=True) + NORM_EPS))
    outs[0][...] = jnp.concatenate(parts, axis=1) * scale_ref[...]


def _v_epilogue(i, accs, ex, outs, scr):
    (cw_ref,) = ex
    (buf,) = scr
    outs[0][...] = _short_conv_silu(i, accs[0], cw_ref, buf, accs[0].shape[0])


def _z_epilogue(i, accs, ex, outs, scr):
    outs[0][...] = accs[0]


def _glu_epilogue(i, accs, ex, outs, scr):
    outs[0][...] = accs[0] * _sigmoid(accs[1])


def _gates_epilogue(i, accs, ex, outs, scr):
    outs[0][...] = _sigmoid(accs[0])


def _ab_epilogue(i, accs, ex, outs, scr):
    alog_ref, dtb_ref = ex
    gc_ref, bt_ref, gt_ref = outs
    acc = accs[0]
    g = -jnp.exp(alog_ref[...]) * _softplus(acc + dtb_ref[...])
    row = lax.broadcasted_iota(I32, g.shape, 0) % CHUNK
    s = 1
    while s < CHUNK:
        g = g + jnp.where(row >= s, pltpu.roll(g, s, axis=0), 0.0)
        s *= 2
    n_chunks = g.shape[0] // CHUNK
    tot = g.reshape(n_chunks, CHUNK, LANES)[:, CHUNK - 1:CHUNK, :]
    gc_ref[...] = g
    bt_ref[...] = _sigmoid(acc)
    gt_ref[...] = jnp.broadcast_to(tot, (n_chunks, CHUNK, LANES)).reshape(g.shape)


def _odelta_epilogue(i, accs, ex, outs, scr):
    (gate_ref,) = ex
    outs[0][...] = accs[0] * gate_ref[...]


def _pw2_epilogue(i, accs, ex, outs, scr):
    bias_ref, gate_ref, ya_ref = ex
    outs[0][...] = (ya_ref[...] + gate_ref[...] * (accs[0] + bias_ref[...])).astype(BF16)


def _repack_kernel(a_ref, b_ref, o_ref):
    half = LANES // 2
    o_ref[...] = jnp.concatenate([a_ref[:, half:], b_ref[:, :half]], axis=1).astype(BF16)


def _repack_call(w_in, n_cols):
    k = w_in.shape[0]
    tn = REPACK_TN
    first = COL_AB // tn
    return pl.pallas_call(
        _repack_kernel,
        grid=(n_cols // tn,),
        in_specs=[pl.BlockSpec((k, tn), lambda j: (0, first + j)),
                  pl.BlockSpec((k, tn), lambda j: (0, first + j + 1))],
        out_specs=pl.BlockSpec((k, tn), lambda j: (0, j)),
        out_shape=jax.ShapeDtypeStruct((k, n_cols), BF16),
        compiler_params=_params(("parallel",)),
        name="repack_w",
    )(w_in, w_in)


def _gdn_head(q, k, kk, qk, causal, strict, gcol, bcol, gtcol, grow, gtrow, v, state):
    r = GDN_ROWS
    dm = jnp.exp(jnp.where(causal, gcol - grow, -jnp.inf))
    a = jnp.where(strict, bcol * kk * dm, 0.0)
    n = -a
    ap = a
    for _ in range(5):
        ap16 = ap.astype(BF16)
        ap = _dot(ap16, ap16)
        n = n + ap + _dot(n.astype(BF16), ap.astype(BF16))
    eg = jnp.exp(gcol)
    rhs = jnp.concatenate([v * bcol, k * (bcol * eg)], axis=1)
    sol16 = (rhs + _dot(n.astype(BF16), rhs.astype(BF16))).astype(BF16)
    x = _dot((qk * dm).astype(BF16), sol16)
    o_intra = x[:, :HEAD]
    qp16 = (q * eg - x[:, HEAD:]).astype(BF16)
    kd16 = (k * jnp.exp(gtcol - gcol)).astype(BF16)
    tn = (((0,), (0,)), ((), ()))
    o_parts = []
    for j in range(r // CHUNK):
        sl = slice(j * CHUNK, (j + 1) * CHUNK)
        kc = lax.dot_general(kd16[sl], sol16[sl], tn, preferred_element_type=F32)
        lhs = jnp.concatenate([kc[:, HEAD:].astype(BF16), qp16[sl]], axis=0)
        ks_qs = _dot(lhs, state.astype(BF16))
        o_parts.append(ks_qs[HEAD:] + o_intra[sl])
        state = state * jnp.exp(gtrow[j:j + 1, :]) + kc[:, :HEAD] - ks_qs[:HEAD]
    return jnp.concatenate(o_parts, axis=0), state


def _gdn_kernel(q_ref, k_ref, v_ref, z_ref, gc_ref, bt_ref, gt_ref, gcrow_ref, gtrow_ref, nw_ref,
                o_ref, s_ref):
    pp = pl.program_id(0)
    tb = pl.program_id(1)
    r = GDN_ROWS

    @pl.when(tb == 0)
    def _():
        s_ref[...] = jnp.zeros(s_ref.shape, F32)

    row = lax.broadcasted_iota(I32, (r, r), 0)
    col = lax.broadcasted_iota(I32, (r, r), 1)
    same = (row // CHUNK) == (col // CHUNK)
    causal = same & (col <= row)
    strict = same & (col < row)
    lane = lax.broadcasted_iota(I32, (r, LANES), 1)
    gc_all = gc_ref[...]
    bt_all = bt_ref[...]
    gt_all = gt_ref[...]
    nt = (((1,), (1,)), ((), ()))

    for pi in range(GDN_PAIRS):
        q = q_ref[:, pi * HEAD:(pi + 1) * HEAD]
        k = k_ref[:, pi * HEAD:(pi + 1) * HEAD]
        k16 = k.astype(BF16)
        kk = lax.dot_general(k16, k16, nt, preferred_element_type=F32)
        qk = lax.dot_general(q.astype(BF16), k16, nt, preferred_element_type=F32)
        for hh in range(2):
            hl = 2 * pi + hh
            h = 2 * GDN_PAIRS * pp + hl
            gcol = jnp.sum(jnp.where(lane == h, gc_all, 0.0), axis=1, keepdims=True)
            bcol = jnp.sum(jnp.where(lane == h + N_V_HEADS, bt_all, 0.0), axis=1, keepdims=True)
            gtcol = jnp.sum(jnp.where(lane == h, gt_all, 0.0), axis=1, keepdims=True)
            cols = slice(hl * HEAD, (hl + 1) * HEAD)
            o, state = _gdn_head(q, k, kk, qk, causal, strict, gcol, bcol, gtcol,
                                 gcrow_ref[hl, 0], gtrow_ref[hl, 0], v_ref[:, cols], s_ref[hl])
            s_ref[hl] = state
            o = o * lax.rsqrt(jnp.mean(o * o, axis=-1, keepdims=True) + NORM_EPS) * nw_ref[...]
            o_ref[:, cols] = (o * _silu(z_ref[:, cols])).astype(o_ref.dtype)


def _gdn_call(qk, v, z, gc, bt, gt, gcrow, gtrow, nw):
    t = qk.shape[0]
    r = GDN_ROWS
    p = GDN_PAIRS
    k_off = N_QK_HEADS // p
    return pl.pallas_call(
        _gdn_kernel,
        grid=(N_QK_HEADS // p, t // r),
        in_specs=[
            pl.BlockSpec((r, p * HEAD), lambda pp, tb: (tb, pp)),
            pl.BlockSpec((r, p * HEAD), lambda pp, tb: (tb, k_off + pp)),
            pl.BlockSpec((r, 2 * p * HEAD), lambda pp, tb: (tb, pp)),
            pl.BlockSpec((r, 2 * p * HEAD), lambda pp, tb: (tb, pp)),
            pl.BlockSpec((r, LANES), lambda pp, tb: (tb, 0)),
            pl.BlockSpec((r, LANES), lambda pp, tb: (tb, 0)),
            pl.BlockSpec((r, LANES), lambda pp, tb: (tb, 0)),
            pl.BlockSpec((2 * p, 1, 1, r), lambda pp, tb: (pp, tb, 0, 0)),
            pl.BlockSpec((2 * p, 1, r // CHUNK, LANES), lambda pp, tb: (pp, tb, 0, 0)),
            pl.BlockSpec((1, HEAD), lambda pp, tb: (0, 0)),
        ],
        out_specs=pl.BlockSpec((r, 2 * p * HEAD), lambda pp, tb: (tb, pp)),
        out_shape=jax.ShapeDtypeStruct((t, V_WIDTH), BF16),
        scratch_shapes=[pltpu.VMEM((2 * p, HEAD, HEAD), F32)],
        compiler_params=_params(("parallel", "arbitrary")),
        name="gdn",
    )(qk, qk, v, z, gc, bt, gt, gcrow, gtrow, nw)


def _layer_norm(r, g, b):
    mu = jnp.mean(r, axis=-1, keepdims=True)
    rc = r - mu
    var = jnp.mean(rc * rc, axis=-1, keepdims=True)
    return rc * lax.rsqrt(var + LN_EPS) * g + b


def _dwconv_kernel(c_ref, w_ref, b_ref, g_ref, beta_ref, o_ref, buf, shifted, accbuf):
    i = pl.program_id(0)
    tm = CONV_TM
    ch = buf.shape[1]
    n_shift = HALO + tm - SUBLANES

    @pl.when(i == 0)
    def _():
        buf[0:HALO, :] = jnp.zeros((HALO, ch), F32)

    buf[HALO:HALO + tm, :] = c_ref[...]
    for b in range(1, SUBLANES):
        shifted[b - 1] = buf[pl.ds(b, n_shift), :]

    def row_body(rc, carry):
        r0 = pl.multiple_of(rc * CONV_RC, CONV_RC)
        for cc in range(ch // CONV_CC):
            cs = slice(cc * CONV_CC, (cc + 1) * CONV_CC)
            acc = buf[pl.ds(HALO + r0, CONV_RC), cs] * w_ref[CONV_WIDTH - 1:CONV_WIDTH, cs] + b_ref[:, cs]
            for s in range(CONV_WIDTH - 1):
                a, b = divmod(HALO - (CONV_WIDTH - 1) + s, SUBLANES)
                if b == 0:
                    src = buf[pl.ds(a * SUBLANES + r0, CONV_RC), cs]
                else:
                    src = shifted[b - 1, pl.ds(a * SUBLANES + r0, CONV_RC), cs]
                acc = acc + src * w_ref[s:s + 1, cs]
            accbuf[pl.ds(r0, CONV_RC), cs] = acc
        return carry

    lax.fori_loop(0, tm // CONV_RC, row_body, 0)
    buf[0:HALO, :] = buf[tm:tm + HALO, :]
    o_ref[...] = _silu(_layer_norm(accbuf[...], g_ref[...], beta_ref[...])).astype(o_ref.dtype)


def _dwconv_call(c, w, b, g, beta):
    t, ch = c.shape
    tm = CONV_TM
    vec = pl.BlockSpec((1, ch), lambda i: (0, 0))
    return pl.pallas_call(
        _dwconv_kernel,
        grid=(t // tm,),
        in_specs=[pl.BlockSpec((tm, ch), lambda i: (i, 0)),
                  pl.BlockSpec((CONV_WIDTH, ch), lambda i: (0, 0)), vec, vec, vec],
        out_specs=pl.BlockSpec((tm, ch), lambda i: (i, 0)),
        out_shape=jax.ShapeDtypeStruct((t, ch), BF16),
        scratch_shapes=[pltpu.VMEM((HALO + tm, ch), F32),
                        pltpu.VMEM((SUBLANES - 1, HALO + tm - SUBLANES, ch), F32),
                        pltpu.VMEM((tm, ch), F32)],
        compiler_params=_params(("arbitrary",)),
        name="dwconv_ln",
    )(c, w, b, g, beta)


def _outproj_kernel(m_ref, w_ref, x_ref, g_ref, b_ref, h_ref):
    y = _dot(m_ref[...], w_ref[...])
    h_ref[...] = _layer_norm(DN_ALPHA * x_ref[...] + y, g_ref[...], b_ref[...])


def _outproj_call(mixed, w16, x, g, b):
    t, d = x.shape
    tm = 256
    vec = pl.BlockSpec((1, d), lambda i: (0, 0))
    return pl.pallas_call(
        _outproj_kernel,
        grid=(t // tm,),
        in_specs=[pl.BlockSpec((tm, d), lambda i: (i, 0)), pl.BlockSpec((d, d), lambda i: (0, 0)),
                  pl.BlockSpec((tm, d), lambda i: (i, 0)), vec, vec],
        out_specs=pl.BlockSpec((tm, d), lambda i: (i, 0)),
        out_shape=jax.ShapeDtypeStruct((t, d), F32),
        compiler_params=_params(("parallel",)),
        name="outproj_ln1",
    )(mixed, w16, x, g, b)


def _route_kernel(h_ref, w_ref, bias_ref, idx_ref, wts_ref, rank_ref, cnt_ref, carry):
    @pl.when(pl.program_id(0) == 0)
    def _():
        carry[...] = jnp.zeros(carry.shape, F32)

    logits = jnp.dot(h_ref[...], w_ref[...], precision=lax.Precision.HIGHEST,
                     preferred_element_type=F32)
    scores = _sigmoid(logits)
    biased = scores + bias_ref[...]
    shape = biased.shape
    tm = shape[0]
    lane_i = lax.broadcasted_iota(I32, shape, 1)
    lane = lane_i.astype(F32)
    per_group = N_EXPERTS // N_GROUPS
    grp = (lane_i // per_group).astype(F32)
    neg = -jnp.inf
    big = 1e9

    def rmax(x):
        return jnp.max(x, axis=1, keepdims=True)

    def rmin(x):
        return jnp.min(x, axis=1, keepdims=True)

    def rsum(x):
        return jnp.sum(x, axis=1, keepdims=True)

    gs = jnp.zeros(shape, F32)
    for g in range(N_GROUPS):
        in_g = grp == float(g)
        m = jnp.where(in_g, biased, neg)
        m1 = rmax(m)
        i1 = rmin(jnp.where(m == m1, lane, big))
        m2 = rmax(jnp.where(lane == i1, neg, m))
        gs = jnp.where(in_g, m1 + m2, gs)
    sel_g = jnp.zeros(shape, jnp.bool_)
    cur = gs
    for _ in range(TOPK_GROUPS):
        mx = rmax(cur)
        gi = rmin(jnp.where(cur == mx, grp, big))
        hit = grp == gi
        sel_g = sel_g | hit
        cur = jnp.where(hit, neg, cur)
    masked = jnp.where(sel_g, biased, neg)
    ids, ws = [], []
    sel = jnp.zeros(shape, F32)
    for _ in range(TOP_K):
        mx = rmax(masked)
        ik = rmin(jnp.where(masked == mx, lane, big))
        hit = lane == ik
        ws.append(rsum(jnp.where(hit, scores, 0.0)))
        ids.append(ik)
        sel = jnp.where(hit, 1.0, sel)
        masked = jnp.where(hit, neg, masked)
    w = jnp.concatenate(ws, axis=1)
    wts_ref[...] = w / rsum(w) * ROUTE_SCALE
    idx_ref[...] = jnp.concatenate(ids, axis=1).astype(I32)

    earlier = (lax.broadcasted_iota(I32, (tm, tm), 1) < lax.broadcasted_iota(I32, (tm, tm), 0))
    before = _dot(earlier.astype(BF16), sel.astype(BF16)) + carry[...]
    ranks = [rsum(jnp.where(lane == ik, before, 0.0)) for ik in ids]
    rank_ref[...] = jnp.concatenate(ranks, axis=1).astype(I32)
    carry[...] = carry[...] + jnp.sum(sel, axis=0, keepdims=True)
    cnt_ref[...] = carry[...]


def _route_call(h, w_router, bias):
    t, d = h.shape
    tm = ROUTE_TM
    tok = pl.BlockSpec((tm, TOP_K), lambda i: (i, 0))
    return pl.pallas_call(
        _route_kernel,
        grid=(t // tm,),
        in_specs=[pl.BlockSpec((tm, d), lambda i: (i, 0)),
                  pl.BlockSpec((d, N_EXPERTS), lambda i: (0, 0)),
                  pl.BlockSpec((1, N_EXPERTS), lambda i: (0, 0))],
        out_specs=(tok, tok, tok, pl.BlockSpec((1, N_EXPERTS), lambda i: (0, 0))),
        out_shape=(jax.ShapeDtypeStruct((t, TOP_K), I32), jax.ShapeDtypeStruct((t, TOP_K), F32),
                   jax.ShapeDtypeStruct((t, TOP_K), I32), jax.ShapeDtypeStruct((1, N_EXPERTS), F32)),
        scratch_shapes=[pltpu.VMEM((1, N_EXPERTS), F32)],
        compiler_params=_params(("arbitrary",)),
        name="route",
    )(h, w_router, bias)


def _dest_kernel(idx_ref, rank_ref, start_ref, dest_ref):
    idx = idx_ref[...]
    shape = (idx.shape[0], N_EXPERTS)
    lane = lax.broadcasted_iota(I32, shape, 1)
    start = start_ref[...]
    cols = []
    for kk in range(TOP_K):
        hit = lane == idx[:, kk:kk + 1]
        cols.append(jnp.sum(jnp.where(hit, start, 0.0), axis=1, keepdims=True))
    dest_ref[...] = jnp.concatenate(cols, axis=1).astype(I32) + rank_ref[...]


def _dest_call(idx, rank, start_f32):
    t = idx.shape[0]
    tm = 512
    tok = pl.BlockSpec((tm, TOP_K), lambda i: (i, 0))
    return pl.pallas_call(
        _dest_kernel,
        grid=(t // tm,),
        in_specs=[tok, tok, pl.BlockSpec((1, N_EXPERTS), lambda i: (0, 0))],
        out_specs=tok,
        out_shape=jax.ShapeDtypeStruct((t, TOP_K), I32),
        compiler_params=_params(("parallel",)),
        name="dest",
    )(idx, rank, start_f32)


def _row_copy(src, src_row, dst, dst_row, sem):
    return pltpu.make_async_copy(src.at[pl.ds(src_row, 1), :], dst.at[pl.ds(dst_row, 1), :], sem)


def _rows_wait(src, dst, dst_row, n_rows, sem):
    pltpu.make_async_copy(src.at[pl.ds(0, n_rows), :], dst.at[pl.ds(dst_row, n_rows), :], sem).wait()


def _scatter_kernel(dest_ref, h_ref, xs_hbm, sem):
    def issue(tok, c):
        for kk in range(TOP_K):
            _row_copy(h_ref, tok, xs_hbm, dest_ref[0, 0, tok * TOP_K + kk], sem.at[tok]).start(priority=kk % 2)
        return c

    lax.fori_loop(0, SCATTER_TM, issue, 0, unroll=2)

    def wait(tok, c):
        _rows_wait(h_ref, xs_hbm, 0, TOP_K, sem.at[tok])
        return c

    lax.fori_loop(0, SCATTER_TM, wait, 0, unroll=8)


def _scatter_call(dest, h):
    t, d = h.shape
    tm = SCATTER_TM
    n = tm * TOP_K
    return pl.pallas_call(
        _scatter_kernel,
        grid=(t // tm,),
        in_specs=[pl.BlockSpec((1, 1, n), lambda i: (i, 0, 0), memory_space=pltpu.SMEM),
                  pl.BlockSpec((tm, d), lambda i: (i, 0))],
        out_specs=pl.BlockSpec(memory_space=pl.ANY),
        out_shape=jax.ShapeDtypeStruct((t * TOP_K, d), F32),
        scratch_shapes=[pltpu.SemaphoreType.DMA((tm,))],
        compiler_params=_params(("arbitrary",)),
        name="dispatch",
    )(dest.reshape(t // tm, 1, n), h)


def _weight_copies(w_hbm, w32, wsem, e, slot):
    return [pltpu.make_async_copy(w_hbm[m].at[e], w32[m].at[slot], wsem.at[slot, m])
            for m in range(3)]


def _expert_kernel(vb_ref, ve_ref, lo_ref, hi_ref, fb_ref, fe_ref, par_ref, nxt_ref, nv_ref,
                   x_ref, wg_hbm, wu_hbm, wd_hbm, y_ref,
                   wg32, wu32, wd32, wg16, wu16, wd16, wsem):
    v = pl.program_id(0)
    w_hbm = (wg_hbm, wu_hbm, wd_hbm)
    w32 = (wg32, wu32, wd32)
    w16 = (wg16, wu16, wd16)

    @pl.when(v < nv_ref[0])
    def _():
        slot = par_ref[v]

        @pl.when(fe_ref[v] == 1)
        def _():
            @pl.when(v == 0)
            def _():
                for cp in _weight_copies(w_hbm, w32, wsem, ve_ref[v], slot):
                    cp.start()

            @pl.when(nxt_ref[v] >= 0)
            def _():
                for cp in _weight_copies(w_hbm, w32, wsem, nxt_ref[v], 1 - slot):
                    cp.start()

            for m, cp in enumerate(_weight_copies(w_hbm, w32, wsem, ve_ref[v], slot)):
                cp.wait()
                w16[m][...] = w32[m][slot].astype(BF16)

        x = x_ref[...].astype(BF16)
        g = _dot(x, wg16[...])
        u = _dot(x, wu16[...])
        y = _dot((_silu(g) * u).astype(BF16), wd16[...])
        rows = lax.broadcasted_iota(I32, y.shape, 0)
        mine = (rows >= lo_ref[v]) & (rows < hi_ref[v])

        @pl.when(fb_ref[v] == 1)
        def _():
            y_ref[...] = jnp.where(mine, y, 0.0)

        @pl.when(fb_ref[v] == 0)
        def _():
            y_ref[...] = jnp.where(mine, y, y_ref[...])


def _expert_call(tables, xs, w_gate, w_up, w_down):
    n_rows, d = xs.shape
    blk = EXPERT_BLOCK
    n_visits = tables[0].shape[0]
    any_spec = pl.BlockSpec(memory_space=pl.ANY)
    grid_spec = pltpu.PrefetchScalarGridSpec(
        num_scalar_prefetch=len(tables),
        grid=(n_visits,),
        in_specs=[pl.BlockSpec((blk, d), lambda v, vb, *_: (vb[v], 0)), any_spec, any_spec, any_spec],
        out_specs=pl.BlockSpec((blk, d), lambda v, vb, *_: (vb[v], 0)),
        scratch_shapes=[
            pltpu.VMEM((2, d, D_EXPERT), F32), pltpu.VMEM((2, d, D_EXPERT), F32),
            pltpu.VMEM((2, D_EXPERT, d), F32),
            pltpu.VMEM((d, D_EXPERT), BF16), pltpu.VMEM((d, D_EXPERT), BF16),
            pltpu.VMEM((D_EXPERT, d), BF16),
            pltpu.SemaphoreType.DMA((2, 3)),
        ],
    )
    return pl.pallas_call(
        _expert_kernel,
        grid_spec=grid_spec,
        out_shape=jax.ShapeDtypeStruct((n_rows, d), F32),
        compiler_params=_params(("arbitrary",)),
        name="experts",
    )(*tables, xs, w_gate, w_up, w_down)


def _visit_tables(counts, n_rows):
    blk = EXPERT_BLOCK
    n_visits = n_rows // blk + N_EXPERTS - 1
    end = jnp.cumsum(counts)
    start = end - counts
    nonempty = counts > 0
    first_blk = start // blk
    last_blk = jnp.maximum(end - 1, 0) // blk
    nvis = jnp.where(nonempty, last_blk - first_blk + 1, 0)
    vis_end = jnp.cumsum(nvis)
    vis_start = vis_end - nvis
    n_used = vis_end[-1]
    ids = jnp.arange(N_EXPERTS, dtype=I32)
    ordinal = jnp.cumsum(nonempty.astype(I32)) - 1
    nxt_incl = lax.cummin(jnp.where(nonempty, ids, N_EXPERTS), axis=0, reverse=True)
    nxt_e = jnp.concatenate([nxt_incl[1:], jnp.full((1,), N_EXPERTS, I32)])
    nxt_e = jnp.where(nxt_e >= N_EXPERTS, -1, nxt_e)
    v = jnp.arange(n_visits, dtype=I32)
    vc = jnp.clip(v, 0, jnp.maximum(n_used - 1, 0))
    ve = jnp.minimum(jnp.sum((vis_end[None, :] <= vc[:, None]).astype(I32), axis=1), N_EXPERTS - 1)
    onehot = ve[:, None] == ids[None, :]
    look = lambda tab: jnp.sum(jnp.where(onehot, tab.astype(I32)[None, :], 0), axis=1)
    v_first = look(vis_start)
    vb = look(first_blk) + (vc - v_first)
    lo = jnp.clip(look(start) - vb * blk, 0, blk)
    hi = jnp.clip(look(end) - vb * blk, 0, blk)
    fb = jnp.concatenate([jnp.ones((1,), I32), (vb[1:] != vb[:-1]).astype(I32)])
    fe = (vc == v_first).astype(I32)
    tables = (vb, ve, lo, hi, fb, fe, look(ordinal % 2), look(nxt_e), n_used.astype(I32).reshape(1))
    return tables, start


def _shared_kernel(h_ref, wg_ref, wu_ref, wd_ref, y_ref, wg16, wu16, wd16):
    @pl.when(pl.program_id(0) == 0)
    def _():
        wg16[...] = wg_ref[...].astype(BF16)
        wu16[...] = wu_ref[...].astype(BF16)
        wd16[...] = wd_ref[...].astype(BF16)

    x = h_ref[...].astype(BF16)
    g = _dot(x, wg16[...])
    u = _dot(x, wu16[...])
    y_ref[...] = _dot((_silu(g) * u).astype(BF16), wd16[...])


def _shared_call(h, wg, wu, wd):
    t, d = h.shape
    ds = wg.shape[1]
    tm = 512
    return pl.pallas_call(
        _shared_kernel,
        grid=(t // tm,),
        in_specs=[pl.BlockSpec((tm, d), lambda i: (i, 0)),
                  pl.BlockSpec((d, ds), lambda i: (0, 0)),
                  pl.BlockSpec((d, ds), lambda i: (0, 0)),
                  pl.BlockSpec((ds, d), lambda i: (0, 0))],
        out_specs=pl.BlockSpec((tm, d), lambda i: (i, 0)),
        out_shape=jax.ShapeDtypeStruct((t, d), F32),
        scratch_shapes=[pltpu.VMEM((d, ds), BF16), pltpu.VMEM((d, ds), BF16),
                        pltpu.VMEM((ds, d), BF16)],
        compiler_params=_params(("arbitrary",)),
        name="shared_expert",
    )(h, wg, wu, wd)


def _combine_kernel(dest_ref, dnext_ref, ys_hbm, wts_ref, h_ref, ysh_ref, g_ref, b_ref, o_ref,
                    buf, sem):
    tm = COMBINE_TM
    i = pl.program_id(0)
    slot = i % 2

    def gather(d_ref, s):
        def issue(r, c):
            for kk in range(TOP_K):
                a = kk * tm + r
                _row_copy(ys_hbm, d_ref[0, 0, a], buf.at[s], a, sem.at[s, kk]).start(priority=kk % 2)
            return c

        lax.fori_loop(0, tm, issue, 0, unroll=2)

    @pl.when(i == 0)
    def _():
        gather(dest_ref, slot)

    @pl.when(i + 1 < pl.num_programs(0))
    def _():
        gather(dnext_ref, 1 - slot)

    acc = DN_ALPHA * h_ref[...] + ysh_ref[...]
    wts = wts_ref[...]
    for kk in range(TOP_K):
        _rows_wait(ys_hbm, buf.at[slot], kk * tm, tm, sem.at[slot, kk])
    for kk in range(TOP_K):
        acc = acc + wts[:, kk:kk + 1] * buf[slot, kk * tm:(kk + 1) * tm, :]
    o_ref[...] = _layer_norm(acc, g_ref[...], b_ref[...])


def _combine_call(dest_km, ys, wts, h, ysh, g, b):
    t, d = h.shape
    tm = COMBINE_TM
    n = tm * TOP_K
    n_tiles = t // tm
    vec = pl.BlockSpec((1, d), lambda i: (0, 0))
    return pl.pallas_call(
        _combine_kernel,
        grid=(n_tiles,),
        in_specs=[
            pl.BlockSpec((1, 1, n), lambda i: (i, 0, 0), memory_space=pltpu.SMEM),
            pl.BlockSpec((1, 1, n), lambda i: (jnp.minimum(i + 1, n_tiles - 1), 0, 0),
                         memory_space=pltpu.SMEM),
            pl.BlockSpec(memory_space=pl.ANY),
            pl.BlockSpec((tm, TOP_K), lambda i: (i, 0)),
            pl.BlockSpec((tm, d), lambda i: (i, 0)),
            pl.BlockSpec((tm, d), lambda i: (i, 0)),
            vec, vec,
        ],
        out_specs=pl.BlockSpec((tm, d), lambda i: (i, 0)),
        out_shape=jax.ShapeDtypeStruct((t, d), F32),
        scratch_shapes=[pltpu.VMEM((2, n, d), F32), pltpu.SemaphoreType.DMA((2, TOP_K))],
        compiler_params=_params(("arbitrary",)),
        name="combine_ln2",
    )(dest_km, dest_km, ys, wts, h, ysh, g, b)


def _token_mixing(x, w_in, w_short_conv, a_log, dt_bias, delta_norm_w, w_o_delta, conv_dw_w,
                  conv_dw_b, conv_ln_g, conv_ln_b, w_pw2, b_pw2, w_out, ln1_g, ln1_b):
    t, d = x.shape
    x16 = x.astype(BF16)
    row = lambda a: a.reshape(1, -1)
    tile = pl.BlockSpec((TM, TN), lambda j, i: (i, j))
    wide = pl.BlockSpec((TM_WIDE, TN), lambda j, i: (i, j))
    conv_buf = pltpu.VMEM((TM_WIDE + SUBLANES, TN), F32)

    scale = jnp.concatenate([jnp.full((1, QK_WIDTH), HEAD ** -0.5, F32), jnp.ones((1, QK_WIDTH), F32)], axis=1)
    qk = _proj_call(
        "proj_qk", x16, [w_in], [0], 2 * QK_WIDTH // TN, _qk_epilogue,
        [w_short_conv, scale],
        [pl.BlockSpec((SHORT_CONV, TN), lambda j, i: (0, j)), pl.BlockSpec((1, TN), lambda j, i: (0, j))],
        jax.ShapeDtypeStruct((t, 2 * QK_WIDTH), F32), wide, scratch=[conv_buf], tm=TM_WIDE)
    v_off = 2 * QK_WIDTH // TN
    v = _proj_call(
        "proj_v", x16, [w_in], [v_off], V_WIDTH // TN, _v_epilogue,
        [w_short_conv], [pl.BlockSpec((SHORT_CONV, TN), lambda j, i: (0, j + v_off))],
        jax.ShapeDtypeStruct((t, V_WIDTH), F32), wide, scratch=[conv_buf], tm=TM_WIDE)
    z = _proj_call(
        "proj_z", x16, [w_in], [COL_Z // TN], V_WIDTH // TN, _z_epilogue, [], [],
        jax.ShapeDtypeStruct((t, V_WIDTH), F32), wide, tm=TM_WIDE)

    pad_h = lambda a: jnp.pad(a.reshape(1, -1), ((0, 0), (0, LANES - N_V_HEADS)))
    lane_vec = pl.BlockSpec((1, LANES), lambda j, i: (0, 0))
    lane_tile = pl.BlockSpec((TM, LANES), lambda j, i: (i, 0))
    lane_shape = jax.ShapeDtypeStruct((t, LANES), F32)
    gc, bt, gt = _proj_call(
        "proj_ab", x16, [w_in], [COL_AB // LANES], 1, _ab_epilogue, [pad_h(a_log), pad_h(dt_bias)],
        [lane_vec, lane_vec], (lane_shape, lane_shape, lane_shape), (lane_tile, lane_tile, lane_tile),
        tn=LANES)

    w_rest = _repack_call(w_in, 4 * d)
    c = _proj_call(
        "proj_glu", x16, [w_rest, w_rest], [0, d // TN], d // TN, _glu_epilogue, [], [],
        jax.ShapeDtypeStruct((t, d), F32), wide, tm=TM_WIDE)
    gates = _proj_call(
        "proj_gates", x16, [w_rest], [2 * d // TN], 2 * d // TN, _gates_epilogue, [], [],
        jax.ShapeDtypeStruct((t, 2 * d), F32), wide, tm=TM_WIDE)

    r = GDN_ROWS
    gcrow = gc[:, :N_V_HEADS].T.reshape(N_V_HEADS, t // r, 1, r)
    gt_chunk = gt[CHUNK - 1::CHUNK, :N_V_HEADS].T
    gtrow = jnp.broadcast_to(gt_chunk[:, :, None], (N_V_HEADS, t // CHUNK, LANES))
    gtrow = gtrow.reshape(N_V_HEADS, t // r, r // CHUNK, LANES)
    og = _gdn_call(qk, v, z, gc, bt, gt, gcrow, gtrow, row(delta_norm_w))

    ya = _proj_call(
        "proj_odelta", og, [w_o_delta], [0], d // TN, _odelta_epilogue, [gates], [tile],
        jax.ShapeDtypeStruct((t, d), F32), tile)

    c_act = _dwconv_call(c, conv_dw_w, row(conv_dw_b), row(conv_ln_g), row(conv_ln_b))
    g_off = d // TN
    mixed = _proj_call(
        "proj_pw2", c_act, [w_pw2], [0], d // TN, _pw2_epilogue, [row(b_pw2), gates, ya],
        [pl.BlockSpec((1, TN), lambda j, i: (0, j)),
         pl.BlockSpec((TM_WIDE, TN), lambda j, i: (i, j + g_off)), wide],
        jax.ShapeDtypeStruct((t, d), BF16), wide, tm=TM_WIDE)
    return _outproj_call(mixed, w_out.astype(BF16), x, row(ln1_g), row(ln1_b))


def _moe(h, w_router, router_bias, w_gate, w_up, w_down, w_sh_gate, w_sh_up, w_sh_down, ln2_g, ln2_b):
    t, d = h.shape
    idx, wts, rank, counts = _route_call(h, w_router, router_bias.reshape(1, -1))
    tables, start = _visit_tables(counts.reshape(-1).astype(I32), t * TOP_K)
    dest = _dest_call(idx, rank, start.astype(F32).reshape(1, -1))
    xs = _scatter_call(dest, h)
    ys = _expert_call(tables, xs, w_gate, w_up, w_down)
    ysh = _shared_call(h, w_sh_gate, w_sh_up, w_sh_down)
    tm = COMBINE_TM
    dest_km = dest.reshape(t // tm, tm, TOP_K).transpose(0, 2, 1).reshape(t // tm, 1, tm * TOP_K)
    return _combine_call(dest_km, ys, wts, h, ysh, ln2_g.reshape(1, -1), ln2_b.reshape(1, -1))


def kernel(x, w_in, w_short_conv, a_log, dt_bias, delta_norm_w, w_o_delta, conv_dw_w, conv_dw_b,
           conv_ln_g, conv_ln_b, w_pw2, b_pw2, w_out, ln1_g, ln1_b, w_router, router_bias, w_gate,
           w_up, w_down, w_sh_gate, w_sh_up, w_sh_down, ln2_g, ln2_b):
    batch, seq, d = x.shape
    depth = w_in.shape[0]
    outs = []
    for bi in range(batch):
        h = x[bi]
        for li in range(depth):
            h = _token_mixing(h, w_in[li], w_short_conv[li], a_log[li], dt_bias[li], delta_norm_w[li],
                              w_o_delta[li], conv_dw_w[li], conv_dw_b[li], conv_ln_g[li], conv_ln_b[li],
                              w_pw2[li], b_pw2[li], w_out[li], ln1_g[li], ln1_b[li])
            h = _moe(h, w_router[li], router_bias[li], w_gate[li], w_up[li], w_down[li],
                     w_sh_gate[li], w_sh_up[li], w_sh_down[li], ln2_g[li], ln2_b[li])
        outs.append(h)
    return jnp.stack(outs, axis=0)
```

```python
import jax
import jax.numpy as jnp
from jax import lax
from jax.experimental import pallas as pl
from jax.experimental.pallas import tpu as pltpu

F32 = jnp.float32
BF16 = jnp.bfloat16
I32 = jnp.int32

D_MODEL = 2048
CHUNK = 64
N_QK_HEADS = 16
N_V_HEADS = 32
HEAD = 128
QK_WIDTH = N_QK_HEADS * HEAD
V_WIDTH = N_V_HEADS * HEAD
SHORT_CONV = 4
CONV_WIDTH = 31
N_EXPERTS = 256
TOP_K = 8
N_GROUPS = 8
TOPK_GROUPS = 4
D_EXPERT = 512
ROUTE_SCALE = 2.5
DN_ALPHA = 2.0 ** 0.25
LN_EPS = 1e-5
NORM_EPS = 1e-6
COL_Z = 2 * QK_WIDTH + V_WIDTH
COL_AB = COL_Z + V_WIDTH
COL_GLU = COL_AB + 2 * N_V_HEADS
COL_GATES = COL_GLU + 2 * D_MODEL

LANES = 128
SUBLANES = 8
VMEM_LIMIT = 56 * 1024 * 1024
TM = 512
TM_WIDE = 1024
TN = 512
GDN_ROWS = 256
GDN_PAIRS = 2
CONV_TM = 256
HALO = 32
CONV_RC = 64
CONV_CC = 256
ROW_ALIGN = 64
ROUTE_TM = 256
EXPERT_BLOCK = 128
SCATTER_TM = 128
COMBINE_TM = 64
WEIGHT_DMA_QUEUE = 1


def _params(sem, vmem=VMEM_LIMIT):
    return pltpu.CompilerParams(dimension_semantics=sem, vmem_limit_bytes=vmem)


def _sigmoid(x):
    return jax.nn.sigmoid(x)


def _silu(x):
    return x * jax.nn.sigmoid(x)


def _softplus(x):
    return jnp.maximum(x, 0.0) + jnp.log1p(jnp.exp(-jnp.abs(x)))


def _dot(a, b):
    return jnp.dot(a, b, preferred_element_type=F32)


def _proj_call(name, x, ws, w_offs, n_tiles, epilogue, extras, extra_specs, out_shapes, out_specs,
               scratch=(), tm=TM, tn=TN, transposed=False):
    m, k = x.shape
    single_out = not isinstance(out_shapes, (tuple, list))
    if single_out:
        out_shapes, out_specs = (out_shapes,), (out_specs,)
    nw, ne, no = len(ws), len(extras), len(out_shapes)
    needs_cast = [w.dtype != BF16 for w in ws]
    contract = (((1,), (1,)), ((), ())) if transposed else (((1,), (0,)), ((), ()))

    def body(*refs):
        x_ref = refs[0]
        w_refs = refs[1:1 + nw]
        ex_refs = refs[1 + nw:1 + nw + ne]
        out_refs = refs[1 + nw + ne:1 + nw + ne + no]
        scr = refs[1 + nw + ne + no:]
        wb_refs = scr[:sum(needs_cast)]
        rest = scr[sum(needs_cast):]
        i = pl.program_id(1)
        wsrc, c = [], 0
        for kk in range(nw):
            if needs_cast[kk]:
                wsrc.append(wb_refs[c])
                c += 1
            else:
                wsrc.append(w_refs[kk])

        @pl.when(i == 0)
        def _():
            cc = 0
            for kk in range(nw):
                if needs_cast[kk]:
                    wb_refs[cc][...] = w_refs[kk][...].astype(BF16)
                    cc += 1

        xv = x_ref[...]
        accs = [lax.dot_general(xv, wr[...], contract, preferred_element_type=F32) for wr in wsrc]
        epilogue(i, accs, ex_refs, out_refs, rest)

    in_specs = [pl.BlockSpec((tm, k), lambda j, i: (i, 0))]
    for off in w_offs:
        if transposed:
            in_specs.append(pl.BlockSpec(
                (pl.Element(tn), pl.Element(k)),
                lambda j, i, off=off: (pl.multiple_of(off + tn * j, ROW_ALIGN), 0)))
        else:
            in_specs.append(pl.BlockSpec((k, tn), lambda j, i, off=off: (0, j + off)))
    in_specs += list(extra_specs)
    w_tile = (tn, k) if transposed else (k, tn)
    scratch_shapes = [pltpu.VMEM(w_tile, BF16) for c in needs_cast if c] + list(scratch)
    res = pl.pallas_call(
        body,
        grid=(n_tiles, m // tm),
        in_specs=in_specs,
        out_specs=tuple(out_specs),
        out_shape=tuple(out_shapes),
        scratch_shapes=scratch_shapes,
        compiler_params=_params(("parallel", "arbitrary")),
        name=name,
    )(x, *ws, *extras)
    return res[0] if single_out else res


def _short_conv_silu(i, acc, cw_ref, buf, tm):
    @pl.when(i == 0)
    def _():
        buf[0:SUBLANES, :] = jnp.zeros((SUBLANES, buf.shape[1]), F32)

    buf[SUBLANES:SUBLANES + tm, :] = acc
    cw = cw_ref[...]
    y = acc * cw[SHORT_CONV - 1:SHORT_CONV, :]
    for s in range(SHORT_CONV - 1):
        y = y + buf[pl.ds(SUBLANES - (SHORT_CONV - 1) + s, tm), :] * cw[s:s + 1, :]
    buf[0:SUBLANES, :] = buf[tm:tm + SUBLANES, :]
    return _silu(y)


def _qk_epilogue(i, accs, ex, outs, scr):
    cw_ref, scale_ref = ex
    (buf,) = scr
    y = _short_conv_silu(i, accs[0], cw_ref, buf, accs[0].shape[0])
    parts = []
    for g in range(TN // HEAD):
        yg = y[:, g * HEAD:(g + 1) * HEAD]
        parts.append(yg * lax.rsqrt(jnp.sum(yg * yg, axis=-1, keepdims=True) + NORM_EPS))
    outs[0][...] = jnp.concatenate(parts, axis=1) * scale_ref[...]


def _v_epilogue(i, accs, ex, outs, scr):
    (cw_ref,) = ex
    (buf,) = scr
    outs[0][...] = _short_conv_silu(i, accs[0], cw_ref, buf, accs[0].shape[0])


def _z_epilogue(i, accs, ex, outs, scr):
    outs[0][...] = accs[0]


def _glu_epilogue(i, accs, ex, outs, scr):
    outs[0][...] = accs[0] * _sigmoid(accs[1])


def _gates_epilogue(i, accs, ex, outs, scr):
    outs[0][...] = _sigmoid(accs[0])


def _ab_epilogue(i, accs, ex, outs, scr):
    alog_ref, dtb_ref = ex
    gc_ref, bt_ref, gt_ref = outs
    acc = accs[0]
    g = -jnp.exp(alog_ref[...]) * _softplus(acc + dtb_ref[...])
    row = lax.broadcasted_iota(I32, g.shape, 0) % CHUNK
    s = 1
    while s < CHUNK:
        g = g + jnp.where(row >= s, pltpu.roll(g, s, axis=0), 0.0)
        s *= 2
    n_chunks = g.shape[0] // CHUNK
    tot = g.reshape(n_chunks, CHUNK, LANES)[:, CHUNK - 1:CHUNK, :]
    gc_ref[...] = g
    bt_ref[...] = _sigmoid(acc)
    gt_ref[...] = jnp.broadcast_to(tot, (n_chunks, CHUNK, LANES)).reshape(g.shape)


def _odelta_epilogue(i, accs, ex, outs, scr):
    (gate_ref,) = ex
    outs[0][...] = accs[0] * gate_ref[...]


def _pw2_epilogue(i, accs, ex, outs, scr):
    bias_ref, gate_ref, ya_ref = ex
    outs[0][...] = (ya_ref[...] + gate_ref[...] * (accs[0] + bias_ref[...])).astype(BF16)


def _gdn_head(q, k, kk, qk, causal, strict, gcol, bcol, gtcol, grow, gtrow, v, state):
    r = GDN_ROWS
    dm = jnp.exp(jnp.where(causal, gcol - grow, -jnp.inf))
    a = jnp.where(strict, bcol * kk * dm, 0.0)
    n = -a
    ap = a
    for _ in range(5):
        ap16 = ap.astype(BF16)
        ap = _dot(ap16, ap16)
        n = n + ap + _dot(n.astype(BF16), ap.astype(BF16))
    eg = jnp.exp(gcol)
    rhs = jnp.concatenate([v * bcol, k * (bcol * eg)], axis=1)
    sol16 = (rhs + _dot(n.astype(BF16), rhs.astype(BF16))).astype(BF16)
    x = _dot((qk * dm).astype(BF16), sol16)
    o_intra = x[:, :HEAD]
    qp16 = (q * eg - x[:, HEAD:]).astype(BF16)
    kd16 = (k * jnp.exp(gtcol - gcol)).astype(BF16)
    tn = (((0,), (0,)), ((), ()))
    o_parts = []
    for j in range(r // CHUNK):
        sl = slice(j * CHUNK, (j + 1) * CHUNK)
        kc = lax.dot_general(kd16[sl], sol16[sl], tn, preferred_element_type=F32)
        lhs = jnp.concatenate([kc[:, HEAD:].astype(BF16), qp16[sl]], axis=0)
        ks_qs = _dot(lhs, state.astype(BF16))
        o_parts.append(ks_qs[HEAD:] + o_intra[sl])
        state = state * jnp.exp(gtrow[j:j + 1, :]) + kc[:, :HEAD] - ks_qs[:HEAD]
    return jnp.concatenate(o_parts, axis=0), state


def _gdn_kernel(q_ref, k_ref, v_ref, z_ref, gc_ref, bt_ref, gt_ref, gcrow_ref, gtrow_ref, nw_ref,
                o_ref, s_ref):
    pp = pl.program_id(0)
    tb = pl.program_id(1)
    r = GDN_ROWS

    @pl.when(tb == 0)
    def _():
        s_ref[...] = jnp.zeros(s_ref.shape, F32)

    row = lax.broadcasted_iota(I32, (r, r), 0)
    col = lax.broadcasted_iota(I32, (r, r), 1)
    same = (row // CHUNK) == (col // CHUNK)
    causal = same & (col <= row)
    strict = same & (col < row)
    lane = lax.broadcasted_iota(I32, (r, LANES), 1)
    gc_all = gc_ref[...]
    bt_all = bt_ref[...]
    gt_all = gt_ref[...]
    nt = (((1,), (1,)), ((), ()))

    for pi in range(GDN_PAIRS):
        q = q_ref[:, pi * HEAD:(pi + 1) * HEAD]
        k = k_ref[:, pi * HEAD:(pi + 1) * HEAD]
        k16 = k.astype(BF16)
        kk = lax.dot_general(k16, k16, nt, preferred_element_type=F32)
        qk = lax.dot_general(q.astype(BF16), k16, nt, preferred_element_type=F32)
        for hh in range(2):
            hl = 2 * pi + hh
            h = 2 * GDN_PAIRS * pp + hl
            gcol = jnp.sum(jnp.where(lane == h, gc_all, 0.0), axis=1, keepdims=True)
            bcol = jnp.sum(jnp.where(lane == h + N_V_HEADS, bt_all, 0.0), axis=1, keepdims=True)
            gtcol = jnp.sum(jnp.where(lane == h, gt_all, 0.0), axis=1, keepdims=True)
            cols = slice(hl * HEAD, (hl + 1) * HEAD)
            o, state = _gdn_head(q, k, kk, qk, causal, strict, gcol, bcol, gtcol,
                                 gcrow_ref[hl, 0], gtrow_ref[hl, 0], v_ref[:, cols], s_ref[hl])
            s_ref[hl] = state
            o = o * lax.rsqrt(jnp.mean(o * o, axis=-1, keepdims=True) + NORM_EPS) * nw_ref[...]
            o_ref[:, cols] = (o * _silu(z_ref[:, cols])).astype(o_ref.dtype)


def _gdn_call(qk, v, z, gc, bt, gt, gcrow, gtrow, nw):
    t = qk.shape[0]
    r = GDN_ROWS
    p = GDN_PAIRS
    k_off = N_QK_HEADS // p
    return pl.pallas_call(
        _gdn_kernel,
        grid=(N_QK_HEADS // p, t // r),
        in_specs=[
            pl.BlockSpec((r, p * HEAD), lambda pp, tb: (tb, pp)),
            pl.BlockSpec((r, p * HEAD), lambda pp, tb: (tb, k_off + pp)),
            pl.BlockSpec((r, 2 * p * HEAD), lambda pp, tb: (tb, pp)),
            pl.BlockSpec((r, 2 * p * HEAD), lambda pp, tb: (tb, pp)),
            pl.BlockSpec((r, LANES), lambda pp, tb: (tb, 0)),
            pl.BlockSpec((r, LANES), lambda pp, tb: (tb, 0)),
            pl.BlockSpec((r, LANES), lambda pp, tb: (tb, 0)),
            pl.BlockSpec((2 * p, 1, 1, r), lambda pp, tb: (pp, tb, 0, 0)),
            pl.BlockSpec((2 * p, 1, r // CHUNK, LANES), lambda pp, tb: (pp, tb, 0, 0)),
            pl.BlockSpec((1, HEAD), lambda pp, tb: (0, 0)),
        ],
        out_specs=pl.BlockSpec((r, 2 * p * HEAD), lambda pp, tb: (tb, pp)),
        out_shape=jax.ShapeDtypeStruct((t, V_WIDTH), BF16),
        scratch_shapes=[pltpu.VMEM((2 * p, HEAD, HEAD), F32)],
        compiler_params=_params(("parallel", "arbitrary")),
        name="gdn",
    )(qk, qk, v, z, gc, bt, gt, gcrow, gtrow, nw)


def _layer_norm(r, g, b):
    mu = jnp.mean(r, axis=-1, keepdims=True)
    rc = r - mu
    var = jnp.mean(rc * rc, axis=-1, keepdims=True)
    return rc * lax.rsqrt(var + LN_EPS) * g + b


def _dwconv_kernel(c_ref, w_ref, b_ref, g_ref, beta_ref, o_ref, buf, shifted, accbuf):
    i = pl.program_id(0)
    tm = CONV_TM
    ch = buf.shape[1]
    n_shift = HALO + tm - SUBLANES

    @pl.when(i == 0)
    def _():
        buf[0:HALO, :] = jnp.zeros((HALO, ch), F32)

    buf[HALO:HALO + tm, :] = c_ref[...]
    for b in range(1, SUBLANES):
        shifted[b - 1] = buf[pl.ds(b, n_shift), :]

    def row_body(rc, carry):
        r0 = pl.multiple_of(rc * CONV_RC, CONV_RC)
        for cc in range(ch // CONV_CC):
            cs = slice(cc * CONV_CC, (cc + 1) * CONV_CC)
            acc = buf[pl.ds(HALO + r0, CONV_RC), cs] * w_ref[CONV_WIDTH - 1:CONV_WIDTH, cs] + b_ref[:, cs]
            for s in range(CONV_WIDTH - 1):
                a, b = divmod(HALO - (CONV_WIDTH - 1) + s, SUBLANES)
                if b == 0:
                    src = buf[pl.ds(a * SUBLANES + r0, CONV_RC), cs]
                else:
                    src = shifted[b - 1, pl.ds(a * SUBLANES + r0, CONV_RC), cs]
                acc = acc + src * w_ref[s:s + 1, cs]
            accbuf[pl.ds(r0, CONV_RC), cs] = acc
        return carry

    lax.fori_loop(0, tm // CONV_RC, row_body, 0)
    buf[0:HALO, :] = buf[tm:tm + HALO, :]
    o_ref[...] = _silu(_layer_norm(accbuf[...], g_ref[...], beta_ref[...])).astype(o_ref.dtype)


def _dwconv_call(c, w, b, g, beta):
    t, ch = c.shape
    tm = CONV_TM
    vec = pl.BlockSpec((1, ch), lambda i: (0, 0))
    return pl.pallas_call(
        _dwconv_kernel,
        grid=(t // tm,),
        in_specs=[pl.BlockSpec((tm, ch), lambda i: (i, 0)),
                  pl.BlockSpec((CONV_WIDTH, ch), lambda i: (0, 0)), vec, vec, vec],
        out_specs=pl.BlockSpec((tm, ch), lambda i: (i, 0)),
        out_shape=jax.ShapeDtypeStruct((t, ch), BF16),
        scratch_shapes=[pltpu.VMEM((HALO + tm, ch), F32),
                        pltpu.VMEM((SUBLANES - 1, HALO + tm - SUBLANES, ch), F32),
                        pltpu.VMEM((tm, ch), F32)],
        compiler_params=_params(("arbitrary",)),
        name="dwconv_ln",
    )(c, w, b, g, beta)


def _outproj_kernel(m_ref, w_ref, x_ref, g_ref, b_ref, h_ref):
    y = _dot(m_ref[...], w_ref[...])
    h_ref[...] = _layer_norm(DN_ALPHA * x_ref[...] + y, g_ref[...], b_ref[...])


def _outproj_call(mixed, w16, x, g, b):
    t, d = x.shape
    tm = 256
    vec = pl.BlockSpec((1, d), lambda i: (0, 0))
    return pl.pallas_call(
        _outproj_kernel,
        grid=(t // tm,),
        in_specs=[pl.BlockSpec((tm, d), lambda i: (i, 0)), pl.BlockSpec((d, d), lambda i: (0, 0)),
                  pl.BlockSpec((tm, d), lambda i: (i, 0)), vec, vec],
        out_specs=pl.BlockSpec((tm, d), lambda i: (i, 0)),
        out_shape=jax.ShapeDtypeStruct((t, d), F32),
        compiler_params=_params(("parallel",)),
        name="outproj_ln1",
    )(mixed, w16, x, g, b)


def _route_kernel(h_ref, w_ref, bias_ref, idx_ref, wts_ref, rank_ref, cnt_ref, carry):
    @pl.when(pl.program_id(0) == 0)
    def _():
        carry[...] = jnp.zeros(carry.shape, F32)

    logits = jnp.dot(h_ref[...], w_ref[...], precision=lax.Precision.HIGHEST,
                     preferred_element_type=F32)
    scores = _sigmoid(logits)
    biased = scores + bias_ref[...]
    shape = biased.shape
    tm = shape[0]
    lane_i = lax.broadcasted_iota(I32, shape, 1)
    lane = lane_i.astype(F32)
    per_group = N_EXPERTS // N_GROUPS
    grp = (lane_i // per_group).astype(F32)
    neg = -jnp.inf
    big = 1e9

    def rmax(x):
        return jnp.max(x, axis=1, keepdims=True)

    def rmin(x):
        return jnp.min(x, axis=1, keepdims=True)

    def rsum(x):
        return jnp.sum(x, axis=1, keepdims=True)

    gs = jnp.zeros(shape, F32)
    for g in range(N_GROUPS):
        in_g = grp == float(g)
        m = jnp.where(in_g, biased, neg)
        m1 = rmax(m)
        i1 = rmin(jnp.where(m == m1, lane, big))
        m2 = rmax(jnp.where(lane == i1, neg, m))
        gs = jnp.where(in_g, m1 + m2, gs)
    sel_g = jnp.zeros(shape, jnp.bool_)
    cur = gs
    for _ in range(TOPK_GROUPS):
        mx = rmax(cur)
        gi = rmin(jnp.where(cur == mx, grp, big))
        hit = grp == gi
        sel_g = sel_g | hit
        cur = jnp.where(hit, neg, cur)
    masked = jnp.where(sel_g, biased, neg)
    ids, ws = [], []
    sel = jnp.zeros(shape, F32)
    for _ in range(TOP_K):
        mx = rmax(masked)
        ik = rmin(jnp.where(masked == mx, lane, big))
        hit = lane == ik
        ws.append(rsum(jnp.where(hit, scores, 0.0)))
        ids.append(ik)
        sel = jnp.where(hit, 1.0, sel)
        masked = jnp.where(hit, neg, masked)
    w = jnp.concatenate(ws, axis=1)
    wts_ref[...] = w / rsum(w) * ROUTE_SCALE
    idx_ref[...] = jnp.concatenate(ids, axis=1).astype(I32)

    earlier = (lax.broadcasted_iota(I32, (tm, tm), 1) < lax.broadcasted_iota(I32, (tm, tm), 0))
    before = _dot(earlier.astype(BF16), sel.astype(BF16)) + carry[...]
    ranks = [rsum(jnp.where(lane == ik, before, 0.0)) for ik in ids]
    rank_ref[...] = jnp.concatenate(ranks, axis=1).astype(I32)
    carry[...] = carry[...] + jnp.sum(sel, axis=0, keepdims=True)
    cnt_ref[...] = carry[...]


def _route_call(h, w_router, bias):
    t, d = h.shape
    tm = ROUTE_TM
    tok = pl.BlockSpec((tm, TOP_K), lambda i: (i, 0))
    return pl.pallas_call(
        _route_kernel,
        grid=(t // tm,),
        in_specs=[pl.BlockSpec((tm, d), lambda i: (i, 0)),
                  pl.BlockSpec((d, N_EXPERTS), lambda i: (0, 0)),
                  pl.BlockSpec((1, N_EXPERTS), lambda i: (0, 0))],
        out_specs=(tok, tok, tok, pl.BlockSpec((1, N_EXPERTS), lambda i: (0, 0))),
        out_shape=(jax.ShapeDtypeStruct((t, TOP_K), I32), jax.ShapeDtypeStruct((t, TOP_K), F32),
                   jax.ShapeDtypeStruct((t, TOP_K), I32), jax.ShapeDtypeStruct((1, N_EXPERTS), F32)),
        scratch_shapes=[pltpu.VMEM((1, N_EXPERTS), F32)],
        compiler_params=_params(("arbitrary",)),
        name="route",
    )(h, w_router, bias)


def _dest_kernel(idx_ref, rank_ref, start_ref, dest_ref):
    idx = idx_ref[...]
    shape = (idx.shape[0], N_EXPERTS)
    lane = lax.broadcasted_iota(I32, shape, 1)
    start = start_ref[...]
    cols = []
    for kk in range(TOP_K):
        hit = lane == idx[:, kk:kk + 1]
        cols.append(jnp.sum(jnp.where(hit, start, 0.0), axis=1, keepdims=True))
    dest_ref[...] = jnp.concatenate(cols, axis=1).astype(I32) + rank_ref[...]


def _dest_call(idx, rank, start_f32):
    t = idx.shape[0]
    tm = 512
    tok = pl.BlockSpec((tm, TOP_K), lambda i: (i, 0))
    return pl.pallas_call(
        _dest_kernel,
        grid=(t // tm,),
        in_specs=[tok, tok, pl.BlockSpec((1, N_EXPERTS), lambda i: (0, 0))],
        out_specs=tok,
        out_shape=jax.ShapeDtypeStruct((t, TOP_K), I32),
        compiler_params=_params(("parallel",)),
        name="dest",
    )(idx, rank, start_f32)


def _row_copy(src, src_row, dst, dst_row, sem):
    return pltpu.make_async_copy(src.at[pl.ds(src_row, 1), :], dst.at[pl.ds(dst_row, 1), :], sem)


def _rows_wait(src, dst, dst_row, n_rows, sem):
    pltpu.make_async_copy(src.at[pl.ds(0, n_rows), :], dst.at[pl.ds(dst_row, n_rows), :], sem).wait()


def _scatter_kernel(dest_ref, h_ref, xs_hbm, sem):
    def issue(tok, c):
        for kk in range(TOP_K):
            _row_copy(h_ref, tok, xs_hbm, dest_ref[0, 0, tok * TOP_K + kk], sem.at[tok]).start(priority=kk % 2)
        return c

    lax.fori_loop(0, SCATTER_TM, issue, 0, unroll=2)

    def wait(tok, c):
        _rows_wait(h_ref, xs_hbm, 0, TOP_K, sem.at[tok])
        return c

    lax.fori_loop(0, SCATTER_TM, wait, 0, unroll=8)


def _scatter_call(dest, h):
    t, d = h.shape
    tm = SCATTER_TM
    n = tm * TOP_K
    return pl.pallas_call(
        _scatter_kernel,
        grid=(t // tm,),
        in_specs=[pl.BlockSpec((1, 1, n), lambda i: (i, 0, 0), memory_space=pltpu.SMEM),
                  pl.BlockSpec((tm, d), lambda i: (i, 0))],
        out_specs=pl.BlockSpec(memory_space=pl.ANY),
        out_shape=jax.ShapeDtypeStruct((t * TOP_K, d), F32),
        scratch_shapes=[pltpu.SemaphoreType.DMA((tm,))],
        compiler_params=_params(("arbitrary",)),
        name="dispatch",
    )(dest.reshape(t // tm, 1, n), h)


def _weight_copies(w_hbm, w32, wsem, e, slot):
    return [pltpu.make_async_copy(w_hbm[m].at[e], w32[m].at[slot], wsem.at[slot, m])
            for m in range(3)]


def _expert_kernel(vb_ref, ve_ref, lo_ref, hi_ref, fb_ref, fe_ref, par_ref, nxt_ref, nv_ref,
                   x_ref, wg_hbm, wu_hbm, wd_hbm, y_ref,
                   wg32, wu32, wd32, wgu16, wd16, wsem):
    v = pl.program_id(0)
    w_hbm = (wg_hbm, wu_hbm, wd_hbm)
    w32 = (wg32, wu32, wd32)
    w16 = (wgu16.at[:, 0:D_EXPERT], wgu16.at[:, D_EXPERT:2 * D_EXPERT], wd16)

    @pl.when(v < nv_ref[0])
    def _():
        slot = par_ref[v]

        @pl.when(fe_ref[v] == 1)
        def _():
            @pl.when(v == 0)
            def _():
                for cp in _weight_copies(w_hbm, w32, wsem, ve_ref[v], slot):
                    cp.start(priority=WEIGHT_DMA_QUEUE)

            @pl.when(nxt_ref[v] >= 0)
            def _():
                for cp in _weight_copies(w_hbm, w32, wsem, nxt_ref[v], 1 - slot):
                    cp.start(priority=WEIGHT_DMA_QUEUE)

            for m, cp in enumerate(_weight_copies(w_hbm, w32, wsem, ve_ref[v], slot)):
                cp.wait()
                w16[m][...] = w32[m][slot].astype(BF16)

        gu = _dot(x_ref[...].astype(BF16), wgu16[...])
        act = _silu(gu[:, :D_EXPERT]) * gu[:, D_EXPERT:]
        y = _dot(act.astype(BF16), wd16[...])
        rows = lax.broadcasted_iota(I32, y.shape, 0)
        mine = (rows >= lo_ref[v]) & (rows < hi_ref[v])

        @pl.when(fb_ref[v] == 1)
        def _():
            y_ref[...] = jnp.where(mine, y, 0.0)

        @pl.when(fb_ref[v] == 0)
        def _():
            y_ref[...] = jnp.where(mine, y, y_ref[...])


def _expert_call(tables, xs, w_gate, w_up, w_down):
    n_rows, d = xs.shape
    blk = EXPERT_BLOCK
    n_visits = tables[0].shape[0]
    any_spec = pl.BlockSpec(memory_space=pl.ANY)
    grid_spec = pltpu.PrefetchScalarGridSpec(
        num_scalar_prefetch=len(tables),
        grid=(n_visits,),
        in_specs=[pl.BlockSpec((blk, d), lambda v, vb, *_: (vb[v], 0)), any_spec, any_spec, any_spec],
        out_specs=pl.BlockSpec((blk, d), lambda v, vb, *_: (vb[v], 0)),
        scratch_shapes=[
            pltpu.VMEM((2, d, D_EXPERT), F32), pltpu.VMEM((2, d, D_EXPERT), F32),
            pltpu.VMEM((2, D_EXPERT, d), F32),
            pltpu.VMEM((d, 2 * D_EXPERT), BF16), pltpu.VMEM((D_EXPERT, d), BF16),
            pltpu.SemaphoreType.DMA((2, 3)),
        ],
    )
    return pl.pallas_call(
        _expert_kernel,
        grid_spec=grid_spec,
        out_shape=jax.ShapeDtypeStruct((n_rows, d), F32),
        compiler_params=_params(("arbitrary",)),
        name="experts",
    )(*tables, xs, w_gate, w_up, w_down)


def _visit_tables(counts, n_rows):
    blk = EXPERT_BLOCK
    n_visits = n_rows // blk + N_EXPERTS - 1
    end = jnp.cumsum(counts)
    start = end - counts
    nonempty = counts > 0
    first_blk = start // blk
    last_blk = jnp.maximum(end - 1, 0) // blk
    nvis = jnp.where(nonempty, last_blk - first_blk + 1, 0)
    vis_end = jnp.cumsum(nvis)
    vis_start = vis_end - nvis
    n_used = vis_end[-1]
    ids = jnp.arange(N_EXPERTS, dtype=I32)
    ordinal = jnp.cumsum(nonempty.astype(I32)) - 1
    nxt_incl = lax.cummin(jnp.where(nonempty, ids, N_EXPERTS), axis=0, reverse=True)
    nxt_e = jnp.concatenate([nxt_incl[1:], jnp.full((1,), N_EXPERTS, I32)])
    nxt_e = jnp.where(nxt_e >= N_EXPERTS, -1, nxt_e)
    v = jnp.arange(n_visits, dtype=I32)
    vc = jnp.clip(v, 0, jnp.maximum(n_used - 1, 0))
    ve = jnp.minimum(jnp.sum((vis_end[None, :] <= vc[:, None]).astype(I32), axis=1), N_EXPERTS - 1)
    onehot = ve[:, None] == ids[None, :]
    look = lambda tab: jnp.sum(jnp.where(onehot, tab.astype(I32)[None, :], 0), axis=1)
    v_first = look(vis_start)
    vb = look(first_blk) + (vc - v_first)
    lo = jnp.clip(look(start) - vb * blk, 0, blk)
    hi = jnp.clip(look(end) - vb * blk, 0, blk)
    fb = jnp.concatenate([jnp.ones((1,), I32), (vb[1:] != vb[:-1]).astype(I32)])
    fe = (vc == v_first).astype(I32)
    tables = (vb, ve, lo, hi, fb, fe, look(ordinal % 2), look(nxt_e), n_used.astype(I32).reshape(1))
    return tables, start


def _shared_kernel(h_ref, wg_ref, wu_ref, wd_ref, y_ref, wg16, wu16, wd16):
    @pl.when(pl.program_id(0) == 0)
    def _():
        wg16[...] = wg_ref[...].astype(BF16)
        wu16[...] = wu_ref[...].astype(BF16)
        wd16[...] = wd_ref[...].astype(BF16)

    x = h_ref[...].astype(BF16)
    g = _dot(x, wg16[...])
    u = _dot(x, wu16[...])
    y_ref[...] = _dot((_silu(g) * u).astype(BF16), wd16[...])


def _shared_call(h, wg, wu, wd):
    t, d = h.shape
    ds = wg.shape[1]
    tm = 512
    return pl.pallas_call(
        _shared_kernel,
        grid=(t // tm,),
        in_specs=[pl.BlockSpec((tm, d), lambda i: (i, 0)),
                  pl.BlockSpec((d, ds), lambda i: (0, 0)),
                  pl.BlockSpec((d, ds), lambda i: (0, 0)),
                  pl.BlockSpec((ds, d), lambda i: (0, 0))],
        out_specs=pl.BlockSpec((tm, d), lambda i: (i, 0)),
        out_shape=jax.ShapeDtypeStruct((t, d), F32),
        scratch_shapes=[pltpu.VMEM((d, ds), BF16), pltpu.VMEM((d, ds), BF16),
                        pltpu.VMEM((ds, d), BF16)],
        compiler_params=_params(("arbitrary",)),
        name="shared_expert",
    )(h, wg, wu, wd)


def _combine_kernel(dest_ref, dnext_ref, ys_hbm, wts_ref, h_ref, ysh_ref, g_ref, b_ref, o_ref,
                    buf, sem):
    tm = COMBINE_TM
    i = pl.program_id(0)
    slot = i % 2

    def gather(d_ref, s):
        def issue(r, c):
            for kk in range(TOP_K):
                a = kk * tm + r
                _row_copy(ys_hbm, d_ref[0, 0, a], buf.at[s], a, sem.at[s, kk]).start(priority=kk % 2)
            return c

        lax.fori_loop(0, tm, issue, 0, unroll=2)

    @pl.when(i == 0)
    def _():
        gather(dest_ref, slot)

    @pl.when(i + 1 < pl.num_programs(0))
    def _():
        gather(dnext_ref, 1 - slot)

    acc = DN_ALPHA * h_ref[...] + ysh_ref[...]
    wts = wts_ref[...]
    for kk in range(TOP_K):
        _rows_wait(ys_hbm, buf.at[slot], kk * tm, tm, sem.at[slot, kk])
    for kk in range(TOP_K):
        acc = acc + wts[:, kk:kk + 1] * buf[slot, kk * tm:(kk + 1) * tm, :]
    o_ref[...] = _layer_norm(acc, g_ref[...], b_ref[...])


def _combine_call(dest_km, ys, wts, h, ysh, g, b):
    t, d = h.shape
    tm = COMBINE_TM
    n = tm * TOP_K
    n_tiles = t // tm
    vec = pl.BlockSpec((1, d), lambda i: (0, 0))
    return pl.pallas_call(
        _combine_kernel,
        grid=(n_tiles,),
        in_specs=[
            pl.BlockSpec((1, 1, n), lambda i: (i, 0, 0), memory_space=pltpu.SMEM),
            pl.BlockSpec((1, 1, n), lambda i: (jnp.minimum(i + 1, n_tiles - 1), 0, 0),
                         memory_space=pltpu.SMEM),
            pl.BlockSpec(memory_space=pl.ANY),
            pl.BlockSpec((tm, TOP_K), lambda i: (i, 0)),
            pl.BlockSpec((tm, d), lambda i: (i, 0)),
            pl.BlockSpec((tm, d), lambda i: (i, 0)),
            vec, vec,
        ],
        out_specs=pl.BlockSpec((tm, d), lambda i: (i, 0)),
        out_shape=jax.ShapeDtypeStruct((t, d), F32),
        scratch_shapes=[pltpu.VMEM((2, n, d), F32), pltpu.SemaphoreType.DMA((2, TOP_K))],
        compiler_params=_params(("arbitrary",)),
        name="combine_ln2",
    )(dest_km, dest_km, ys, wts, h, ysh, g, b)


def _token_mixing(x, w_in, w_short_conv, a_log, dt_bias, delta_norm_w, w_o_delta, conv_dw_w,
                  conv_dw_b, conv_ln_g, conv_ln_b, w_pw2, b_pw2, w_out, ln1_g, ln1_b):
    t, d = x.shape
    x16 = x.astype(BF16)
    w_t = w_in.T
    row = lambda a: a.reshape(1, -1)
    tile = pl.BlockSpec((TM, TN), lambda j, i: (i, j))
    wide = pl.BlockSpec((TM_WIDE, TN), lambda j, i: (i, j))
    conv_buf = pltpu.VMEM((TM_WIDE + SUBLANES, TN), F32)

    scale = jnp.concatenate([jnp.full((1, QK_WIDTH), HEAD ** -0.5, F32), jnp.ones((1, QK_WIDTH), F32)], axis=1)
    qk = _proj_call(
        "proj_qk", x16, [w_t], [0], 2 * QK_WIDTH // TN, _qk_epilogue,
        [w_short_conv, scale],
        [pl.BlockSpec((SHORT_CONV, TN), lambda j, i: (0, j)), pl.BlockSpec((1, TN), lambda j, i: (0, j))],
        jax.ShapeDtypeStruct((t, 2 * QK_WIDTH), F32), wide, scratch=[conv_buf], tm=TM_WIDE, transposed=True)
    v_off = 2 * QK_WIDTH // TN
    v = _proj_call(
        "proj_v", x16, [w_t], [2 * QK_WIDTH], V_WIDTH // TN, _v_epilogue,
        [w_short_conv], [pl.BlockSpec((SHORT_CONV, TN), lambda j, i: (0, j + v_off))],
        jax.ShapeDtypeStruct((t, V_WIDTH), F32), wide, scratch=[conv_buf], tm=TM_WIDE, transposed=True)
    z = _proj_call(
        "proj_z", x16, [w_t], [COL_Z], V_WIDTH // TN, _z_epilogue, [], [],
        jax.ShapeDtypeStruct((t, V_WIDTH), F32), wide, tm=TM_WIDE, transposed=True)

    pad_h = lambda a: jnp.pad(a.reshape(1, -1), ((0, 0), (0, LANES - N_V_HEADS)))
    lane_vec = pl.BlockSpec((1, LANES), lambda j, i: (0, 0))
    lane_tile = pl.BlockSpec((TM, LANES), lambda j, i: (i, 0))
    lane_shape = jax.ShapeDtypeStruct((t, LANES), F32)
    gc, bt, gt = _proj_call(
        "proj_ab", x16, [w_t], [COL_AB], 1, _ab_epilogue, [pad_h(a_log), pad_h(dt_bias)],
        [lane_vec, lane_vec], (lane_shape, lane_shape, lane_shape), (lane_tile, lane_tile, lane_tile),
        tn=LANES, transposed=True)

    c = _proj_call(
        "proj_glu", x16, [w_t, w_t], [COL_GLU, COL_GLU + d], d // TN, _glu_epilogue, [], [],
        jax.ShapeDtypeStruct((t, d), F32), wide, tm=TM_WIDE, transposed=True)
    gates = _proj_call(
        "proj_gates", x16, [w_t], [COL_GATES], 2 * d // TN, _gates_epilogue, [], [],
        jax.ShapeDtypeStruct((t, 2 * d), F32), wide, tm=TM_WIDE, transposed=True)

    r = GDN_ROWS
    gcrow = gc[:, :N_V_HEADS].T.reshape(N_V_HEADS, t // r, 1, r)
    gt_chunk = gt[CHUNK - 1::CHUNK, :N_V_HEADS].T
    gtrow = jnp.broadcast_to(gt_chunk[:, :, None], (N_V_HEADS, t // CHUNK, LANES))
    gtrow = gtrow.reshape(N_V_HEADS, t // r, r // CHUNK, LANES)
    og = _gdn_call(qk, v, z, gc, bt, gt, gcrow, gtrow, row(delta_norm_w))

    ya = _proj_call(
        "proj_odelta", og, [w_o_delta], [0], d // TN, _odelta_epilogue, [gates], [tile],
        jax.ShapeDtypeStruct((t, d), F32), tile)

    c_act = _dwconv_call(c, conv_dw_w, row(conv_dw_b), row(conv_ln_g), row(conv_ln_b))
    g_off = d // TN
    mixed = _proj_call(
        "proj_pw2", c_act, [w_pw2], [0], d // TN, _pw2_epilogue, [row(b_pw2), gates, ya],
        [pl.BlockSpec((1, TN), lambda j, i: (0, j)),
         pl.BlockSpec((TM_WIDE, TN), lambda j, i: (i, j + g_off)), wide],
        jax.ShapeDtypeStruct((t, d), BF16), wide, tm=TM_WIDE)
    return _outproj_call(mixed, w_out.astype(BF16), x, row(ln1_g), row(ln1_b))


def _moe(h, w_router, router_bias, w_gate, w_up, w_down, w_sh_gate, w_sh_up, w_sh_down, ln2_g, ln2_b):
    t, d = h.shape
    idx, wts, rank, counts = _route_call(h, w_router, router_bias.reshape(1, -1))
    tables, start = _visit_tables(counts.reshape(-1).astype(I32), t * TOP_K)
    dest = _dest_call(idx, rank, start.astype(F32).reshape(1, -1))
    xs = _scatter_call(dest, h)
    ys = _expert_call(tables, xs, w_gate, w_up, w_down)
    ysh = _shared_call(h, w_sh_gate, w_sh_up, w_sh_down)
    tm = COMBINE_TM
    dest_km = dest.reshape(t // tm, tm, TOP_K).transpose(0, 2, 1).reshape(t // tm, 1, tm * TOP_K)
    return _combine_call(dest_km, ys, wts, h, ysh, ln2_g.reshape(1, -1), ln2_b.reshape(1, -1))


def kernel(x, w_in, w_short_conv, a_log, dt_bias, delta_norm_w, w_o_delta, conv_dw_w, conv_dw_b,
           conv_ln_g, conv_ln_b, w_pw2, b_pw2, w_out, ln1_g, ln1_b, w_router, router_bias, w_gate,
           w_up, w_down, w_sh_gate, w_sh_up, w_sh_down, ln2_g, ln2_b):
    batch, seq, d = x.shape
    depth = w_in.shape[0]
    outs = []
    for bi in range(batch):
        h = x[bi]
        for li in range(depth):
            h = _token_mixing(h, w_in[li], w_short_conv[li], a_log[li], dt_bias[li], delta_norm_w[li],
                              w_o_delta[li], conv_dw_w[li], conv_dw_b[li], conv_ln_g[li], conv_ln_b[li],
                              w_pw2[li], b_pw2[li], w_out[li], ln1_g[li], ln1_b[li])
            h = _moe(h, w_router[li], router_bias[li], w_gate[li], w_up[li], w_down[li],
                     w_sh_gate[li], w_sh_up[li], w_sh_down[li], ln2_g[li], ln2_b[li])
        outs.append(h)
    return jnp.stack(outs, axis=0)
```

```python
import jax
import jax.numpy as jnp
from jax import lax
from jax.experimental import pallas as pl
from jax.experimental.pallas import tpu as pltpu

F32 = jnp.float32
BF16 = jnp.bfloat16
I32 = jnp.int32

D_MODEL = 2048
CHUNK = 64
N_QK_HEADS = 16
N_V_HEADS = 32
HEAD = 128
QK_WIDTH = N_QK_HEADS * HEAD
V_WIDTH = N_V_HEADS * HEAD
SHORT_CONV = 4
CONV_WIDTH = 31
N_EXPERTS = 256
TOP_K = 8
N_GROUPS = 8
TOPK_GROUPS = 4
D_EXPERT = 512
ROUTE_SCALE = 2.5
DN_ALPHA = 2.0 ** 0.25
LN_EPS = 1e-5
NORM_EPS = 1e-6
COL_Z = 2 * QK_WIDTH + V_WIDTH
COL_AB = COL_Z + V_WIDTH
COL_GLU = COL_AB + 2 * N_V_HEADS
COL_GATES = COL_GLU + 2 * D_MODEL

LANES = 128
SUBLANES = 8
VMEM_LIMIT = 56 * 1024 * 1024
TM = 512
TM_WIDE = 1024
TN = 512
GDN_ROWS = 256
GDN_PAIRS = 4
CONV_TM = 256
HALO = 32
CONV_RC = 64
CONV_CC = 256
ROW_ALIGN = 64
ROUTE_TM = 256
EXPERT_BLOCK = 128
SCATTER_TM = 128
COMBINE_TM = 64
WEIGHT_DMA_QUEUE = 1


def _params(sem, vmem=VMEM_LIMIT):
    return pltpu.CompilerParams(dimension_semantics=sem, vmem_limit_bytes=vmem)


def _sigmoid(x):
    return jax.nn.sigmoid(x)


def _silu(x):
    return x * jax.nn.sigmoid(x)


def _softplus(x):
    return jnp.maximum(x, 0.0) + jnp.log1p(jnp.exp(-jnp.abs(x)))


def _dot(a, b):
    return jnp.dot(a, b, preferred_element_type=F32)


def _proj_call(name, x, ws, w_offs, n_tiles, epilogue, extras, extra_specs, out_shapes, out_specs,
               scratch=(), tm=TM, tn=TN, transposed=False):
    m, k = x.shape
    single_out = not isinstance(out_shapes, (tuple, list))
    if single_out:
        out_shapes, out_specs = (out_shapes,), (out_specs,)
    nw, ne, no = len(ws), len(extras), len(out_shapes)
    needs_cast = [w.dtype != BF16 for w in ws]
    contract = (((1,), (1,)), ((), ())) if transposed else (((1,), (0,)), ((), ()))

    def body(*refs):
        x_ref = refs[0]
        w_refs = refs[1:1 + nw]
        ex_refs = refs[1 + nw:1 + nw + ne]
        out_refs = refs[1 + nw + ne:1 + nw + ne + no]
        scr = refs[1 + nw + ne + no:]
        wb_refs = scr[:sum(needs_cast)]
        rest = scr[sum(needs_cast):]
        i = pl.program_id(1)
        wsrc, c = [], 0
        for kk in range(nw):
            if needs_cast[kk]:
                wsrc.append(wb_refs[c])
                c += 1
            else:
                wsrc.append(w_refs[kk])

        @pl.when(i == 0)
        def _():
            cc = 0
            for kk in range(nw):
                if needs_cast[kk]:
                    wb_refs[cc][...] = w_refs[kk][...].astype(BF16)
                    cc += 1

        xv = x_ref[...]
        accs = [lax.dot_general(xv, wr[...], contract, preferred_element_type=F32) for wr in wsrc]
        epilogue(i, accs, ex_refs, out_refs, rest)

    in_specs = [pl.BlockSpec((tm, k), lambda j, i: (i, 0))]
    for off in w_offs:
        if transposed:
            in_specs.append(pl.BlockSpec(
                (pl.Element(tn), pl.Element(k)),
                lambda j, i, off=off: (pl.multiple_of(off + tn * j, ROW_ALIGN), 0)))
        else:
            in_specs.append(pl.BlockSpec((k, tn), lambda j, i, off=off: (0, j + off)))
    in_specs += list(extra_specs)
    w_tile = (tn, k) if transposed else (k, tn)
    scratch_shapes = [pltpu.VMEM(w_tile, BF16) for c in needs_cast if c] + list(scratch)
    res = pl.pallas_call(
        body,
        grid=(n_tiles, m // tm),
        in_specs=in_specs,
        out_specs=tuple(out_specs),
        out_shape=tuple(out_shapes),
        scratch_shapes=scratch_shapes,
        compiler_params=_params(("parallel", "arbitrary")),
        name=name,
    )(x, *ws, *extras)
    return res[0] if single_out else res


def _short_conv_silu(i, acc, cw_ref, buf, tm):
    @pl.when(i == 0)
    def _():
        buf[0:SUBLANES, :] = jnp.zeros((SUBLANES, buf.shape[1]), F32)

    buf[SUBLANES:SUBLANES + tm, :] = acc
    cw = cw_ref[...]
    y = acc * cw[SHORT_CONV - 1:SHORT_CONV, :]
    for s in range(SHORT_CONV - 1):
        y = y + buf[pl.ds(SUBLANES - (SHORT_CONV - 1) + s, tm), :] * cw[s:s + 1, :]
    buf[0:SUBLANES, :] = buf[tm:tm + SUBLANES, :]
    return _silu(y)


def _qk_epilogue(i, accs, ex, outs, scr):
    cw_ref, scale_ref = ex
    (buf,) = scr
    y = _short_conv_silu(i, accs[0], cw_ref, buf, accs[0].shape[0])
    parts = []
    for g in range(TN // HEAD):
        yg = y[:, g * HEAD:(g + 1) * HEAD]
        parts.append(yg * lax.rsqrt(jnp.sum(yg * yg, axis=-1, keepdims=True) + NORM_EPS))
    outs[0][...] = jnp.concatenate(parts, axis=1) * scale_ref[...]


def _v_epilogue(i, accs, ex, outs, scr):
    (cw_ref,) = ex
    (buf,) = scr
    outs[0][...] = _short_conv_silu(i, accs[0], cw_ref, buf, accs[0].shape[0])


def _z_epilogue(i, accs, ex, outs, scr):
    outs[0][...] = accs[0]


def _glu_epilogue(i, accs, ex, outs, scr):
    outs[0][...] = accs[0] * _sigmoid(accs[1])


def _gates_epilogue(i, accs, ex, outs, scr):
    outs[0][...] = _sigmoid(accs[0])


def _ab_epilogue(i, accs, ex, outs, scr):
    alog_ref, dtb_ref = ex
    gc_ref, bt_ref, gt_ref = outs
    acc = accs[0]
    g = -jnp.exp(alog_ref[...]) * _softplus(acc + dtb_ref[...])
    row = lax.broadcasted_iota(I32, g.shape, 0) % CHUNK
    s = 1
    while s < CHUNK:
        g = g + jnp.where(row >= s, pltpu.roll(g, s, axis=0), 0.0)
        s *= 2
    n_chunks = g.shape[0] // CHUNK
    tot = g.reshape(n_chunks, CHUNK, LANES)[:, CHUNK - 1:CHUNK, :]
    gc_ref[...] = g
    bt_ref[...] = _sigmoid(acc)
    gt_ref[...] = jnp.broadcast_to(tot, (n_chunks, CHUNK, LANES)).reshape(g.shape)


def _odelta_epilogue(i, accs, ex, outs, scr):
    (gate_ref,) = ex
    outs[0][...] = accs[0] * gate_ref[...]


def _pw2_epilogue(i, accs, ex, outs, scr):
    bias_ref, gate_ref, ya_ref = ex
    outs[0][...] = (ya_ref[...] + gate_ref[...] * (accs[0] + bias_ref[...])).astype(BF16)


class _Head:
    pass


def _gdn_kernel(q_ref, k_ref, v_ref, z_ref, gc_ref, bt_ref, gt_ref, gcrow_ref, gtrow_ref, nw_ref,
                o_ref, s_ref):
    pp = pl.program_id(0)
    tb = pl.program_id(1)
    r = GDN_ROWS
    n_heads = 2 * GDN_PAIRS

    @pl.when(tb == 0)
    def _():
        s_ref[...] = jnp.zeros(s_ref.shape, F32)

    row = lax.broadcasted_iota(I32, (r, r), 0)
    col = lax.broadcasted_iota(I32, (r, r), 1)
    same = (row // CHUNK) == (col // CHUNK)
    causal = same & (col <= row)
    strict = same & (col < row)
    lane = lax.broadcasted_iota(I32, (r, LANES), 1)
    gc_all = gc_ref[...]
    bt_all = bt_ref[...]
    gt_all = gt_ref[...]
    nt = (((1,), (1,)), ((), ()))
    tn = (((0,), (0,)), ((), ()))

    heads = [_Head() for _ in range(n_heads)]
    for pi in range(GDN_PAIRS):
        q = q_ref[:, pi * HEAD:(pi + 1) * HEAD]
        k = k_ref[:, pi * HEAD:(pi + 1) * HEAD]
        k16 = k.astype(BF16)
        kk = lax.dot_general(k16, k16, nt, preferred_element_type=F32)
        qk = lax.dot_general(q.astype(BF16), k16, nt, preferred_element_type=F32)
        for hd in heads[2 * pi:2 * pi + 2]:
            hd.q, hd.k, hd.kk, hd.qk = q, k, kk, qk

    for hl, hd in enumerate(heads):
        h = n_heads * pp + hl
        hd.cols = slice(hl * HEAD, (hl + 1) * HEAD)
        hd.gcol = jnp.sum(jnp.where(lane == h, gc_all, 0.0), axis=1, keepdims=True)
        hd.bcol = jnp.sum(jnp.where(lane == h + N_V_HEADS, bt_all, 0.0), axis=1, keepdims=True)
        hd.gtcol = jnp.sum(jnp.where(lane == h, gt_all, 0.0), axis=1, keepdims=True)
        hd.dm = jnp.exp(jnp.where(causal, hd.gcol - gcrow_ref[hl, 0], -jnp.inf))
        hd.ap = jnp.where(strict, hd.bcol * hd.kk * hd.dm, 0.0)
        hd.n = -hd.ap

    for _ in range(5):
        for hd in heads:
            ap16 = hd.ap.astype(BF16)
            hd.ap = _dot(ap16, ap16)
        for hd in heads:
            hd.n = hd.n + hd.ap + _dot(hd.n.astype(BF16), hd.ap.astype(BF16))

    for hd in heads:
        hd.eg = jnp.exp(hd.gcol)
        hd.rhs = jnp.concatenate([v_ref[:, hd.cols] * hd.bcol, hd.k * (hd.bcol * hd.eg)], axis=1)
    for hd in heads:
        hd.sol16 = (hd.rhs + _dot(hd.n.astype(BF16), hd.rhs.astype(BF16))).astype(BF16)
    for hd in heads:
        hd.x = _dot((hd.qk * hd.dm).astype(BF16), hd.sol16)
    for hl, hd in enumerate(heads):
        hd.qp16 = (hd.q * hd.eg - hd.x[:, HEAD:]).astype(BF16)
        hd.kd16 = (hd.k * jnp.exp(hd.gtcol - hd.gcol)).astype(BF16)
        hd.state = s_ref[hl]
        hd.o = []

    for j in range(r // CHUNK):
        sl = slice(j * CHUNK, (j + 1) * CHUNK)
        for hd in heads:
            hd.kc = lax.dot_general(hd.kd16[sl], hd.sol16[sl], tn, preferred_element_type=F32)
        for hd in heads:
            lhs = jnp.concatenate([hd.kc[:, HEAD:].astype(BF16), hd.qp16[sl]], axis=0)
            hd.ks_qs = _dot(lhs, hd.state.astype(BF16))
        for hl, hd in enumerate(heads):
            hd.o.append(hd.ks_qs[HEAD:] + hd.x[sl, :HEAD])
            hd.state = (hd.state * jnp.exp(gtrow_ref[hl, 0, j:j + 1, :]) + hd.kc[:, :HEAD]
                        - hd.ks_qs[:HEAD])

    for hl, hd in enumerate(heads):
        s_ref[hl] = hd.state
        o = jnp.concatenate(hd.o, axis=0)
        o = o * lax.rsqrt(jnp.mean(o * o, axis=-1, keepdims=True) + NORM_EPS) * nw_ref[...]
        o_ref[:, hd.cols] = (o * _silu(z_ref[:, hd.cols])).astype(o_ref.dtype)


def _gdn_call(qk, v, z, gc, bt, gt, gcrow, gtrow, nw):
    t = qk.shape[0]
    r = GDN_ROWS
    p = GDN_PAIRS
    k_off = N_QK_HEADS // p
    return pl.pallas_call(
        _gdn_kernel,
        grid=(N_QK_HEADS // p, t // r),
        in_specs=[
            pl.BlockSpec((r, p * HEAD), lambda pp, tb: (tb, pp)),
            pl.BlockSpec((r, p * HEAD), lambda pp, tb: (tb, k_off + pp)),
            pl.BlockSpec((r, 2 * p * HEAD), lambda pp, tb: (tb, pp)),
            pl.BlockSpec((r, 2 * p * HEAD), lambda pp, tb: (tb, pp)),
            pl.BlockSpec((r, LANES), lambda pp, tb: (tb, 0)),
            pl.BlockSpec((r, LANES), lambda pp, tb: (tb, 0)),
            pl.BlockSpec((r, LANES), lambda pp, tb: (tb, 0)),
            pl.BlockSpec((2 * p, 1, 1, r), lambda pp, tb: (pp, tb, 0, 0)),
            pl.BlockSpec((2 * p, 1, r // CHUNK, LANES), lambda pp, tb: (pp, tb, 0, 0)),
            pl.BlockSpec((1, HEAD), lambda pp, tb: (0, 0)),
        ],
        out_specs=pl.BlockSpec((r, 2 * p * HEAD), lambda pp, tb: (tb, pp)),
        out_shape=jax.ShapeDtypeStruct((t, V_WIDTH), BF16),
        scratch_shapes=[pltpu.VMEM((2 * p, HEAD, HEAD), F32)],
        compiler_params=_params(("parallel", "arbitrary")),
        name="gdn",
    )(qk, qk, v, z, gc, bt, gt, gcrow, gtrow, nw)


def _layer_norm(r, g, b):
    mu = jnp.mean(r, axis=-1, keepdims=True)
    rc = r - mu
    var = jnp.mean(rc * rc, axis=-1, keepdims=True)
    return rc * lax.rsqrt(var + LN_EPS) * g + b


def _dwconv_kernel(c_ref, w_ref, b_ref, g_ref, beta_ref, o_ref, buf, shifted, accbuf):
    i = pl.program_id(0)
    tm = CONV_TM
    ch = buf.shape[1]
    n_shift = HALO + tm - SUBLANES

    @pl.when(i == 0)
    def _():
        buf[0:HALO, :] = jnp.zeros((HALO, ch), F32)

    buf[HALO:HALO + tm, :] = c_ref[...]
    for b in range(1, SUBLANES):
        shifted[b - 1] = buf[pl.ds(b, n_shift), :]

    def row_body(rc, carry):
        r0 = pl.multiple_of(rc * CONV_RC, CONV_RC)
        for cc in range(ch // CONV_CC):
            cs = slice(cc * CONV_CC, (cc + 1) * CONV_CC)
            acc = buf[pl.ds(HALO + r0, CONV_RC), cs] * w_ref[CONV_WIDTH - 1:CONV_WIDTH, cs] + b_ref[:, cs]
            for s in range(CONV_WIDTH - 1):
                a, b = divmod(HALO - (CONV_WIDTH - 1) + s, SUBLANES)
                if b == 0:
                    src = buf[pl.ds(a * SUBLANES + r0, CONV_RC), cs]
                else:
                    src = shifted[b - 1, pl.ds(a * SUBLANES + r0, CONV_RC), cs]
                acc = acc + src * w_ref[s:s + 1, cs]
            accbuf[pl.ds(r0, CONV_RC), cs] = acc
        return carry

    lax.fori_loop(0, tm // CONV_RC, row_body, 0)
    buf[0:HALO, :] = buf[tm:tm + HALO, :]
    o_ref[...] = _silu(_layer_norm(accbuf[...], g_ref[...], beta_ref[...])).astype(o_ref.dtype)


def _dwconv_call(c, w, b, g, beta):
    t, ch = c.shape
    tm = CONV_TM
    vec = pl.BlockSpec((1, ch), lambda i: (0, 0))
    return pl.pallas_call(
        _dwconv_kernel,
        grid=(t // tm,),
        in_specs=[pl.BlockSpec((tm, ch), lambda i: (i, 0)),
                  pl.BlockSpec((CONV_WIDTH, ch), lambda i: (0, 0)), vec, vec, vec],
        out_specs=pl.BlockSpec((tm, ch), lambda i: (i, 0)),
        out_shape=jax.ShapeDtypeStruct((t, ch), BF16),
        scratch_shapes=[pltpu.VMEM((HALO + tm, ch), F32),
                        pltpu.VMEM((SUBLANES - 1, HALO + tm - SUBLANES, ch), F32),
                        pltpu.VMEM((tm, ch), F32)],
        compiler_params=_params(("arbitrary",)),
        name="dwconv_ln",
    )(c, w, b, g, beta)


def _outproj_kernel(m_ref, w_ref, x_ref, g_ref, b_ref, h_ref):
    y = _dot(m_ref[...], w_ref[...])
    h_ref[...] = _layer_norm(DN_ALPHA * x_ref[...] + y, g_ref[...], b_ref[...])


def _outproj_call(mixed, w16, x, g, b):
    t, d = x.shape
    tm = 256
    vec = pl.BlockSpec((1, d), lambda i: (0, 0))
    return pl.pallas_call(
        _outproj_kernel,
        grid=(t // tm,),
        in_specs=[pl.BlockSpec((tm, d), lambda i: (i, 0)), pl.BlockSpec((d, d), lambda i: (0, 0)),
                  pl.BlockSpec((tm, d), lambda i: (i, 0)), vec, vec],
        out_specs=pl.BlockSpec((tm, d), lambda i: (i, 0)),
        out_shape=jax.ShapeDtypeStruct((t, d), F32),
        compiler_params=_params(("parallel",)),
        name="outproj_ln1",
    )(mixed, w16, x, g, b)


def _route_kernel(h_ref, w_ref, bias_ref, idx_ref, wts_ref, rank_ref, cnt_ref, carry):
    @pl.when(pl.program_id(0) == 0)
    def _():
        carry[...] = jnp.zeros(carry.shape, F32)

    logits = jnp.dot(h_ref[...], w_ref[...], precision=lax.Precision.HIGHEST,
                     preferred_element_type=F32)
    scores = _sigmoid(logits)
    biased = scores + bias_ref[...]
    shape = biased.shape
    tm = shape[0]
    lane_i = lax.broadcasted_iota(I32, shape, 1)
    lane = lane_i.astype(F32)
    per_group = N_EXPERTS // N_GROUPS
    grp = (lane_i // per_group).astype(F32)
    neg = -jnp.inf
    big = 1e9

    def rmax(x):
        return jnp.max(x, axis=1, keepdims=True)

    def rmin(x):
        return jnp.min(x, axis=1, keepdims=True)

    def rsum(x):
        return jnp.sum(x, axis=1, keepdims=True)

    gs = jnp.zeros(shape, F32)
    for g in range(N_GROUPS):
        in_g = grp == float(g)
        m = jnp.where(in_g, biased, neg)
        m1 = rmax(m)
        i1 = rmin(jnp.where(m == m1, lane, big))
        m2 = rmax(jnp.where(lane == i1, neg, m))
        gs = jnp.where(in_g, m1 + m2, gs)
    sel_g = jnp.zeros(shape, jnp.bool_)
    cur = gs
    for _ in range(TOPK_GROUPS):
        mx = rmax(cur)
        gi = rmin(jnp.where(cur == mx, grp, big))
        hit = grp == gi
        sel_g = sel_g | hit
        cur = jnp.where(hit, neg, cur)
    masked = jnp.where(sel_g, biased, neg)
    ids, ws = [], []
    sel = jnp.zeros(shape, F32)
    for _ in range(TOP_K):
        mx = rmax(masked)
        ik = rmin(jnp.where(masked == mx, lane, big))
        hit = lane == ik
        ws.append(rsum(jnp.where(hit, scores, 0.0)))
        ids.append(ik)
        sel = jnp.where(hit, 1.0, sel)
        masked = jnp.where(hit, neg, masked)
    w = jnp.concatenate(ws, axis=1)
    wts_ref[...] = w / rsum(w) * ROUTE_SCALE
    idx_ref[...] = jnp.concatenate(ids, axis=1).astype(I32)

    earlier = (lax.broadcasted_iota(I32, (tm, tm), 1) < lax.broadcasted_iota(I32, (tm, tm), 0))
    before = _dot(earlier.astype(BF16), sel.astype(BF16)) + carry[...]
    ranks = [rsum(jnp.where(lane == ik, before, 0.0)) for ik in ids]
    rank_ref[...] = jnp.concatenate(ranks, axis=1).astype(I32)
    carry[...] = carry[...] + jnp.sum(sel, axis=0, keepdims=True)
    cnt_ref[...] = carry[...]


def _route_call(h, w_router, bias):
    t, d = h.shape
    tm = ROUTE_TM
    tok = pl.BlockSpec((tm, TOP_K), lambda i: (i, 0))
    return pl.pallas_call(
        _route_kernel,
        grid=(t // tm,),
        in_specs=[pl.BlockSpec((tm, d), lambda i: (i, 0)),
                  pl.BlockSpec((d, N_EXPERTS), lambda i: (0, 0)),
                  pl.BlockSpec((1, N_EXPERTS), lambda i: (0, 0))],
        out_specs=(tok, tok, tok, pl.BlockSpec((1, N_EXPERTS), lambda i: (0, 0))),
        out_shape=(jax.ShapeDtypeStruct((t, TOP_K), I32), jax.ShapeDtypeStruct((t, TOP_K), F32),
                   jax.ShapeDtypeStruct((t, TOP_K), I32), jax.ShapeDtypeStruct((1, N_EXPERTS), F32)),
        scratch_shapes=[pltpu.VMEM((1, N_EXPERTS), F32)],
        compiler_params=_params(("arbitrary",)),
        name="route",
    )(h, w_router, bias)


def _dest_kernel(idx_ref, rank_ref, start_ref, dest_ref):
    idx = idx_ref[...]
    shape = (idx.shape[0], N_EXPERTS)
    lane = lax.broadcasted_iota(I32, shape, 1)
    start = start_ref[...]
    cols = []
    for kk in range(TOP_K):
        hit = lane == idx[:, kk:kk + 1]
        cols.append(jnp.sum(jnp.where(hit, start, 0.0), axis=1, keepdims=True))
    dest_ref[...] = jnp.concatenate(cols, axis=1).astype(I32) + rank_ref[...]


def _dest_call(idx, rank, start_f32):
    t = idx.shape[0]
    tm = 512
    tok = pl.BlockSpec((tm, TOP_K), lambda i: (i, 0))
    return pl.pallas_call(
        _dest_kernel,
        grid=(t // tm,),
        in_specs=[tok, tok, pl.BlockSpec((1, N_EXPERTS), lambda i: (0, 0))],
        out_specs=tok,
        out_shape=jax.ShapeDtypeStruct((t, TOP_K), I32),
        compiler_params=_params(("parallel",)),
        name="dest",
    )(idx, rank, start_f32)


def _row_copy(src, src_row, dst, dst_row, sem):
    return pltpu.make_async_copy(src.at[pl.ds(src_row, 1), :], dst.at[pl.ds(dst_row, 1), :], sem)


def _rows_wait(src, dst, dst_row, n_rows, sem):
    pltpu.make_async_copy(src.at[pl.ds(0, n_rows), :], dst.at[pl.ds(dst_row, n_rows), :], sem).wait()


def _scatter_kernel(dest_ref, h_ref, xs_hbm, sem):
    def issue(tok, c):
        for kk in range(TOP_K):
            _row_copy(h_ref, tok, xs_hbm, dest_ref[0, 0, tok * TOP_K + kk], sem.at[tok]).start(priority=kk % 2)
        return c

    lax.fori_loop(0, SCATTER_TM, issue, 0, unroll=2)

    def wait(tok, c):
        _rows_wait(h_ref, xs_hbm, 0, TOP_K, sem.at[tok])
        return c

    lax.fori_loop(0, SCATTER_TM, wait, 0, unroll=8)


def _scatter_call(dest, h):
    t, d = h.shape
    tm = SCATTER_TM
    n = tm * TOP_K
    return pl.pallas_call(
        _scatter_kernel,
        grid=(t // tm,),
        in_specs=[pl.BlockSpec((1, 1, n), lambda i: (i, 0, 0), memory_space=pltpu.SMEM),
                  pl.BlockSpec((tm, d), lambda i: (i, 0))],
        out_specs=pl.BlockSpec(memory_space=pl.ANY),
        out_shape=jax.ShapeDtypeStruct((t * TOP_K, d), F32),
        scratch_shapes=[pltpu.SemaphoreType.DMA((tm,))],
        compiler_params=_params(("arbitrary",)),
        name="dispatch",
    )(dest.reshape(t // tm, 1, n), h)


def _weight_copies(w_hbm, w32, wsem, e, slot):
    return [pltpu.make_async_copy(w_hbm[m].at[e], w32[m].at[slot], wsem.at[slot, m])
            for m in range(3)]


def _expert_kernel(vb_ref, ve_ref, lo_ref, hi_ref, fb_ref, fe_ref, par_ref, nxt_ref, nv_ref,
                   x_ref, wg_hbm, wu_hbm, wd_hbm, y_ref,
                   wg32, wu32, wd32, wgu16, wd16, wsem):
    v = pl.program_id(0)
    w_hbm = (wg_hbm, wu_hbm, wd_hbm)
    w32 = (wg32, wu32, wd32)
    w16 = (wgu16.at[:, 0:D_EXPERT], wgu16.at[:, D_EXPERT:2 * D_EXPERT], wd16)

    @pl.when(v < nv_ref[0])
    def _():
        slot = par_ref[v]

        @pl.when(fe_ref[v] == 1)
        def _():
            @pl.when(v == 0)
            def _():
                for cp in _weight_copies(w_hbm, w32, wsem, ve_ref[v], slot):
                    cp.start(priority=WEIGHT_DMA_QUEUE)

            @pl.when(nxt_ref[v] >= 0)
            def _():
                for cp in _weight_copies(w_hbm, w32, wsem, nxt_ref[v], 1 - slot):
                    cp.start(priority=WEIGHT_DMA_QUEUE)

            for m, cp in enumerate(_weight_copies(w_hbm, w32, wsem, ve_ref[v], slot)):
                cp.wait()
                w16[m][...] = w32[m][slot].astype(BF16)

        gu = _dot(x_ref[...].astype(BF16), wgu16[...])
        act = _silu(gu[:, :D_EXPERT]) * gu[:, D_EXPERT:]
        y = _dot(act.astype(BF16), wd16[...])
        rows = lax.broadcasted_iota(I32, y.shape, 0)
        mine = (rows >= lo_ref[v]) & (rows < hi_ref[v])

        @pl.when(fb_ref[v] == 1)
        def _():
            y_ref[...] = jnp.where(mine, y, 0.0)

        @pl.when(fb_ref[v] == 0)
        def _():
            y_ref[...] = jnp.where(mine, y, y_ref[...])


def _expert_call(tables, xs, w_gate, w_up, w_down):
    n_rows, d = xs.shape
    blk = EXPERT_BLOCK
    n_visits = tables[0].shape[0]
    any_spec = pl.BlockSpec(memory_space=pl.ANY)
    grid_spec = pltpu.PrefetchScalarGridSpec(
        num_scalar_prefetch=len(tables),
        grid=(n_visits,),
        in_specs=[pl.BlockSpec((blk, d), lambda v, vb, *_: (vb[v], 0)), any_spec, any_spec, any_spec],
        out_specs=pl.BlockSpec((blk, d), lambda v, vb, *_: (vb[v], 0)),
        scratch_shapes=[
            pltpu.VMEM((2, d, D_EXPERT), F32), pltpu.VMEM((2, d, D_EXPERT), F32),
            pltpu.VMEM((2, D_EXPERT, d), F32),
            pltpu.VMEM((d, 2 * D_EXPERT), BF16), pltpu.VMEM((D_EXPERT, d), BF16),
            pltpu.SemaphoreType.DMA((2, 3)),
        ],
    )
    return pl.pallas_call(
        _expert_kernel,
        grid_spec=grid_spec,
        out_shape=jax.ShapeDtypeStruct((n_rows, d), F32),
        compiler_params=_params(("arbitrary",)),
        name="experts",
    )(*tables, xs, w_gate, w_up, w_down)


def _visit_tables(counts, n_rows):
    blk = EXPERT_BLOCK
    n_visits = n_rows // blk + N_EXPERTS - 1
    end = jnp.cumsum(counts)
    start = end - counts
    nonempty = counts > 0
    first_blk = start // blk
    last_blk = jnp.maximum(end - 1, 0) // blk
    nvis = jnp.where(nonempty, last_blk - first_blk + 1, 0)
    vis_end = jnp.cumsum(nvis)
    vis_start = vis_end - nvis
    n_used = vis_end[-1]
    ids = jnp.arange(N_EXPERTS, dtype=I32)
    ordinal = jnp.cumsum(nonempty.astype(I32)) - 1
    nxt_incl = lax.cummin(jnp.where(nonempty, ids, N_EXPERTS), axis=0, reverse=True)
    nxt_e = jnp.concatenate([nxt_incl[1:], jnp.full((1,), N_EXPERTS, I32)])
    nxt_e = jnp.where(nxt_e >= N_EXPERTS, -1, nxt_e)
    v = jnp.arange(n_visits, dtype=I32)
    vc = jnp.clip(v, 0, jnp.maximum(n_used - 1, 0))
    ve = jnp.minimum(jnp.sum((vis_end[None, :] <= vc[:, None]).astype(I32), axis=1), N_EXPERTS - 1)
    onehot = ve[:, None] == ids[None, :]
    look = lambda tab: jnp.sum(jnp.where(onehot, tab.astype(I32)[None, :], 0), axis=1)
    v_first = look(vis_start)
    vb = look(first_blk) + (vc - v_first)
    lo = jnp.clip(look(start) - vb * blk, 0, blk)
    hi = jnp.clip(look(end) - vb * blk, 0, blk)
    fb = jnp.concatenate([jnp.ones((1,), I32), (vb[1:] != vb[:-1]).astype(I32)])
    fe = (vc == v_first).astype(I32)
    tables = (vb, ve, lo, hi, fb, fe, look(ordinal % 2), look(nxt_e), n_used.astype(I32).reshape(1))
    return tables, start


def _shared_kernel(h_ref, wg_ref, wu_ref, wd_ref, y_ref, wg16, wu16, wd16):
    @pl.when(pl.program_id(0) == 0)
    def _():
        wg16[...] = wg_ref[...].astype(BF16)
        wu16[...] = wu_ref[...].astype(BF16)
        wd16[...] = wd_ref[...].astype(BF16)

    x = h_ref[...].astype(BF16)
    g = _dot(x, wg16[...])
    u = _dot(x, wu16[...])
    y_ref[...] = _dot((_silu(g) * u).astype(BF16), wd16[...])


def _shared_call(h, wg, wu, wd):
    t, d = h.shape
    ds = wg.shape[1]
    tm = 512
    return pl.pallas_call(
        _shared_kernel,
        grid=(t // tm,),
        in_specs=[pl.BlockSpec((tm, d), lambda i: (i, 0)),
                  pl.BlockSpec((d, ds), lambda i: (0, 0)),
                  pl.BlockSpec((d, ds), lambda i: (0, 0)),
                  pl.BlockSpec((ds, d), lambda i: (0, 0))],
        out_specs=pl.BlockSpec((tm, d), lambda i: (i, 0)),
        out_shape=jax.ShapeDtypeStruct((t, d), F32),
        scratch_shapes=[pltpu.VMEM((d, ds), BF16), pltpu.VMEM((d, ds), BF16),
                        pltpu.VMEM((ds, d), BF16)],
        compiler_params=_params(("arbitrary",)),
        name="shared_expert",
    )(h, wg, wu, wd)


def _combine_kernel(dest_ref, dnext_ref, ys_hbm, wts_ref, h_ref, ysh_ref, g_ref, b_ref, o_ref,
                    buf, sem):
    tm = COMBINE_TM
    i = pl.program_id(0)
    slot = i % 2

    def gather(d_ref, s):
        def issue(r, c):
            for kk in range(TOP_K):
                a = kk * tm + r
                _row_copy(ys_hbm, d_ref[0, 0, a], buf.at[s], a, sem.at[s, kk]).start(priority=kk % 2)
            return c

        lax.fori_loop(0, tm, issue, 0, unroll=2)

    @pl.when(i == 0)
    def _():
        gather(dest_ref, slot)

    @pl.when(i + 1 < pl.num_programs(0))
    def _():
        gather(dnext_ref, 1 - slot)

    acc = DN_ALPHA * h_ref[...] + ysh_ref[...]
    wts = wts_ref[...]
    for kk in range(TOP_K):
        _rows_wait(ys_hbm, buf.at[slot], kk * tm, tm, sem.at[slot, kk])
    for kk in range(TOP_K):
        acc = acc + wts[:, kk:kk + 1] * buf[slot, kk * tm:(kk + 1) * tm, :]
    o_ref[...] = _layer_norm(acc, g_ref[...], b_ref[...])


def _combine_call(dest_km, ys, wts, h, ysh, g, b):
    t, d = h.shape
    tm = COMBINE_TM
    n = tm * TOP_K
    n_tiles = t // tm
    vec = pl.BlockSpec((1, d), lambda i: (0, 0))
    return pl.pallas_call(
        _combine_kernel,
        grid=(n_tiles,),
        in_specs=[
            pl.BlockSpec((1, 1, n), lambda i: (i, 0, 0), memory_space=pltpu.SMEM),
            pl.BlockSpec((1, 1, n), lambda i: (jnp.minimum(i + 1, n_tiles - 1), 0, 0),
                         memory_space=pltpu.SMEM),
            pl.BlockSpec(memory_space=pl.ANY),
            pl.BlockSpec((tm, TOP_K), lambda i: (i, 0)),
            pl.BlockSpec((tm, d), lambda i: (i, 0)),
            pl.BlockSpec((tm, d), lambda i: (i, 0)),
            vec, vec,
        ],
        out_specs=pl.BlockSpec((tm, d), lambda i: (i, 0)),
        out_shape=jax.ShapeDtypeStruct((t, d), F32),
        scratch_shapes=[pltpu.VMEM((2, n, d), F32), pltpu.SemaphoreType.DMA((2, TOP_K))],
        compiler_params=_params(("arbitrary",)),
        name="combine_ln2",
    )(dest_km, dest_km, ys, wts, h, ysh, g, b)


def _token_mixing(x, w_in, w_short_conv, a_log, dt_bias, delta_norm_w, w_o_delta, conv_dw_w,
                  conv_dw_b, conv_ln_g, conv_ln_b, w_pw2, b_pw2, w_out, ln1_g, ln1_b):
    t, d = x.shape
    x16 = x.astype(BF16)
    w_t = w_in.T
    row = lambda a: a.reshape(1, -1)
    tile = pl.BlockSpec((TM, TN), lambda j, i: (i, j))
    wide = pl.BlockSpec((TM_WIDE, TN), lambda j, i: (i, j))
    conv_buf = pltpu.VMEM((TM_WIDE + SUBLANES, TN), F32)

    scale = jnp.concatenate([jnp.full((1, QK_WIDTH), HEAD ** -0.5, F32), jnp.ones((1, QK_WIDTH), F32)], axis=1)
    qk = _proj_call(
        "proj_qk", x16, [w_t], [0], 2 * QK_WIDTH // TN, _qk_epilogue,
        [w_short_conv, scale],
        [pl.BlockSpec((SHORT_CONV, TN), lambda j, i: (0, j)), pl.BlockSpec((1, TN), lambda j, i: (0, j))],
        jax.ShapeDtypeStruct((t, 2 * QK_WIDTH), F32), wide, scratch=[conv_buf], tm=TM_WIDE, transposed=True)
    v_off = 2 * QK_WIDTH // TN
    v = _proj_call(
        "proj_v", x16, [w_t], [2 * QK_WIDTH], V_WIDTH // TN, _v_epilogue,
        [w_short_conv], [pl.BlockSpec((SHORT_CONV, TN), lambda j, i: (0, j + v_off))],
        jax.ShapeDtypeStruct((t, V_WIDTH), F32), wide, scratch=[conv_buf], tm=TM_WIDE, transposed=True)
    z = _proj_call(
        "proj_z", x16, [w_t], [COL_Z], V_WIDTH // TN, _z_epilogue, [], [],
        jax.ShapeDtypeStruct((t, V_WIDTH), F32), wide, tm=TM_WIDE, transposed=True)

    pad_h = lambda a: jnp.pad(a.reshape(1, -1), ((0, 0), (0, LANES - N_V_HEADS)))
    lane_vec = pl.BlockSpec((1, LANES), lambda j, i: (0, 0))
    lane_tile = pl.BlockSpec((TM, LANES), lambda j, i: (i, 0))
    lane_shape = jax.ShapeDtypeStruct((t, LANES), F32)
    gc, bt, gt = _proj_call(
        "proj_ab", x16, [w_t], [COL_AB], 1, _ab_epilogue, [pad_h(a_log), pad_h(dt_bias)],
        [lane_vec, lane_vec], (lane_shape, lane_shape, lane_shape), (lane_tile, lane_tile, lane_tile),
        tn=LANES, transposed=True)

    c = _proj_call(
        "proj_glu", x16, [w_t, w_t], [COL_GLU, COL_GLU + d], d // TN, _glu_epilogue, [], [],
        jax.ShapeDtypeStruct((t, d), F32), wide, tm=TM_WIDE, transposed=True)
    gates = _proj_call(
        "proj_gates", x16, [w_t], [COL_GATES], 2 * d // TN, _gates_epilogue, [], [],
        jax.ShapeDtypeStruct((t, 2 * d), F32), wide, tm=TM_WIDE, transposed=True)

    r = GDN_ROWS
    gcrow = gc[:, :N_V_HEADS].T.reshape(N_V_HEADS, t // r, 1, r)
    gt_chunk = gt[CHUNK - 1::CHUNK, :N_V_HEADS].T
    gtrow = jnp.broadcast_to(gt_chunk[:, :, None], (N_V_HEADS, t // CHUNK, LANES))
    gtrow = gtrow.reshape(N_V_HEADS, t // r, r // CHUNK, LANES)
    og = _gdn_call(qk, v, z, gc, bt, gt, gcrow, gtrow, row(delta_norm_w))

    ya = _proj_call(
        "proj_odelta", og, [w_o_delta], [0], d // TN, _odelta_epilogue, [gates], [tile],
        jax.ShapeDtypeStruct((t, d), F32), tile)

    c_act = _dwconv_call(c, conv_dw_w, row(conv_dw_b), row(conv_ln_g), row(conv_ln_b))
    g_off = d // TN
    mixed = _proj_call(
        "proj_pw2", c_act, [w_pw2], [0], d // TN, _pw2_epilogue, [row(b_pw2), gates, ya],
        [pl.BlockSpec((1, TN), lambda j, i: (0, j)),
         pl.BlockSpec((TM_WIDE, TN), lambda j, i: (i, j + g_off)), wide],
        jax.ShapeDtypeStruct((t, d), BF16), wide, tm=TM_WIDE)
    return _outproj_call(mixed, w_out.astype(BF16), x, row(ln1_g), row(ln1_b))


def _moe(h, w_router, router_bias, w_gate, w_up, w_down, w_sh_gate, w_sh_up, w_sh_down, ln2_g, ln2_b):
    t, d = h.shape
    idx, wts, rank, counts = _route_call(h, w_router, router_bias.reshape(1, -1))
    tables, start = _visit_tables(counts.reshape(-1).astype(I32), t * TOP_K)
    dest = _dest_call(idx, rank, start.astype(F32).reshape(1, -1))
    xs = _scatter_call(dest, h)
    ys = _expert_call(tables, xs, w_gate, w_up, w_down)
    ysh = _shared_call(h, w_sh_gate, w_sh_up, w_sh_down)
    tm = COMBINE_TM
    dest_km = dest.reshape(t // tm, tm, TOP_K).transpose(0, 2, 1).reshape(t // tm, 1, tm * TOP_K)
    return _combine_call(dest_km, ys, wts, h, ysh, ln2_g.reshape(1, -1), ln2_b.reshape(1, -1))


def kernel(x, w_in, w_short_conv, a_log, dt_bias, delta_norm_w, w_o_delta, conv_dw_w, conv_dw_b,
           conv_ln_g, conv_ln_b, w_pw2, b_pw2, w_out, ln1_g, ln1_b, w_router, router_bias, w_gate,
           w_up, w_down, w_sh_gate, w_sh_up, w_sh_down, ln2_g, ln2_b):
    batch, seq, d = x.shape
    depth = w_in.shape[0]
    outs = []
    for bi in range(batch):
        h = x[bi]
        for li in range(depth):
            h = _token_mixing(h, w_in[li], w_short_conv[li], a_log[li], dt_bias[li], delta_norm_w[li],
                              w_o_delta[li], conv_dw_w[li], conv_dw_b[li], conv_ln_g[li], conv_ln_b[li],
                              w_pw2[li], b_pw2[li], w_out[li], ln1_g[li], ln1_b[li])
            h = _moe(h, w_router[li], router_bias[li], w_gate[li], w_up[li], w_down[li],
                     w_sh_gate[li], w_sh_up[li], w_sh_down[li], ln2_g[li], ln2_b[li])
        outs.append(h)
    return jnp.stack(outs, axis=0)
```

```python
import jax
import jax.numpy as jnp
from jax import lax
from jax.experimental import pallas as pl
from jax.experimental.pallas import tpu as pltpu

F32 = jnp.float32
BF16 = jnp.bfloat16
I32 = jnp.int32
U32 = jnp.uint32

D_MODEL = 2048
CHUNK = 64
N_QK_HEADS = 16
N_V_HEADS = 32
HEAD = 128
QK_WIDTH = N_QK_HEADS * HEAD
V_WIDTH = N_V_HEADS * HEAD
SHORT_CONV = 4
CONV_WIDTH = 31
N_EXPERTS = 256
TOP_K = 8
N_GROUPS = 8
TOPK_GROUPS = 4
D_EXPERT = 512
ROUTE_SCALE = 2.5
DN_ALPHA = 2.0 ** 0.25
LN_EPS = 1e-5
NORM_EPS = 1e-6
COL_Z = 2 * QK_WIDTH + V_WIDTH
COL_AB = COL_Z + V_WIDTH
COL_GLU = COL_AB + 2 * N_V_HEADS
COL_GATES = COL_GLU + 2 * D_MODEL

LANES = 128
SUBLANES = 8
VMEM_LIMIT = 56 * 1024 * 1024
TM = 512
TM_WIDE = 1024
TN = 512
GDN_ROWS = 256
GDN_PAIRS = 4
CONV_TM = 256
HALO = 32
CONV_RC = 64
CONV_CC = 256
ROW_ALIGN = 64
ROUTE_TM = 256
EXPERT_BLOCK = 128
SCATTER_TM = 128
COMBINE_TM = 64
WEIGHT_DMA_QUEUE = 1


def _params(sem, vmem=VMEM_LIMIT):
    return pltpu.CompilerParams(dimension_semantics=sem, vmem_limit_bytes=vmem)


def _sigmoid(x):
    return jax.nn.sigmoid(x)


def _silu(x):
    return x * jax.nn.sigmoid(x)


def _softplus(x):
    return jnp.maximum(x, 0.0) + jnp.log1p(jnp.exp(-jnp.abs(x)))


def _dot(a, b):
    return jnp.dot(a, b, preferred_element_type=F32)


def _proj_call(name, x, ws, w_offs, n_tiles, epilogue, extras, extra_specs, out_shapes, out_specs,
               scratch=(), tm=TM, tn=TN, transposed=False):
    m, k = x.shape
    single_out = not isinstance(out_shapes, (tuple, list))
    if single_out:
        out_shapes, out_specs = (out_shapes,), (out_specs,)
    nw, ne, no = len(ws), len(extras), len(out_shapes)
    needs_cast = [w.dtype != BF16 for w in ws]
    contract = (((1,), (1,)), ((), ())) if transposed else (((1,), (0,)), ((), ()))

    def body(*refs):
        x_ref = refs[0]
        w_refs = refs[1:1 + nw]
        ex_refs = refs[1 + nw:1 + nw + ne]
        out_refs = refs[1 + nw + ne:1 + nw + ne + no]
        scr = refs[1 + nw + ne + no:]
        wb_refs = scr[:sum(needs_cast)]
        rest = scr[sum(needs_cast):]
        i = pl.program_id(1)
        wsrc, c = [], 0
        for kk in range(nw):
            if needs_cast[kk]:
                wsrc.append(wb_refs[c])
                c += 1
            else:
                wsrc.append(w_refs[kk])

        @pl.when(i == 0)
        def _():
            cc = 0
            for kk in range(nw):
                if needs_cast[kk]:
                    wb_refs[cc][...] = w_refs[kk][...].astype(BF16)
                    cc += 1

        xv = x_ref[...]
        accs = [lax.dot_general(xv, wr[...], contract, preferred_element_type=F32) for wr in wsrc]
        epilogue(i, accs, ex_refs, out_refs, rest)

    in_specs = [pl.BlockSpec((tm, k), lambda j, i: (i, 0))]
    for off in w_offs:
        if transposed:
            in_specs.append(pl.BlockSpec(
                (pl.Element(tn), pl.Element(k)),
                lambda j, i, off=off: (pl.multiple_of(off + tn * j, ROW_ALIGN), 0)))
        else:
            in_specs.append(pl.BlockSpec((k, tn), lambda j, i, off=off: (0, j + off)))
    in_specs += list(extra_specs)
    w_tile = (tn, k) if transposed else (k, tn)
    scratch_shapes = [pltpu.VMEM(w_tile, BF16) for c in needs_cast if c] + list(scratch)
    res = pl.pallas_call(
        body,
        grid=(n_tiles, m // tm),
        in_specs=in_specs,
        out_specs=tuple(out_specs),
        out_shape=tuple(out_shapes),
        scratch_shapes=scratch_shapes,
        compiler_params=_params(("parallel", "arbitrary")),
        name=name,
    )(x, *ws, *extras)
    return res[0] if single_out else res


def _short_conv_silu(i, acc, cw_ref, buf, tm):
    @pl.when(i == 0)
    def _():
        buf[0:SUBLANES, :] = jnp.zeros((SUBLANES, buf.shape[1]), F32)

    buf[SUBLANES:SUBLANES + tm, :] = acc
    cw = cw_ref[...]
    y = acc * cw[SHORT_CONV - 1:SHORT_CONV, :]
    for s in range(SHORT_CONV - 1):
        y = y + buf[pl.ds(SUBLANES - (SHORT_CONV - 1) + s, tm), :] * cw[s:s + 1, :]
    buf[0:SUBLANES, :] = buf[tm:tm + SUBLANES, :]
    return _silu(y)


def _qk_epilogue(i, accs, ex, outs, scr):
    cw_ref, scale_ref = ex
    (buf,) = scr
    y = _short_conv_silu(i, accs[0], cw_ref, buf, accs[0].shape[0])
    parts = []
    for g in range(TN // HEAD):
        yg = y[:, g * HEAD:(g + 1) * HEAD]
        parts.append(yg * lax.rsqrt(jnp.sum(yg * yg, axis=-1, keepdims=True) + NORM_EPS))
    outs[0][...] = jnp.concatenate(parts, axis=1) * scale_ref[...]


def _v_epilogue(i, accs, ex, outs, scr):
    (cw_ref,) = ex
    (buf,) = scr
    outs[0][...] = _short_conv_silu(i, accs[0], cw_ref, buf, accs[0].shape[0])


def _z_epilogue(i, accs, ex, outs, scr):
    outs[0][...] = accs[0]


def _glu_epilogue(i, accs, ex, outs, scr):
    outs[0][...] = accs[0] * _sigmoid(accs[1])


def _gates_epilogue(i, accs, ex, outs, scr):
    outs[0][...] = _sigmoid(accs[0])


def _ab_epilogue(i, accs, ex, outs, scr):
    alog_ref, dtb_ref = ex
    gc_ref, bt_ref, gt_ref = outs
    acc = accs[0]
    g = -jnp.exp(alog_ref[...]) * _softplus(acc + dtb_ref[...])
    row = lax.broadcasted_iota(I32, g.shape, 0) % CHUNK
    s = 1
    while s < CHUNK:
        g = g + jnp.where(row >= s, pltpu.roll(g, s, axis=0), 0.0)
        s *= 2
    n_chunks = g.shape[0] // CHUNK
    tot = g.reshape(n_chunks, CHUNK, LANES)[:, CHUNK - 1:CHUNK, :]
    gc_ref[...] = g
    bt_ref[...] = _sigmoid(acc)
    gt_ref[...] = jnp.broadcast_to(tot, (n_chunks, CHUNK, LANES)).reshape(g.shape)


def _odelta_epilogue(i, accs, ex, outs, scr):
    (gate_ref,) = ex
    outs[0][...] = accs[0] * gate_ref[...]


def _pw2_epilogue(i, accs, ex, outs, scr):
    bias_ref, gate_ref, ya_ref = ex
    outs[0][...] = (ya_ref[...] + gate_ref[...] * (accs[0] + bias_ref[...])).astype(BF16)


class _Head:
    pass


def _gdn_kernel(q_ref, k_ref, v_ref, z_ref, gc_ref, bt_ref, gt_ref, gcrow_ref, btrow_ref, gtrow_ref,
                nw_ref, o_ref, s_ref):
    pp = pl.program_id(0)
    tb = pl.program_id(1)
    r = GDN_ROWS
    half = r // 2
    n_heads = 2 * GDN_PAIRS

    @pl.when(tb == 0)
    def _():
        s_ref[...] = jnp.zeros(s_ref.shape, F32)

    col_tok = lax.broadcasted_iota(I32, (half, r), 0)
    row_tok = lax.broadcasted_iota(I32, (half, r), 1)
    first_half = row_tok < half
    row_in_half = row_tok % half
    same = (row_in_half // CHUNK) == (col_tok // CHUNK)
    causal = same & (col_tok <= row_in_half)
    strict = same & (col_tok < row_in_half)
    lane = lax.broadcasted_iota(I32, (r, LANES), 1)
    gc_all = gc_ref[...]
    bt_all = bt_ref[...]
    gt_all = gt_ref[...]
    nt = (((1,), (1,)), ((), ()))
    tn = (((0,), (0,)), ((), ()))

    def transpose_bd(xc):
        return jnp.concatenate([jnp.where(first_half, xc, 0.0), jnp.where(first_half, 0.0, xc)], axis=0)

    heads = [_Head() for _ in range(n_heads)]
    for pi in range(GDN_PAIRS):
        q = q_ref[:, pi * HEAD:(pi + 1) * HEAD]
        k = k_ref[:, pi * HEAD:(pi + 1) * HEAD]
        k16 = k.astype(BF16)
        q16 = q.astype(BF16)
        kq = [lax.dot_general(k16[rs], jnp.concatenate([k16[rs], q16[rs]], axis=0), nt,
                              preferred_element_type=F32) for rs in (slice(0, half), slice(half, r))]
        kk = jnp.concatenate([kq[0][:, :half], kq[1][:, :half]], axis=1)
        qk = jnp.concatenate([kq[0][:, half:], kq[1][:, half:]], axis=1)
        for hd in heads[2 * pi:2 * pi + 2]:
            hd.q, hd.k, hd.kk, hd.qk = q, k, kk, qk

    for hl, hd in enumerate(heads):
        h = n_heads * pp + hl
        hd.cols = slice(hl * HEAD, (hl + 1) * HEAD)
        hd.gcol = jnp.sum(jnp.where(lane == h, gc_all, 0.0), axis=1, keepdims=True)
        hd.bcol = jnp.sum(jnp.where(lane == h + N_V_HEADS, bt_all, 0.0), axis=1, keepdims=True)
        hd.gtcol = jnp.sum(jnp.where(lane == h, gt_all, 0.0), axis=1, keepdims=True)
        gc_col_tok = jnp.concatenate([jnp.broadcast_to(hd.gcol[:half], (half, half)),
                                      jnp.broadcast_to(hd.gcol[half:], (half, half))], axis=1)
        hd.dm = jnp.exp(jnp.where(causal, gcrow_ref[hl, 0] - gc_col_tok, -jnp.inf))
        hd.p = jnp.where(strict, btrow_ref[hl, 0] * hd.kk * hd.dm, 0.0)
        hd.n = -hd.p

    for hd in heads:
        hd.res = _dot(hd.p.astype(BF16), transpose_bd(hd.p).astype(BF16))
    for hd in heads:
        hd.p = hd.res
    for _ in range(4):
        for hd in heads:
            rhs = jnp.concatenate([transpose_bd(hd.p), transpose_bd(hd.n)], axis=1).astype(BF16)
            hd.res = _dot(hd.p.astype(BF16), rhs)
        for hd in heads:
            hd.n = hd.n + hd.p + hd.res[:, r:]
            hd.p = hd.res[:, :r]
    for hd in heads:
        hd.res = _dot(hd.p.astype(BF16), transpose_bd(hd.n).astype(BF16))
    for hd in heads:
        hd.n = hd.n + hd.p + hd.res

    for hd in heads:
        hd.n16 = transpose_bd(hd.n).T.astype(BF16)
        hd.eg = jnp.exp(hd.gcol)
        hd.rhs = jnp.concatenate([v_ref[:, hd.cols] * hd.bcol, hd.k * (hd.bcol * hd.eg)], axis=1)
    for hd in heads:
        hd.sol16 = (hd.rhs + _dot(hd.n16, hd.rhs.astype(BF16))).astype(BF16)
    for hd in heads:
        hd.qkd16 = transpose_bd(hd.qk * hd.dm).T.astype(BF16)
    for hd in heads:
        hd.x = _dot(hd.qkd16, hd.sol16)
    for hl, hd in enumerate(heads):
        hd.qp16 = (hd.q * hd.eg - hd.x[:, HEAD:]).astype(BF16)
        hd.kd16 = (hd.k * jnp.exp(hd.gtcol - hd.gcol)).astype(BF16)
        hd.state = s_ref[hl]
        hd.o = []

    for j in range(r // CHUNK):
        sl = slice(j * CHUNK, (j + 1) * CHUNK)
        for hd in heads:
            hd.kc = lax.dot_general(hd.kd16[sl], hd.sol16[sl], tn, preferred_element_type=F32)
        for hd in heads:
            lhs = jnp.concatenate([hd.kc[:, HEAD:].astype(BF16), hd.qp16[sl]], axis=0)
            hd.ks_qs = _dot(lhs, hd.state.astype(BF16))
        for hl, hd in enumerate(heads):
            hd.o.append(hd.ks_qs[HEAD:] + hd.x[sl, :HEAD])
            hd.state = (hd.state * jnp.exp(gtrow_ref[hl, 0, j:j + 1, :]) + hd.kc[:, :HEAD]
                        - hd.ks_qs[:HEAD])

    for hl, hd in enumerate(heads):
        s_ref[hl] = hd.state
        o = jnp.concatenate(hd.o, axis=0)
        o = o * lax.rsqrt(jnp.mean(o * o, axis=-1, keepdims=True) + NORM_EPS) * nw_ref[...]
        o_ref[:, hd.cols] = (o * _silu(z_ref[:, hd.cols])).astype(o_ref.dtype)


def _gdn_call(qk, v, z, gc, bt, gt, gcrow, btrow, gtrow, nw):
    t = qk.shape[0]
    r = GDN_ROWS
    p = GDN_PAIRS
    k_off = N_QK_HEADS // p
    return pl.pallas_call(
        _gdn_kernel,
        grid=(N_QK_HEADS // p, t // r),
        in_specs=[
            pl.BlockSpec((r, p * HEAD), lambda pp, tb: (tb, pp)),
            pl.BlockSpec((r, p * HEAD), lambda pp, tb: (tb, k_off + pp)),
            pl.BlockSpec((r, 2 * p * HEAD), lambda pp, tb: (tb, pp)),
            pl.BlockSpec((r, 2 * p * HEAD), lambda pp, tb: (tb, pp)),
            pl.BlockSpec((r, LANES), lambda pp, tb: (tb, 0)),
            pl.BlockSpec((r, LANES), lambda pp, tb: (tb, 0)),
            pl.BlockSpec((r, LANES), lambda pp, tb: (tb, 0)),
            pl.BlockSpec((2 * p, 1, 1, r), lambda pp, tb: (pp, tb, 0, 0)),
            pl.BlockSpec((2 * p, 1, 1, r), lambda pp, tb: (pp, tb, 0, 0)),
            pl.BlockSpec((2 * p, 1, r // CHUNK, LANES), lambda pp, tb: (pp, tb, 0, 0)),
            pl.BlockSpec((1, HEAD), lambda pp, tb: (0, 0)),
        ],
        out_specs=pl.BlockSpec((r, 2 * p * HEAD), lambda pp, tb: (tb, pp)),
        out_shape=jax.ShapeDtypeStruct((t, V_WIDTH), BF16),
        scratch_shapes=[pltpu.VMEM((2 * p, HEAD, HEAD), F32)],
        compiler_params=_params(("parallel", "arbitrary")),
        name="gdn",
    )(qk, qk, v, z, gc, bt, gt, gcrow, btrow, gtrow, nw)


def _layer_norm(r, g, b):
    mu = jnp.mean(r, axis=-1, keepdims=True)
    rc = r - mu
    var = jnp.mean(rc * rc, axis=-1, keepdims=True)
    return rc * lax.rsqrt(var + LN_EPS) * g + b


def _dwconv_kernel(c_ref, w_ref, b_ref, g_ref, beta_ref, o_ref, buf, shifted, accbuf):
    i = pl.program_id(0)
    tm = CONV_TM
    ch = buf.shape[1]
    n_shift = HALO + tm - SUBLANES

    @pl.when(i == 0)
    def _():
        buf[0:HALO, :] = jnp.zeros((HALO, ch), F32)

    buf[HALO:HALO + tm, :] = c_ref[...]
    for b in range(1, SUBLANES):
        shifted[b - 1] = buf[pl.ds(b, n_shift), :]

    def row_body(rc, carry):
        r0 = pl.multiple_of(rc * CONV_RC, CONV_RC)
        for cc in range(ch // CONV_CC):
            cs = slice(cc * CONV_CC, (cc + 1) * CONV_CC)
            acc = buf[pl.ds(HALO + r0, CONV_RC), cs] * w_ref[CONV_WIDTH - 1:CONV_WIDTH, cs] + b_ref[:, cs]
            for s in range(CONV_WIDTH - 1):
                a, b = divmod(HALO - (CONV_WIDTH - 1) + s, SUBLANES)
                if b == 0:
                    src = buf[pl.ds(a * SUBLANES + r0, CONV_RC), cs]
                else:
                    src = shifted[b - 1, pl.ds(a * SUBLANES + r0, CONV_RC), cs]
                acc = acc + src * w_ref[s:s + 1, cs]
            accbuf[pl.ds(r0, CONV_RC), cs] = acc
        return carry

    lax.fori_loop(0, tm // CONV_RC, row_body, 0)
    buf[0:HALO, :] = buf[tm:tm + HALO, :]
    o_ref[...] = _silu(_layer_norm(accbuf[...], g_ref[...], beta_ref[...])).astype(o_ref.dtype)


def _dwconv_call(c, w, b, g, beta):
    t, ch = c.shape
    tm = CONV_TM
    vec = pl.BlockSpec((1, ch), lambda i: (0, 0))
    return pl.pallas_call(
        _dwconv_kernel,
        grid=(t // tm,),
        in_specs=[pl.BlockSpec((tm, ch), lambda i: (i, 0)),
                  pl.BlockSpec((CONV_WIDTH, ch), lambda i: (0, 0)), vec, vec, vec],
        out_specs=pl.BlockSpec((tm, ch), lambda i: (i, 0)),
        out_shape=jax.ShapeDtypeStruct((t, ch), BF16),
        scratch_shapes=[pltpu.VMEM((HALO + tm, ch), F32),
                        pltpu.VMEM((SUBLANES - 1, HALO + tm - SUBLANES, ch), F32),
                        pltpu.VMEM((tm, ch), F32)],
        compiler_params=_params(("arbitrary",)),
        name="dwconv_ln",
    )(c, w, b, g, beta)


def _outproj_kernel(m_ref, w_ref, x_ref, g_ref, b_ref, h_ref):
    y = _dot(m_ref[...], w_ref[...])
    h_ref[...] = _layer_norm(DN_ALPHA * x_ref[...] + y, g_ref[...], b_ref[...])


def _outproj_call(mixed, w16, x, g, b):
    t, d = x.shape
    tm = 256
    vec = pl.BlockSpec((1, d), lambda i: (0, 0))
    return pl.pallas_call(
        _outproj_kernel,
        grid=(t // tm,),
        in_specs=[pl.BlockSpec((tm, d), lambda i: (i, 0)), pl.BlockSpec((d, d), lambda i: (0, 0)),
                  pl.BlockSpec((tm, d), lambda i: (i, 0)), vec, vec],
        out_specs=pl.BlockSpec((tm, d), lambda i: (i, 0)),
        out_shape=jax.ShapeDtypeStruct((t, d), F32),
        compiler_params=_params(("parallel",)),
        name="outproj_ln1",
    )(mixed, w16, x, g, b)


def _route_kernel(h_ref, w_ref, bias_ref, idx_ref, wts_ref, rank_ref, cnt_ref, carry):
    @pl.when(pl.program_id(0) == 0)
    def _():
        carry[...] = jnp.zeros(carry.shape, F32)

    logits = jnp.dot(h_ref[...], w_ref[...], precision=lax.Precision.HIGHEST,
                     preferred_element_type=F32)
    scores = _sigmoid(logits)
    biased = scores + bias_ref[...]
    shape = biased.shape
    tm = shape[0]
    lane_i = lax.broadcasted_iota(I32, shape, 1)
    lane = lane_i.astype(F32)
    per_group = N_EXPERTS // N_GROUPS
    grp = (lane_i // per_group).astype(F32)
    neg = -jnp.inf
    big = 1e9

    def rmax(x):
        return jnp.max(x, axis=1, keepdims=True)

    def rmin(x):
        return jnp.min(x, axis=1, keepdims=True)

    def rsum(x):
        return jnp.sum(x, axis=1, keepdims=True)

    gs = jnp.zeros(shape, F32)
    for g in range(N_GROUPS):
        in_g = grp == float(g)
        m = jnp.where(in_g, biased, neg)
        m1 = rmax(m)
        i1 = rmin(jnp.where(m == m1, lane, big))
        m2 = rmax(jnp.where(lane == i1, neg, m))
        gs = jnp.where(in_g, m1 + m2, gs)
    sel_g = jnp.zeros(shape, jnp.bool_)
    cur = gs
    for _ in range(TOPK_GROUPS):
        mx = rmax(cur)
        gi = rmin(jnp.where(cur == mx, grp, big))
        hit = grp == gi
        sel_g = sel_g | hit
        cur = jnp.where(hit, neg, cur)
    masked = jnp.where(sel_g, biased, neg)
    ids, ws = [], []
    sel = jnp.zeros(shape, F32)
    for _ in range(TOP_K):
        mx = rmax(masked)
        ik = rmin(jnp.where(masked == mx, lane, big))
        hit = lane == ik
        ws.append(rsum(jnp.where(hit, scores, 0.0)))
        ids.append(ik)
        sel = jnp.where(hit, 1.0, sel)
        masked = jnp.where(hit, neg, masked)
    w = jnp.concatenate(ws, axis=1)
    wts_ref[...] = w / rsum(w) * ROUTE_SCALE
    idx_ref[...] = jnp.concatenate(ids, axis=1).astype(I32)

    earlier = (lax.broadcasted_iota(I32, (tm, tm), 1) < lax.broadcasted_iota(I32, (tm, tm), 0))
    before = _dot(earlier.astype(BF16), sel.astype(BF16)) + carry[...]
    ranks = [rsum(jnp.where(lane == ik, before, 0.0)) for ik in ids]
    rank_ref[...] = jnp.concatenate(ranks, axis=1).astype(I32)
    carry[...] = carry[...] + jnp.sum(sel, axis=0, keepdims=True)
    cnt_ref[...] = carry[...]


def _route_call(h, w_router, bias):
    t, d = h.shape
    tm = ROUTE_TM
    tok = pl.BlockSpec((tm, TOP_K), lambda i: (i, 0))
    return pl.pallas_call(
        _route_kernel,
        grid=(t // tm,),
        in_specs=[pl.BlockSpec((tm, d), lambda i: (i, 0)),
                  pl.BlockSpec((d, N_EXPERTS), lambda i: (0, 0)),
                  pl.BlockSpec((1, N_EXPERTS), lambda i: (0, 0))],
        out_specs=(tok, tok, tok, pl.BlockSpec((1, N_EXPERTS), lambda i: (0, 0))),
        out_shape=(jax.ShapeDtypeStruct((t, TOP_K), I32), jax.ShapeDtypeStruct((t, TOP_K), F32),
                   jax.ShapeDtypeStruct((t, TOP_K), I32), jax.ShapeDtypeStruct((1, N_EXPERTS), F32)),
        scratch_shapes=[pltpu.VMEM((1, N_EXPERTS), F32)],
        compiler_params=_params(("arbitrary",)),
        name="route",
    )(h, w_router, bias)


def _dest_kernel(idx_ref, rank_ref, start_ref, dest_ref):
    idx = idx_ref[...]
    shape = (idx.shape[0], N_EXPERTS)
    lane = lax.broadcasted_iota(I32, shape, 1)
    start = start_ref[...]
    cols = []
    for kk in range(TOP_K):
        hit = lane == idx[:, kk:kk + 1]
        cols.append(jnp.sum(jnp.where(hit, start, 0.0), axis=1, keepdims=True))
    dest_ref[...] = jnp.concatenate(cols, axis=1).astype(I32) + rank_ref[...]


def _dest_call(idx, rank, start_f32):
    t = idx.shape[0]
    tm = 512
    tok = pl.BlockSpec((tm, TOP_K), lambda i: (i, 0))
    return pl.pallas_call(
        _dest_kernel,
        grid=(t // tm,),
        in_specs=[tok, tok, pl.BlockSpec((1, N_EXPERTS), lambda i: (0, 0))],
        out_specs=tok,
        out_shape=jax.ShapeDtypeStruct((t, TOP_K), I32),
        compiler_params=_params(("parallel",)),
        name="dest",
    )(idx, rank, start_f32)


def _row_copy(src, src_row, dst, dst_row, sem):
    return pltpu.make_async_copy(src.at[pl.ds(src_row, 1), :], dst.at[pl.ds(dst_row, 1), :], sem)


def _pack_halves(x):
    half = x.shape[1] // 2
    lo = lax.bitcast_convert_type(x[:, :half].astype(BF16).astype(F32), U32)
    hi = lax.bitcast_convert_type(x[:, half:].astype(BF16).astype(F32), U32)
    return hi | (lo >> 16)


def _unpack_halves(w):
    lo = lax.bitcast_convert_type(w << 16, F32)
    hi = lax.bitcast_convert_type(w & jnp.uint32(0xFFFF0000), F32)
    return lo, hi


def _rows_wait(src, dst, dst_row, n_rows, sem):
    pltpu.make_async_copy(src.at[pl.ds(0, n_rows), :], dst.at[pl.ds(dst_row, n_rows), :], sem).wait()


def _scatter_kernel(dest_ref, h_ref, xs_hbm, packed, sem):
    packed[...] = _pack_halves(h_ref[...])

    def issue(tok, c):
        for kk in range(TOP_K):
            _row_copy(packed, tok, xs_hbm, dest_ref[0, 0, tok * TOP_K + kk], sem.at[tok]).start(priority=kk % 2)
        return c

    lax.fori_loop(0, SCATTER_TM, issue, 0, unroll=2)

    def wait(tok, c):
        _rows_wait(packed, xs_hbm, 0, TOP_K, sem.at[tok])
        return c

    lax.fori_loop(0, SCATTER_TM, wait, 0, unroll=8)


def _scatter_call(dest, h):
    t, d = h.shape
    tm = SCATTER_TM
    n = tm * TOP_K
    return pl.pallas_call(
        _scatter_kernel,
        grid=(t // tm,),
        in_specs=[pl.BlockSpec((1, 1, n), lambda i: (i, 0, 0), memory_space=pltpu.SMEM),
                  pl.BlockSpec((tm, d), lambda i: (i, 0))],
        out_specs=pl.BlockSpec(memory_space=pl.ANY),
        out_shape=jax.ShapeDtypeStruct((t * TOP_K, d // 2), U32),
        scratch_shapes=[pltpu.VMEM((tm, d // 2), U32), pltpu.SemaphoreType.DMA((tm,))],
        compiler_params=_params(("arbitrary",)),
        name="dispatch",
    )(dest.reshape(t // tm, 1, n), h)


def _weight_copies(w_hbm, w32, wsem, e, slot):
    return [pltpu.make_async_copy(w_hbm[m].at[e], w32[m].at[slot], wsem.at[slot, m])
            for m in range(3)]


def _expert_kernel(vb_ref, ve_ref, lo_ref, hi_ref, fb_ref, fe_ref, par_ref, nxt_ref, nv_ref,
                   x_ref, wg_hbm, wu_hbm, wd_hbm, y_ref,
                   wg32, wu32, wd32, wgu16, wd16, wsem):
    v = pl.program_id(0)
    w_hbm = (wg_hbm, wu_hbm, wd_hbm)
    w32 = (wg32, wu32, wd32)
    w16 = (wgu16.at[:, 0:D_EXPERT], wgu16.at[:, D_EXPERT:2 * D_EXPERT], wd16)

    @pl.when(v < nv_ref[0])
    def _():
        slot = par_ref[v]

        @pl.when(fe_ref[v] == 1)
        def _():
            @pl.when(v == 0)
            def _():
                for cp in _weight_copies(w_hbm, w32, wsem, ve_ref[v], slot):
                    cp.start(priority=WEIGHT_DMA_QUEUE)

            @pl.when(nxt_ref[v] >= 0)
            def _():
                for cp in _weight_copies(w_hbm, w32, wsem, nxt_ref[v], 1 - slot):
                    cp.start(priority=WEIGHT_DMA_QUEUE)

            for m, cp in enumerate(_weight_copies(w_hbm, w32, wsem, ve_ref[v], slot)):
                cp.wait()
                w16[m][...] = w32[m][slot].astype(BF16)

        x_lo, x_hi = _unpack_halves(x_ref[...])
        x = jnp.concatenate([x_lo.astype(BF16), x_hi.astype(BF16)], axis=1)
        gu = _dot(x, wgu16[...])
        act = _silu(gu[:, :D_EXPERT]) * gu[:, D_EXPERT:]
        y = _pack_halves(_dot(act.astype(BF16), wd16[...]))
        rows = lax.broadcasted_iota(I32, y.shape, 0)
        mine = (rows >= lo_ref[v]) & (rows < hi_ref[v])

        @pl.when(fb_ref[v] == 1)
        def _():
            y_ref[...] = jnp.where(mine, y, jnp.uint32(0))

        @pl.when(fb_ref[v] == 0)
        def _():
            y_ref[...] = jnp.where(mine, y, y_ref[...])


def _expert_call(tables, xs, w_gate, w_up, w_down):
    n_rows, dp = xs.shape
    d = 2 * dp
    blk = EXPERT_BLOCK
    n_visits = tables[0].shape[0]
    any_spec = pl.BlockSpec(memory_space=pl.ANY)
    grid_spec = pltpu.PrefetchScalarGridSpec(
        num_scalar_prefetch=len(tables),
        grid=(n_visits,),
        in_specs=[pl.BlockSpec((blk, dp), lambda v, vb, *_: (vb[v], 0)), any_spec, any_spec, any_spec],
        out_specs=pl.BlockSpec((blk, dp), lambda v, vb, *_: (vb[v], 0)),
        scratch_shapes=[
            pltpu.VMEM((2, d, D_EXPERT), F32), pltpu.VMEM((2, d, D_EXPERT), F32),
            pltpu.VMEM((2, D_EXPERT, d), F32),
            pltpu.VMEM((d, 2 * D_EXPERT), BF16), pltpu.VMEM((D_EXPERT, d), BF16),
            pltpu.SemaphoreType.DMA((2, 3)),
        ],
    )
    return pl.pallas_call(
        _expert_kernel,
        grid_spec=grid_spec,
        out_shape=jax.ShapeDtypeStruct((n_rows, dp), U32),
        compiler_params=_params(("arbitrary",)),
        name="experts",
    )(*tables, xs, w_gate, w_up, w_down)


def _visit_tables(counts, n_rows):
    blk = EXPERT_BLOCK
    n_visits = n_rows // blk + N_EXPERTS - 1
    end = jnp.cumsum(counts)
    start = end - counts
    nonempty = counts > 0
    first_blk = start // blk
    last_blk = jnp.maximum(end - 1, 0) // blk
    nvis = jnp.where(nonempty, last_blk - first_blk + 1, 0)
    vis_end = jnp.cumsum(nvis)
    vis_start = vis_end - nvis
    n_used = vis_end[-1]
    ids = jnp.arange(N_EXPERTS, dtype=I32)
    ordinal = jnp.cumsum(nonempty.astype(I32)) - 1
    nxt_incl = lax.cummin(jnp.where(nonempty, ids, N_EXPERTS), axis=0, reverse=True)
    nxt_e = jnp.concatenate([nxt_incl[1:], jnp.full((1,), N_EXPERTS, I32)])
    nxt_e = jnp.where(nxt_e >= N_EXPERTS, -1, nxt_e)
    v = jnp.arange(n_visits, dtype=I32)
    vc = jnp.clip(v, 0, jnp.maximum(n_used - 1, 0))
    ve = jnp.minimum(jnp.sum((vis_end[None, :] <= vc[:, None]).astype(I32), axis=1), N_EXPERTS - 1)
    onehot = ve[:, None] == ids[None, :]
    look = lambda tab: jnp.sum(jnp.where(onehot, tab.astype(I32)[None, :], 0), axis=1)
    v_first = look(vis_start)
    vb = look(first_blk) + (vc - v_first)
    lo = jnp.clip(look(start) - vb * blk, 0, blk)
    hi = jnp.clip(look(end) - vb * blk, 0, blk)
    fb = jnp.concatenate([jnp.ones((1,), I32), (vb[1:] != vb[:-1]).astype(I32)])
    fe = (vc == v_first).astype(I32)
    tables = (vb, ve, lo, hi, fb, fe, look(ordinal % 2), look(nxt_e), n_used.astype(I32).reshape(1))
    return tables, start


def _shared_kernel(h_ref, wg_ref, wu_ref, wd_ref, y_ref, wg16, wu16, wd16):
    @pl.when(pl.program_id(0) == 0)
    def _():
        wg16[...] = wg_ref[...].astype(BF16)
        wu16[...] = wu_ref[...].astype(BF16)
        wd16[...] = wd_ref[...].astype(BF16)

    x = h_ref[...].astype(BF16)
    g = _dot(x, wg16[...])
    u = _dot(x, wu16[...])
    y_ref[...] = _dot((_silu(g) * u).astype(BF16), wd16[...])


def _shared_call(h, wg, wu, wd):
    t, d = h.shape
    ds = wg.shape[1]
    tm = 512
    return pl.pallas_call(
        _shared_kernel,
        grid=(t // tm,),
        in_specs=[pl.BlockSpec((tm, d), lambda i: (i, 0)),
                  pl.BlockSpec((d, ds), lambda i: (0, 0)),
                  pl.BlockSpec((d, ds), lambda i: (0, 0)),
                  pl.BlockSpec((ds, d), lambda i: (0, 0))],
        out_specs=pl.BlockSpec((tm, d), lambda i: (i, 0)),
        out_shape=jax.ShapeDtypeStruct((t, d), F32),
        scratch_shapes=[pltpu.VMEM((d, ds), BF16), pltpu.VMEM((d, ds), BF16),
                        pltpu.VMEM((ds, d), BF16)],
        compiler_params=_params(("arbitrary",)),
        name="shared_expert",
    )(h, wg, wu, wd)


def _combine_kernel(dest_ref, dnext_ref, ys_hbm, wts_ref, h_ref, ysh_ref, g_ref, b_ref, o_ref,
                    buf, sem):
    tm = COMBINE_TM
    i = pl.program_id(0)
    slot = i % 2

    def gather(d_ref, s):
        def issue(r, c):
            for kk in range(TOP_K):
                a = kk * tm + r
                _row_copy(ys_hbm, d_ref[0, 0, a], buf.at[s], a, sem.at[s, kk]).start(priority=kk % 2)
            return c

        lax.fori_loop(0, tm, issue, 0, unroll=2)

    @pl.when(i == 0)
    def _():
        gather(dest_ref, slot)

    @pl.when(i + 1 < pl.num_programs(0))
    def _():
        gather(dnext_ref, 1 - slot)

    acc = DN_ALPHA * h_ref[...] + ysh_ref[...]
    wts = wts_ref[...]
    for kk in range(TOP_K):
        _rows_wait(ys_hbm, buf.at[slot], kk * tm, tm, sem.at[slot, kk])
    half = acc.shape[1] // 2
    acc_lo, acc_hi = acc[:, :half], acc[:, half:]
    for kk in range(TOP_K):
        y_lo, y_hi = _unpack_halves(buf[slot, kk * tm:(kk + 1) * tm, :])
        acc_lo = acc_lo + wts[:, kk:kk + 1] * y_lo
        acc_hi = acc_hi + wts[:, kk:kk + 1] * y_hi
    o_ref[...] = _layer_norm(jnp.concatenate([acc_lo, acc_hi], axis=1), g_ref[...], b_ref[...])


def _combine_call(dest_km, ys, wts, h, ysh, g, b):
    t, d = h.shape
    tm = COMBINE_TM
    n = tm * TOP_K
    n_tiles = t // tm
    vec = pl.BlockSpec((1, d), lambda i: (0, 0))
    return pl.pallas_call(
        _combine_kernel,
        grid=(n_tiles,),
        in_specs=[
            pl.BlockSpec((1, 1, n), lambda i: (i, 0, 0), memory_space=pltpu.SMEM),
            pl.BlockSpec((1, 1, n), lambda i: (jnp.minimum(i + 1, n_tiles - 1), 0, 0),
                         memory_space=pltpu.SMEM),
            pl.BlockSpec(memory_space=pl.ANY),
            pl.BlockSpec((tm, TOP_K), lambda i: (i, 0)),
            pl.BlockSpec((tm, d), lambda i: (i, 0)),
            pl.BlockSpec((tm, d), lambda i: (i, 0)),
            vec, vec,
        ],
        out_specs=pl.BlockSpec((tm, d), lambda i: (i, 0)),
        out_shape=jax.ShapeDtypeStruct((t, d), F32),
        scratch_shapes=[pltpu.VMEM((2, n, d // 2), U32), pltpu.SemaphoreType.DMA((2, TOP_K))],
        compiler_params=_params(("arbitrary",)),
        name="combine_ln2",
    )(dest_km, dest_km, ys, wts, h, ysh, g, b)


def _token_mixing(x, w_in, w_short_conv, a_log, dt_bias, delta_norm_w, w_o_delta, conv_dw_w,
                  conv_dw_b, conv_ln_g, conv_ln_b, w_pw2, b_pw2, w_out, ln1_g, ln1_b):
    t, d = x.shape
    x16 = x.astype(BF16)
    w_t = w_in.T
    row = lambda a: a.reshape(1, -1)
    tile = pl.BlockSpec((TM, TN), lambda j, i: (i, j))
    wide = pl.BlockSpec((TM_WIDE, TN), lambda j, i: (i, j))
    conv_buf = pltpu.VMEM((TM_WIDE + SUBLANES, TN), F32)

    scale = jnp.concatenate([jnp.full((1, QK_WIDTH), HEAD ** -0.5, F32), jnp.ones((1, QK_WIDTH), F32)], axis=1)
    qk = _proj_call(
        "proj_qk", x16, [w_t], [0], 2 * QK_WIDTH // TN, _qk_epilogue,
        [w_short_conv, scale],
        [pl.BlockSpec((SHORT_CONV, TN), lambda j, i: (0, j)), pl.BlockSpec((1, TN), lambda j, i: (0, j))],
        jax.ShapeDtypeStruct((t, 2 * QK_WIDTH), F32), wide, scratch=[conv_buf], tm=TM_WIDE, transposed=True)
    v_off = 2 * QK_WIDTH // TN
    v = _proj_call(
        "proj_v", x16, [w_t], [2 * QK_WIDTH], V_WIDTH // TN, _v_epilogue,
        [w_short_conv], [pl.BlockSpec((SHORT_CONV, TN), lambda j, i: (0, j + v_off))],
        jax.ShapeDtypeStruct((t, V_WIDTH), F32), wide, scratch=[conv_buf], tm=TM_WIDE, transposed=True)
    z = _proj_call(
        "proj_z", x16, [w_t], [COL_Z], V_WIDTH // TN, _z_epilogue, [], [],
        jax.ShapeDtypeStruct((t, V_WIDTH), F32), wide, tm=TM_WIDE, transposed=True)

    pad_h = lambda a: jnp.pad(a.reshape(1, -1), ((0, 0), (0, LANES - N_V_HEADS)))
    lane_vec = pl.BlockSpec((1, LANES), lambda j, i: (0, 0))
    lane_tile = pl.BlockSpec((TM, LANES), lambda j, i: (i, 0))
    lane_shape = jax.ShapeDtypeStruct((t, LANES), F32)
    gc, bt, gt = _proj_call(
        "proj_ab", x16, [w_t], [COL_AB], 1, _ab_epilogue, [pad_h(a_log), pad_h(dt_bias)],
        [lane_vec, lane_vec], (lane_shape, lane_shape, lane_shape), (lane_tile, lane_tile, lane_tile),
        tn=LANES, transposed=True)

    c = _proj_call(
        "proj_glu", x16, [w_t, w_t], [COL_GLU, COL_GLU + d], d // TN, _glu_epilogue, [], [],
        jax.ShapeDtypeStruct((t, d), F32), wide, tm=TM_WIDE, transposed=True)
    gates = _proj_call(
        "proj_gates", x16, [w_t], [COL_GATES], 2 * d // TN, _gates_epilogue, [], [],
        jax.ShapeDtypeStruct((t, 2 * d), F32), wide, tm=TM_WIDE, transposed=True)

    r = GDN_ROWS
    gcrow = gc[:, :N_V_HEADS].T.reshape(N_V_HEADS, t // r, 1, r)
    btrow = bt[:, N_V_HEADS:2 * N_V_HEADS].T.reshape(N_V_HEADS, t // r, 1, r)
    gt_chunk = gt[CHUNK - 1::CHUNK, :N_V_HEADS].T
    gtrow = jnp.broadcast_to(gt_chunk[:, :, None], (N_V_HEADS, t // CHUNK, LANES))
    gtrow = gtrow.reshape(N_V_HEADS, t // r, r // CHUNK, LANES)
    og = _gdn_call(qk, v, z, gc, bt, gt, gcrow, btrow, gtrow, row(delta_norm_w))

    ya = _proj_call(
        "proj_odelta", og, [w_o_delta], [0], d // TN, _odelta_epilogue, [gates], [tile],
        jax.ShapeDtypeStruct((t, d), F32), tile)

    c_act = _dwconv_call(c, conv_dw_w, row(conv_dw_b), row(conv_ln_g), row(conv_ln_b))
    g_off = d // TN
    mixed = _proj_call(
        "proj_pw2", c_act, [w_pw2], [0], d // TN, _pw2_epilogue, [row(b_pw2), gates, ya],
        [pl.BlockSpec((1, TN), lambda j, i: (0, j)),
         pl.BlockSpec((TM_WIDE, TN), lambda j, i: (i, j + g_off)), wide],
        jax.ShapeDtypeStruct((t, d), BF16), wide, tm=TM_WIDE)
    return _outproj_call(mixed, w_out.astype(BF16), x, row(ln1_g), row(ln1_b))


def _moe(h, w_router, router_bias, w_gate, w_up, w_down, w_sh_gate, w_sh_up, w_sh_down, ln2_g, ln2_b):
    t, d = h.shape
    idx, wts, rank, counts = _route_call(h, w_router, router_bias.reshape(1, -1))
    tables, start = _visit_tables(counts.reshape(-1).astype(I32), t * TOP_K)
    dest = _dest_call(idx, rank, start.astype(F32).reshape(1, -1))
    xs = _scatter_call(dest, h)
    ys = _expert_call(tables, xs, w_gate, w_up, w_down)
    ysh = _shared_call(h, w_sh_gate, w_sh_up, w_sh_down)
    tm = COMBINE_TM
    dest_km = dest.reshape(t // tm, tm, TOP_K).transpose(0, 2, 1).reshape(t // tm, 1, tm * TOP_K)
    return _combine_call(dest_km, ys, wts, h, ysh, ln2_g.reshape(1, -1), ln2_b.reshape(1, -1))


def kernel(x, w_in, w_short_conv, a_log, dt_bias, delta_norm_w, w_o_delta, conv_dw_w, conv_dw_b,
           conv_ln_g, conv_ln_b, w_pw2, b_pw2, w_out, ln1_g, ln1_b, w_router, router_bias, w_gate,
           w_up, w_down, w_sh_gate, w_sh_up, w_sh_down, ln2_g, ln2_b):
    batch, seq, d = x.shape
    depth = w_in.shape[0]
    outs = []
    for bi in range(batch):
        h = x[bi]
        for li in range(depth):
            h = _token_mixing(h, w_in[li], w_short_conv[li], a_log[li], dt_bias[li], delta_norm_w[li],
                              w_o_delta[li], conv_dw_w[li], conv_dw_b[li], conv_ln_g[li], conv_ln_b[li],
                              w_pw2[li], b_pw2[li], w_out[li], ln1_g[li], ln1_b[li])
            h = _moe(h, w_router[li], router_bias[li], w_gate[li], w_up[li], w_down[li],
                     w_sh_gate[li], w_sh_up[li], w_sh_down[li], ln2_g[li], ln2_b[li])
        outs.append(h)
    return jnp.stack(outs, axis=0)
```

```python
import jax
import jax.numpy as jnp
from jax import lax
from jax.experimental import pallas as pl
from jax.experimental.pallas import tpu as pltpu

F32 = jnp.float32
BF16 = jnp.bfloat16
I32 = jnp.int32
U32 = jnp.uint32

D_MODEL = 2048
CHUNK = 64
N_QK_HEADS = 16
N_V_HEADS = 32
HEAD = 128
QK_WIDTH = N_QK_HEADS * HEAD
V_WIDTH = N_V_HEADS * HEAD
SHORT_CONV = 4
CONV_WIDTH = 31
N_EXPERTS = 256
TOP_K = 8
N_GROUPS = 8
TOPK_GROUPS = 4
D_EXPERT = 512
ROUTE_SCALE = 2.5
DN_ALPHA = 2.0 ** 0.25
LN_EPS = 1e-5
NORM_EPS = 1e-6
COL_Z = 2 * QK_WIDTH + V_WIDTH
COL_AB = COL_Z + V_WIDTH
COL_GLU = COL_AB + 2 * N_V_HEADS
COL_GATES = COL_GLU + 2 * D_MODEL

LANES = 128
SUBLANES = 8
VMEM_LIMIT = 56 * 1024 * 1024
TM = 512
TM_WIDE = 1024
TN = 512
GDN_ROWS = 256
GDN_PAIRS = 4
CONV_TM = 256
HALO = 32
CONV_RC = 64
CONV_CC = 256
ROW_ALIGN = 64
ROUTE_TM = 256
EXPERT_BLOCK = 128
SCATTER_TM = 128
COMBINE_TM = 64
WEIGHT_DMA_QUEUE = 1
WEIGHT_SLOTS = 3


def _params(sem, vmem=VMEM_LIMIT):
    return pltpu.CompilerParams(dimension_semantics=sem, vmem_limit_bytes=vmem)


def _sigmoid(x):
    return jax.nn.sigmoid(x)


def _silu(x):
    return x * jax.nn.sigmoid(x)


def _softplus(x):
    return jnp.maximum(x, 0.0) + jnp.log1p(jnp.exp(-jnp.abs(x)))


def _dot(a, b):
    return jnp.dot(a, b, preferred_element_type=F32)


def _proj_call(name, x, ws, w_offs, n_tiles, epilogue, extras, extra_specs, out_shapes, out_specs,
               scratch=(), tm=TM, tn=TN, transposed=False):
    m, k = x.shape
    single_out = not isinstance(out_shapes, (tuple, list))
    if single_out:
        out_shapes, out_specs = (out_shapes,), (out_specs,)
    nw, ne, no = len(ws), len(extras), len(out_shapes)
    needs_cast = [w.dtype != BF16 for w in ws]
    contract = (((1,), (1,)), ((), ())) if transposed else (((1,), (0,)), ((), ()))

    def body(*refs):
        x_ref = refs[0]
        w_refs = refs[1:1 + nw]
        ex_refs = refs[1 + nw:1 + nw + ne]
        out_refs = refs[1 + nw + ne:1 + nw + ne + no]
        scr = refs[1 + nw + ne + no:]
        wb_refs = scr[:sum(needs_cast)]
        rest = scr[sum(needs_cast):]
        i = pl.program_id(1)
        wsrc, c = [], 0
        for kk in range(nw):
            if needs_cast[kk]:
                wsrc.append(wb_refs[c])
                c += 1
            else:
                wsrc.append(w_refs[kk])

        @pl.when(i == 0)
        def _():
            cc = 0
            for kk in range(nw):
                if needs_cast[kk]:
                    wb_refs[cc][...] = w_refs[kk][...].astype(BF16)
                    cc += 1

        xv = x_ref[...]
        accs = [lax.dot_general(xv, wr[...], contract, preferred_element_type=F32) for wr in wsrc]
        epilogue(i, accs, ex_refs, out_refs, rest)

    in_specs = [pl.BlockSpec((tm, k), lambda j, i: (i, 0))]
    for off in w_offs:
        if transposed:
            in_specs.append(pl.BlockSpec(
                (pl.Element(tn), pl.Element(k)),
                lambda j, i, off=off: (pl.multiple_of(off + tn * j, ROW_ALIGN), 0)))
        else:
            in_specs.append(pl.BlockSpec((k, tn), lambda j, i, off=off: (0, j + off)))
    in_specs += list(extra_specs)
    w_tile = (tn, k) if transposed else (k, tn)
    scratch_shapes = [pltpu.VMEM(w_tile, BF16) for c in needs_cast if c] + list(scratch)
    res = pl.pallas_call(
        body,
        grid=(n_tiles, m // tm),
        in_specs=in_specs,
        out_specs=tuple(out_specs),
        out_shape=tuple(out_shapes),
        scratch_shapes=scratch_shapes,
        compiler_params=_params(("parallel", "arbitrary")),
        name=name,
    )(x, *ws, *extras)
    return res[0] if single_out else res


def _short_conv_silu(i, acc, cw_ref, buf, tm):
    @pl.when(i == 0)
    def _():
        buf[0:SUBLANES, :] = jnp.zeros((SUBLANES, buf.shape[1]), F32)

    buf[SUBLANES:SUBLANES + tm, :] = acc
    cw = cw_ref[...]
    y = acc * cw[SHORT_CONV - 1:SHORT_CONV, :]
    for s in range(SHORT_CONV - 1):
        y = y + buf[pl.ds(SUBLANES - (SHORT_CONV - 1) + s, tm), :] * cw[s:s + 1, :]
    buf[0:SUBLANES, :] = buf[tm:tm + SUBLANES, :]
    return _silu(y)


def _qk_epilogue(i, accs, ex, outs, scr):
    cw_ref, scale_ref = ex
    (buf,) = scr
    y = _short_conv_silu(i, accs[0], cw_ref, buf, accs[0].shape[0])
    parts = []
    for g in range(TN // HEAD):
        yg = y[:, g * HEAD:(g + 1) * HEAD]
        parts.append(yg * lax.rsqrt(jnp.sum(yg * yg, axis=-1, keepdims=True) + NORM_EPS))
    outs[0][...] = jnp.concatenate(parts, axis=1) * scale_ref[...]


def _v_epilogue(i, accs, ex, outs, scr):
    (cw_ref,) = ex
    (buf,) = scr
    outs[0][...] = _short_conv_silu(i, accs[0], cw_ref, buf, accs[0].shape[0])


def _z_epilogue(i, accs, ex, outs, scr):
    outs[0][...] = accs[0]


def _glu_epilogue(i, accs, ex, outs, scr):
    outs[0][...] = accs[0] * _sigmoid(accs[1])


def _gates_epilogue(i, accs, ex, outs, scr):
    outs[0][...] = _sigmoid(accs[0])


def _ab_epilogue(i, accs, ex, outs, scr):
    alog_ref, dtb_ref = ex
    gc_ref, bt_ref, gt_ref = outs
    acc = accs[0]
    g = -jnp.exp(alog_ref[...]) * _softplus(acc + dtb_ref[...])
    row = lax.broadcasted_iota(I32, g.shape, 0) % CHUNK
    s = 1
    while s < CHUNK:
        g = g + jnp.where(row >= s, pltpu.roll(g, s, axis=0), 0.0)
        s *= 2
    n_chunks = g.shape[0] // CHUNK
    tot = g.reshape(n_chunks, CHUNK, LANES)[:, CHUNK - 1:CHUNK, :]
    gc_ref[...] = g
    bt_ref[...] = _sigmoid(acc)
    gt_ref[...] = jnp.broadcast_to(tot, (n_chunks, CHUNK, LANES)).reshape(g.shape)


def _odelta_epilogue(i, accs, ex, outs, scr):
    (gate_ref,) = ex
    outs[0][...] = accs[0] * gate_ref[...]


def _pw2_epilogue(i, accs, ex, outs, scr):
    bias_ref, gate_ref, ya_ref = ex
    outs[0][...] = (ya_ref[...] + gate_ref[...] * (accs[0] + bias_ref[...])).astype(BF16)


class _Head:
    pass


def _gdn_kernel(q_ref, k_ref, v_ref, z_ref, gc_ref, bt_ref, gt_ref, gcrow_ref, btrow_ref, gtrow_ref,
                nw_ref, o_ref, s_ref):
    pp = pl.program_id(0)
    tb = pl.program_id(1)
    r = GDN_ROWS
    half = r // 2
    n_heads = 2 * GDN_PAIRS

    @pl.when(tb == 0)
    def _():
        s_ref[...] = jnp.zeros(s_ref.shape, F32)

    col_tok = lax.broadcasted_iota(I32, (half, r), 0)
    row_tok = lax.broadcasted_iota(I32, (half, r), 1)
    first_half = row_tok < half
    row_in_half = row_tok % half
    same = (row_in_half // CHUNK) == (col_tok // CHUNK)
    causal = same & (col_tok <= row_in_half)
    strict = same & (col_tok < row_in_half)
    lane = lax.broadcasted_iota(I32, (r, LANES), 1)
    gc_all = gc_ref[...]
    bt_all = bt_ref[...]
    gt_all = gt_ref[...]
    nt = (((1,), (1,)), ((), ()))
    tn = (((0,), (0,)), ((), ()))

    def transpose_bd(xc):
        return jnp.concatenate([jnp.where(first_half, xc, 0.0), jnp.where(first_half, 0.0, xc)], axis=0)

    heads = [_Head() for _ in range(n_heads)]
    for pi in range(GDN_PAIRS):
        q = q_ref[:, pi * HEAD:(pi + 1) * HEAD]
        k = k_ref[:, pi * HEAD:(pi + 1) * HEAD]
        k16 = k.astype(BF16)
        q16 = q.astype(BF16)
        kq = [lax.dot_general(k16[rs], jnp.concatenate([k16[rs], q16[rs]], axis=0), nt,
                              preferred_element_type=F32) for rs in (slice(0, half), slice(half, r))]
        kk = jnp.concatenate([kq[0][:, :half], kq[1][:, :half]], axis=1)
        qk = jnp.concatenate([kq[0][:, half:], kq[1][:, half:]], axis=1)
        for hd in heads[2 * pi:2 * pi + 2]:
            hd.q, hd.k, hd.kk, hd.qk = q, k, kk, qk

    for hl, hd in enumerate(heads):
        h = n_heads * pp + hl
        hd.cols = slice(hl * HEAD, (hl + 1) * HEAD)
        hd.gcol = jnp.sum(jnp.where(lane == h, gc_all, 0.0), axis=1, keepdims=True)
        hd.bcol = jnp.sum(jnp.where(lane == h + N_V_HEADS, bt_all, 0.0), axis=1, keepdims=True)
        hd.gtcol = jnp.sum(jnp.where(lane == h, gt_all, 0.0), axis=1, keepdims=True)
        gc_col_tok = jnp.concatenate([jnp.broadcast_to(hd.gcol[:half], (half, half)),
                                      jnp.broadcast_to(hd.gcol[half:], (half, half))], axis=1)
        hd.dm = jnp.exp(jnp.where(causal, gcrow_ref[hl, 0] - gc_col_tok, -jnp.inf))
        hd.p = jnp.where(strict, btrow_ref[hl, 0] * hd.kk * hd.dm, 0.0)
        hd.n = -hd.p

    for hd in heads:
        hd.res = _dot(hd.p.astype(BF16), transpose_bd(hd.p).astype(BF16))
    for hd in heads:
        hd.p = hd.res
    for _ in range(4):
        for hd in heads:
            rhs = jnp.concatenate([transpose_bd(hd.p), transpose_bd(hd.n)], axis=1).astype(BF16)
            hd.res = _dot(hd.p.astype(BF16), rhs)
        for hd in heads:
            hd.n = hd.n + hd.p + hd.res[:, r:]
            hd.p = hd.res[:, :r]
    for hd in heads:
        hd.res = _dot(hd.p.astype(BF16), transpose_bd(hd.n).astype(BF16))
    for hd in heads:
        hd.n = hd.n + hd.p + hd.res

    for hd in heads:
        hd.n16 = transpose_bd(hd.n).T.astype(BF16)
        hd.eg = jnp.exp(hd.gcol)
        hd.rhs = jnp.concatenate([v_ref[:, hd.cols] * hd.bcol, hd.k * (hd.bcol * hd.eg)], axis=1)
    for hd in heads:
        hd.sol16 = (hd.rhs + _dot(hd.n16, hd.rhs.astype(BF16))).astype(BF16)
    for hd in heads:
        hd.qkd16 = transpose_bd(hd.qk * hd.dm).T.astype(BF16)
    for hd in heads:
        hd.x = _dot(hd.qkd16, hd.sol16)
    for hl, hd in enumerate(heads):
        hd.qp16 = (hd.q * hd.eg - hd.x[:, HEAD:]).astype(BF16)
        hd.kd16 = (hd.k * jnp.exp(hd.gtcol - hd.gcol)).astype(BF16)
        hd.state = s_ref[hl]
        hd.o = []

    for j in range(r // CHUNK):
        sl = slice(j * CHUNK, (j + 1) * CHUNK)
        for hd in heads:
            hd.kc = lax.dot_general(hd.kd16[sl], hd.sol16[sl], tn, preferred_element_type=F32)
        for hd in heads:
            lhs = jnp.concatenate([hd.kc[:, HEAD:].astype(BF16), hd.qp16[sl]], axis=0)
            hd.ks_qs = _dot(lhs, hd.state.astype(BF16))
        for hl, hd in enumerate(heads):
            hd.o.append(hd.ks_qs[HEAD:] + hd.x[sl, :HEAD])
            hd.state = (hd.state * jnp.exp(gtrow_ref[hl, 0, j:j + 1, :]) + hd.kc[:, :HEAD]
                        - hd.ks_qs[:HEAD])

    for hl, hd in enumerate(heads):
        s_ref[hl] = hd.state
        o = jnp.concatenate(hd.o, axis=0)
        o = o * lax.rsqrt(jnp.mean(o * o, axis=-1, keepdims=True) + NORM_EPS) * nw_ref[...]
        o_ref[:, hd.cols] = (o * _silu(z_ref[:, hd.cols])).astype(o_ref.dtype)


def _gdn_call(qk, v, z, gc, bt, gt, gcrow, btrow, gtrow, nw):
    t = qk.shape[0]
    r = GDN_ROWS
    p = GDN_PAIRS
    k_off = N_QK_HEADS // p
    return pl.pallas_call(
        _gdn_kernel,
        grid=(N_QK_HEADS // p, t // r),
        in_specs=[
            pl.BlockSpec((r, p * HEAD), lambda pp, tb: (tb, pp)),
            pl.BlockSpec((r, p * HEAD), lambda pp, tb: (tb, k_off + pp)),
            pl.BlockSpec((r, 2 * p * HEAD), lambda pp, tb: (tb, pp)),
            pl.BlockSpec((r, 2 * p * HEAD), lambda pp, tb: (tb, pp)),
            pl.BlockSpec((r, LANES), lambda pp, tb: (tb, 0)),
            pl.BlockSpec((r, LANES), lambda pp, tb: (tb, 0)),
            pl.BlockSpec((r, LANES), lambda pp, tb: (tb, 0)),
            pl.BlockSpec((2 * p, 1, 1, r), lambda pp, tb: (pp, tb, 0, 0)),
            pl.BlockSpec((2 * p, 1, 1, r), lambda pp, tb: (pp, tb, 0, 0)),
            pl.BlockSpec((2 * p, 1, r // CHUNK, LANES), lambda pp, tb: (pp, tb, 0, 0)),
            pl.BlockSpec((1, HEAD), lambda pp, tb: (0, 0)),
        ],
        out_specs=pl.BlockSpec((r, 2 * p * HEAD), lambda pp, tb: (tb, pp)),
        out_shape=jax.ShapeDtypeStruct((t, V_WIDTH), BF16),
        scratch_shapes=[pltpu.VMEM((2 * p, HEAD, HEAD), F32)],
        compiler_params=_params(("parallel", "arbitrary")),
        name="gdn",
    )(qk, qk, v, z, gc, bt, gt, gcrow, btrow, gtrow, nw)


def _layer_norm(r, g, b):
    mu = jnp.mean(r, axis=-1, keepdims=True)
    rc = r - mu
    var = jnp.mean(rc * rc, axis=-1, keepdims=True)
    return rc * lax.rsqrt(var + LN_EPS) * g + b


def _dwconv_kernel(c_ref, w_ref, b_ref, g_ref, beta_ref, o_ref, buf, shifted, accbuf):
    i = pl.program_id(0)
    tm = CONV_TM
    ch = buf.shape[1]
    n_shift = HALO + tm - SUBLANES

    @pl.when(i == 0)
    def _():
        buf[0:HALO, :] = jnp.zeros((HALO, ch), F32)

    buf[HALO:HALO + tm, :] = c_ref[...]
    for b in range(1, SUBLANES):
        shifted[b - 1] = buf[pl.ds(b, n_shift), :]

    def row_body(rc, carry):
        r0 = pl.multiple_of(rc * CONV_RC, CONV_RC)
        for cc in range(ch // CONV_CC):
            cs = slice(cc * CONV_CC, (cc + 1) * CONV_CC)
            acc = buf[pl.ds(HALO + r0, CONV_RC), cs] * w_ref[CONV_WIDTH - 1:CONV_WIDTH, cs] + b_ref[:, cs]
            for s in range(CONV_WIDTH - 1):
                a, b = divmod(HALO - (CONV_WIDTH - 1) + s, SUBLANES)
                if b == 0:
                    src = buf[pl.ds(a * SUBLANES + r0, CONV_RC), cs]
                else:
                    src = shifted[b - 1, pl.ds(a * SUBLANES + r0, CONV_RC), cs]
                acc = acc + src * w_ref[s:s + 1, cs]
            accbuf[pl.ds(r0, CONV_RC), cs] = acc
        return carry

    lax.fori_loop(0, tm // CONV_RC, row_body, 0)
    buf[0:HALO, :] = buf[tm:tm + HALO, :]
    o_ref[...] = _silu(_layer_norm(accbuf[...], g_ref[...], beta_ref[...])).astype(o_ref.dtype)


def _dwconv_call(c, w, b, g, beta):
    t, ch = c.shape
    tm = CONV_TM
    vec = pl.BlockSpec((1, ch), lambda i: (0, 0))
    return pl.pallas_call(
        _dwconv_kernel,
        grid=(t // tm,),
        in_specs=[pl.BlockSpec((tm, ch), lambda i: (i, 0)),
                  pl.BlockSpec((CONV_WIDTH, ch), lambda i: (0, 0)), vec, vec, vec],
        out_specs=pl.BlockSpec((tm, ch), lambda i: (i, 0)),
        out_shape=jax.ShapeDtypeStruct((t, ch), BF16),
        scratch_shapes=[pltpu.VMEM((HALO + tm, ch), F32),
                        pltpu.VMEM((SUBLANES - 1, HALO + tm - SUBLANES, ch), F32),
                        pltpu.VMEM((tm, ch), F32)],
        compiler_params=_params(("arbitrary",)),
        name="dwconv_ln",
    )(c, w, b, g, beta)


def _outproj_kernel(m_ref, w_ref, x_ref, g_ref, b_ref, h_ref):
    y = _dot(m_ref[...], w_ref[...])
    h_ref[...] = _layer_norm(DN_ALPHA * x_ref[...] + y, g_ref[...], b_ref[...])


def _outproj_call(mixed, w16, x, g, b):
    t, d = x.shape
    tm = 256
    vec = pl.BlockSpec((1, d), lambda i: (0, 0))
    return pl.pallas_call(
        _outproj_kernel,
        grid=(t // tm,),
        in_specs=[pl.BlockSpec((tm, d), lambda i: (i, 0)), pl.BlockSpec((d, d), lambda i: (0, 0)),
                  pl.BlockSpec((tm, d), lambda i: (i, 0)), vec, vec],
        out_specs=pl.BlockSpec((tm, d), lambda i: (i, 0)),
        out_shape=jax.ShapeDtypeStruct((t, d), F32),
        compiler_params=_params(("parallel",)),
        name="outproj_ln1",
    )(mixed, w16, x, g, b)


def _route_kernel(h_ref, w_ref, bias_ref, idx_ref, wts_ref, rank_ref, cnt_ref, carry):
    @pl.when(pl.program_id(0) == 0)
    def _():
        carry[...] = jnp.zeros(carry.shape, F32)

    logits = jnp.dot(h_ref[...], w_ref[...], precision=lax.Precision.HIGHEST,
                     preferred_element_type=F32)
    scores = _sigmoid(logits)
    biased = scores + bias_ref[...]
    shape = biased.shape
    tm = shape[0]
    lane_i = lax.broadcasted_iota(I32, shape, 1)
    lane = lane_i.astype(F32)
    per_group = N_EXPERTS // N_GROUPS
    grp = (lane_i // per_group).astype(F32)
    neg = -jnp.inf
    big = 1e9

    def rmax(x):
        return jnp.max(x, axis=1, keepdims=True)

    def rmin(x):
        return jnp.min(x, axis=1, keepdims=True)

    def rsum(x):
        return jnp.sum(x, axis=1, keepdims=True)

    gs = jnp.zeros(shape, F32)
    for g in range(N_GROUPS):
        in_g = grp == float(g)
        m = jnp.where(in_g, biased, neg)
        m1 = rmax(m)
        i1 = rmin(jnp.where(m == m1, lane, big))
        m2 = rmax(jnp.where(lane == i1, neg, m))
        gs = jnp.where(in_g, m1 + m2, gs)
    sel_g = jnp.zeros(shape, jnp.bool_)
    cur = gs
    for _ in range(TOPK_GROUPS):
        mx = rmax(cur)
        gi = rmin(jnp.where(cur == mx, grp, big))
        hit = grp == gi
        sel_g = sel_g | hit
        cur = jnp.where(hit, neg, cur)
    masked = jnp.where(sel_g, biased, neg)
    ids, ws = [], []
    sel = jnp.zeros(shape, F32)
    for _ in range(TOP_K):
        mx = rmax(masked)
        ik = rmin(jnp.where(masked == mx, lane, big))
        hit = lane == ik
        ws.append(rsum(jnp.where(hit, scores, 0.0)))
        ids.append(ik)
        sel = jnp.where(hit, 1.0, sel)
        masked = jnp.where(hit, neg, masked)
    w = jnp.concatenate(ws, axis=1)
    wts_ref[...] = w / rsum(w) * ROUTE_SCALE
    idx_ref[...] = jnp.concatenate(ids, axis=1).astype(I32)

    earlier = (lax.broadcasted_iota(I32, (tm, tm), 1) < lax.broadcasted_iota(I32, (tm, tm), 0))
    before = _dot(earlier.astype(BF16), sel.astype(BF16)) + carry[...]
    ranks = [rsum(jnp.where(lane == ik, before, 0.0)) for ik in ids]
    rank_ref[...] = jnp.concatenate(ranks, axis=1).astype(I32)
    carry[...] = carry[...] + jnp.sum(sel, axis=0, keepdims=True)
    cnt_ref[...] = carry[...]


def _route_call(h, w_router, bias):
    t, d = h.shape
    tm = ROUTE_TM
    tok = pl.BlockSpec((tm, TOP_K), lambda i: (i, 0))
    return pl.pallas_call(
        _route_kernel,
        grid=(t // tm,),
        in_specs=[pl.BlockSpec((tm, d), lambda i: (i, 0)),
                  pl.BlockSpec((d, N_EXPERTS), lambda i: (0, 0)),
                  pl.BlockSpec((1, N_EXPERTS), lambda i: (0, 0))],
        out_specs=(tok, tok, tok, pl.BlockSpec((1, N_EXPERTS), lambda i: (0, 0))),
        out_shape=(jax.ShapeDtypeStruct((t, TOP_K), I32), jax.ShapeDtypeStruct((t, TOP_K), F32),
                   jax.ShapeDtypeStruct((t, TOP_K), I32), jax.ShapeDtypeStruct((1, N_EXPERTS), F32)),
        scratch_shapes=[pltpu.VMEM((1, N_EXPERTS), F32)],
        compiler_params=_params(("arbitrary",)),
        name="route",
    )(h, w_router, bias)


def _dest_kernel(idx_ref, rank_ref, start_ref, dest_ref):
    idx = idx_ref[...]
    shape = (idx.shape[0], N_EXPERTS)
    lane = lax.broadcasted_iota(I32, shape, 1)
    start = start_ref[...]
    cols = []
    for kk in range(TOP_K):
        hit = lane == idx[:, kk:kk + 1]
        cols.append(jnp.sum(jnp.where(hit, start, 0.0), axis=1, keepdims=True))
    dest_ref[...] = jnp.concatenate(cols, axis=1).astype(I32) + rank_ref[...]


def _dest_call(idx, rank, start_f32):
    t = idx.shape[0]
    tm = 512
    tok = pl.BlockSpec((tm, TOP_K), lambda i: (i, 0))
    return pl.pallas_call(
        _dest_kernel,
        grid=(t // tm,),
        in_specs=[tok, tok, pl.BlockSpec((1, N_EXPERTS), lambda i: (0, 0))],
        out_specs=tok,
        out_shape=jax.ShapeDtypeStruct((t, TOP_K), I32),
        compiler_params=_params(("parallel",)),
        name="dest",
    )(idx, rank, start_f32)


def _row_copy(src, src_row, dst, dst_row, sem):
    return pltpu.make_async_copy(src.at[pl.ds(src_row, 1), :], dst.at[pl.ds(dst_row, 1), :], sem)


def _pack_halves(x):
    half = x.shape[1] // 2
    lo = lax.bitcast_convert_type(x[:, :half].astype(BF16).astype(F32), U32)
    hi = lax.bitcast_convert_type(x[:, half:].astype(BF16).astype(F32), U32)
    return hi | (lo >> 16)


def _unpack_halves(w):
    lo = lax.bitcast_convert_type(w << 16, F32)
    hi = lax.bitcast_convert_type(w & jnp.uint32(0xFFFF0000), F32)
    return lo, hi


def _rows_wait(src, dst, dst_row, n_rows, sem):
    pltpu.make_async_copy(src.at[pl.ds(0, n_rows), :], dst.at[pl.ds(dst_row, n_rows), :], sem).wait()


def _scatter_kernel(dest_ref, h_ref, xs_hbm, packed, sem):
    packed[...] = _pack_halves(h_ref[...])

    def issue(tok, c):
        for kk in range(TOP_K):
            _row_copy(packed, tok, xs_hbm, dest_ref[0, 0, tok * TOP_K + kk], sem.at[tok]).start(priority=kk % 2)
        return c

    lax.fori_loop(0, SCATTER_TM, issue, 0, unroll=2)

    def wait(tok, c):
        _rows_wait(packed, xs_hbm, 0, TOP_K, sem.at[tok])
        return c

    lax.fori_loop(0, SCATTER_TM, wait, 0, unroll=8)


def _scatter_call(dest, h):
    t, d = h.shape
    tm = SCATTER_TM
    n = tm * TOP_K
    return pl.pallas_call(
        _scatter_kernel,
        grid=(t // tm,),
        in_specs=[pl.BlockSpec((1, 1, n), lambda i: (i, 0, 0), memory_space=pltpu.SMEM),
                  pl.BlockSpec((tm, d), lambda i: (i, 0))],
        out_specs=pl.BlockSpec(memory_space=pl.ANY),
        out_shape=jax.ShapeDtypeStruct((t * TOP_K, d // 2), U32),
        scratch_shapes=[pltpu.VMEM((tm, d // 2), U32), pltpu.SemaphoreType.DMA((tm,))],
        compiler_params=_params(("arbitrary",)),
        name="dispatch",
    )(dest.reshape(t // tm, 1, n), h)


def _weight_copies(w_hbm, wgu32, wd32, wsem, e, slot):
    wg_hbm, wu_hbm, wd_hbm = w_hbm
    return [pltpu.make_async_copy(wg_hbm.at[e], wgu32.at[slot, :, 0:D_EXPERT], wsem.at[slot, 0]),
            pltpu.make_async_copy(wu_hbm.at[e], wgu32.at[slot, :, D_EXPERT:2 * D_EXPERT], wsem.at[slot, 1]),
            pltpu.make_async_copy(wd_hbm.at[e], wd32.at[slot], wsem.at[slot, 2])]


def _expert_kernel(vb_ref, ve_ref, lo_ref, hi_ref, fb_ref, fe_ref, slot_ref, nxt1_ref, nxt2_ref, nv_ref,
                   x_ref, wg_hbm, wu_hbm, wd_hbm, y_ref, wgu32, wd32, wsem):
    v = pl.program_id(0)
    w_hbm = (wg_hbm, wu_hbm, wd_hbm)

    def start(e, slot):
        for cp in _weight_copies(w_hbm, wgu32, wd32, wsem, e, slot):
            cp.start(priority=WEIGHT_DMA_QUEUE)

    @pl.when(v < nv_ref[0])
    def _():
        slot = slot_ref[v]

        @pl.when(fe_ref[v] == 1)
        def _():
            @pl.when(v == 0)
            def _():
                start(ve_ref[v], slot)

                @pl.when(nxt1_ref[v] >= 0)
                def _():
                    start(nxt1_ref[v], (slot + 1) % WEIGHT_SLOTS)

            @pl.when(nxt2_ref[v] >= 0)
            def _():
                start(nxt2_ref[v], (slot + 2) % WEIGHT_SLOTS)

            for cp in _weight_copies(w_hbm, wgu32, wd32, wsem, ve_ref[v], slot):
                cp.wait()

        x_lo, x_hi = _unpack_halves(x_ref[...])
        x = jnp.concatenate([x_lo.astype(BF16), x_hi.astype(BF16)], axis=1)
        gu = _dot(x, wgu32[slot])
        act = _silu(gu[:, :D_EXPERT]) * gu[:, D_EXPERT:]
        y = _pack_halves(_dot(act.astype(BF16), wd32[slot]))
        rows = lax.broadcasted_iota(I32, y.shape, 0)
        mine = (rows >= lo_ref[v]) & (rows < hi_ref[v])

        @pl.when(fb_ref[v] == 1)
        def _():
            y_ref[...] = jnp.where(mine, y, jnp.uint32(0))

        @pl.when(fb_ref[v] == 0)
        def _():
            y_ref[...] = jnp.where(mine, y, y_ref[...])


def _expert_call(tables, xs, w_gate, w_up, w_down):
    n_rows, dp = xs.shape
    d = 2 * dp
    blk = EXPERT_BLOCK
    n_visits = tables[0].shape[0]
    any_spec = pl.BlockSpec(memory_space=pl.ANY)
    grid_spec = pltpu.PrefetchScalarGridSpec(
        num_scalar_prefetch=len(tables),
        grid=(n_visits,),
        in_specs=[pl.BlockSpec((blk, dp), lambda v, vb, *_: (vb[v], 0)), any_spec, any_spec, any_spec],
        out_specs=pl.BlockSpec((blk, dp), lambda v, vb, *_: (vb[v], 0)),
        scratch_shapes=[
            pltpu.VMEM((WEIGHT_SLOTS, d, 2 * D_EXPERT), F32),
            pltpu.VMEM((WEIGHT_SLOTS, D_EXPERT, d), F32),
            pltpu.SemaphoreType.DMA((WEIGHT_SLOTS, 3)),
        ],
    )
    return pl.pallas_call(
        _expert_kernel,
        grid_spec=grid_spec,
        out_shape=jax.ShapeDtypeStruct((n_rows, dp), U32),
        compiler_params=_params(("arbitrary",)),
        name="experts",
    )(*tables, xs, w_gate, w_up, w_down)


def _visit_tables(counts, n_rows):
    blk = EXPERT_BLOCK
    n_visits = n_rows // blk + N_EXPERTS - 1
    end = jnp.cumsum(counts)
    start = end - counts
    nonempty = counts > 0
    first_blk = start // blk
    last_blk = jnp.maximum(end - 1, 0) // blk
    nvis = jnp.where(nonempty, last_blk - first_blk + 1, 0)
    vis_end = jnp.cumsum(nvis)
    vis_start = vis_end - nvis
    n_used = vis_end[-1]
    ids = jnp.arange(N_EXPERTS, dtype=I32)
    ordinal = jnp.cumsum(nonempty.astype(I32)) - 1
    nxt_incl = lax.cummin(jnp.where(nonempty, ids, N_EXPERTS), axis=0, reverse=True)
    nxt_e = jnp.concatenate([nxt_incl[1:], jnp.full((1,), N_EXPERTS, I32)])
    nxt_e = jnp.where(nxt_e >= N_EXPERTS, -1, nxt_e)
    nxt2_e = jnp.sum(jnp.where(nxt_e[:, None] == ids[None, :], nxt_e[None, :], 0), axis=1)
    nxt2_e = jnp.where(nxt_e >= 0, nxt2_e, -1)
    v = jnp.arange(n_visits, dtype=I32)
    vc = jnp.clip(v, 0, jnp.maximum(n_used - 1, 0))
    ve = jnp.minimum(jnp.sum((vis_end[None, :] <= vc[:, None]).astype(I32), axis=1), N_EXPERTS - 1)
    onehot = ve[:, None] == ids[None, :]
    look = lambda tab: jnp.sum(jnp.where(onehot, tab.astype(I32)[None, :], 0), axis=1)
    v_first = look(vis_start)
    vb = look(first_blk) + (vc - v_first)
    lo = jnp.clip(look(start) - vb * blk, 0, blk)
    hi = jnp.clip(look(end) - vb * blk, 0, blk)
    fb = jnp.concatenate([jnp.ones((1,), I32), (vb[1:] != vb[:-1]).astype(I32)])
    fe = (vc == v_first).astype(I32)
    tables = (vb, ve, lo, hi, fb, fe, look(ordinal % WEIGHT_SLOTS), look(nxt_e), look(nxt2_e),
              n_used.astype(I32).reshape(1))
    return tables, start


def _shared_kernel(h_ref, wg_ref, wu_ref, wd_ref, y_ref, wg16, wu16, wd16):
    @pl.when(pl.program_id(0) == 0)
    def _():
        wg16[...] = wg_ref[...].astype(BF16)
        wu16[...] = wu_ref[...].astype(BF16)
        wd16[...] = wd_ref[...].astype(BF16)

    x = h_ref[...].astype(BF16)
    g = _dot(x, wg16[...])
    u = _dot(x, wu16[...])
    y_ref[...] = _dot((_silu(g) * u).astype(BF16), wd16[...])


def _shared_call(h, wg, wu, wd):
    t, d = h.shape
    ds = wg.shape[1]
    tm = 512
    return pl.pallas_call(
        _shared_kernel,
        grid=(t // tm,),
        in_specs=[pl.BlockSpec((tm, d), lambda i: (i, 0)),
                  pl.BlockSpec((d, ds), lambda i: (0, 0)),
                  pl.BlockSpec((d, ds), lambda i: (0, 0)),
                  pl.BlockSpec((ds, d), lambda i: (0, 0))],
        out_specs=pl.BlockSpec((tm, d), lambda i: (i, 0)),
        out_shape=jax.ShapeDtypeStruct((t, d), F32),
        scratch_shapes=[pltpu.VMEM((d, ds), BF16), pltpu.VMEM((d, ds), BF16),
                        pltpu.VMEM((ds, d), BF16)],
        compiler_params=_params(("arbitrary",)),
        name="shared_expert",
    )(h, wg, wu, wd)


def _combine_kernel(dest_ref, dnext_ref, ys_hbm, wts_ref, h_ref, ysh_ref, g_ref, b_ref, o_ref,
                    buf, sem):
    tm = COMBINE_TM
    i = pl.program_id(0)
    slot = i % 2

    def gather(d_ref, s):
        def issue(r, c):
            for kk in range(TOP_K):
                a = kk * tm + r
                _row_copy(ys_hbm, d_ref[0, 0, a], buf.at[s], a, sem.at[s, kk]).start(priority=kk % 2)
            return c

        lax.fori_loop(0, tm, issue, 0, unroll=2)

    @pl.when(i == 0)
    def _():
        gather(dest_ref, slot)

    @pl.when(i + 1 < pl.num_programs(0))
    def _():
        gather(dnext_ref, 1 - slot)

    acc = DN_ALPHA * h_ref[...] + ysh_ref[...]
    wts = wts_ref[...]
    for kk in range(TOP_K):
        _rows_wait(ys_hbm, buf.at[slot], kk * tm, tm, sem.at[slot, kk])
    half = acc.shape[1] // 2
    acc_lo, acc_hi = acc[:, :half], acc[:, half:]
    for kk in range(TOP_K):
        y_lo, y_hi = _unpack_halves(buf[slot, kk * tm:(kk + 1) * tm, :])
        acc_lo = acc_lo + wts[:, kk:kk + 1] * y_lo
        acc_hi = acc_hi + wts[:, kk:kk + 1] * y_hi
    o_ref[...] = _layer_norm(jnp.concatenate([acc_lo, acc_hi], axis=1), g_ref[...], b_ref[...])


def _combine_call(dest_km, ys, wts, h, ysh, g, b):
    t, d = h.shape
    tm = COMBINE_TM
    n = tm * TOP_K
    n_tiles = t // tm
    vec = pl.BlockSpec((1, d), lambda i: (0, 0))
    return pl.pallas_call(
        _combine_kernel,
        grid=(n_tiles,),
        in_specs=[
            pl.BlockSpec((1, 1, n), lambda i: (i, 0, 0), memory_space=pltpu.SMEM),
            pl.BlockSpec((1, 1, n), lambda i: (jnp.minimum(i + 1, n_tiles - 1), 0, 0),
                         memory_space=pltpu.SMEM),
            pl.BlockSpec(memory_space=pl.ANY),
            pl.BlockSpec((tm, TOP_K), lambda i: (i, 0)),
            pl.BlockSpec((tm, d), lambda i: (i, 0)),
            pl.BlockSpec((tm, d), lambda i: (i, 0)),
            vec, vec,
        ],
        out_specs=pl.BlockSpec((tm, d), lambda i: (i, 0)),
        out_shape=jax.ShapeDtypeStruct((t, d), F32),
        scratch_shapes=[pltpu.VMEM((2, n, d // 2), U32), pltpu.SemaphoreType.DMA((2, TOP_K))],
        compiler_params=_params(("arbitrary",)),
        name="combine_ln2",
    )(dest_km, dest_km, ys, wts, h, ysh, g, b)


def _token_mixing(x, w_in, w_short_conv, a_log, dt_bias, delta_norm_w, w_o_delta, conv_dw_w,
                  conv_dw_b, conv_ln_g, conv_ln_b, w_pw2, b_pw2, w_out, ln1_g, ln1_b):
    t, d = x.shape
    x16 = x.astype(BF16)
    w_t = w_in.T
    row = lambda a: a.reshape(1, -1)
    tile = pl.BlockSpec((TM, TN), lambda j, i: (i, j))
    wide = pl.BlockSpec((TM_WIDE, TN), lambda j, i: (i, j))
    conv_buf = pltpu.VMEM((TM_WIDE + SUBLANES, TN), F32)

    scale = jnp.concatenate([jnp.full((1, QK_WIDTH), HEAD ** -0.5, F32), jnp.ones((1, QK_WIDTH), F32)], axis=1)
    qk = _proj_call(
        "proj_qk", x16, [w_t], [0], 2 * QK_WIDTH // TN, _qk_epilogue,
        [w_short_conv, scale],
        [pl.BlockSpec((SHORT_CONV, TN), lambda j, i: (0, j)), pl.BlockSpec((1, TN), lambda j, i: (0, j))],
        jax.ShapeDtypeStruct((t, 2 * QK_WIDTH), F32), wide, scratch=[conv_buf], tm=TM_WIDE, transposed=True)
    v_off = 2 * QK_WIDTH // TN
    v = _proj_call(
        "proj_v", x16, [w_t], [2 * QK_WIDTH], V_WIDTH // TN, _v_epilogue,
        [w_short_conv], [pl.BlockSpec((SHORT_CONV, TN), lambda j, i: (0, j + v_off))],
        jax.ShapeDtypeStruct((t, V_WIDTH), F32), wide, scratch=[conv_buf], tm=TM_WIDE, transposed=True)
    z = _proj_call(
        "proj_z", x16, [w_t], [COL_Z], V_WIDTH // TN, _z_epilogue, [], [],
        jax.ShapeDtypeStruct((t, V_WIDTH), F32), wide, tm=TM_WIDE, transposed=True)

    pad_h = lambda a: jnp.pad(a.reshape(1, -1), ((0, 0), (0, LANES - N_V_HEADS)))
    lane_vec = pl.BlockSpec((1, LANES), lambda j, i: (0, 0))
    lane_tile = pl.BlockSpec((TM, LANES), lambda j, i: (i, 0))
    lane_shape = jax.ShapeDtypeStruct((t, LANES), F32)
    gc, bt, gt = _proj_call(
        "proj_ab", x16, [w_t], [COL_AB], 1, _ab_epilogue, [pad_h(a_log), pad_h(dt_bias)],
        [lane_vec, lane_vec], (lane_shape, lane_shape, lane_shape), (lane_tile, lane_tile, lane_tile),
        tn=LANES, transposed=True)

    c = _proj_call(
        "proj_glu", x16, [w_t, w_t], [COL_GLU, COL_GLU + d], d // TN, _glu_epilogue, [], [],
        jax.ShapeDtypeStruct((t, d), F32), wide, tm=TM_WIDE, transposed=True)
    gates = _proj_call(
        "proj_gates", x16, [w_t], [COL_GATES], 2 * d // TN, _gates_epilogue, [], [],
        jax.ShapeDtypeStruct((t, 2 * d), F32), wide, tm=TM_WIDE, transposed=True)

    r = GDN_ROWS
    gcrow = gc[:, :N_V_HEADS].T.reshape(N_V_HEADS, t // r, 1, r)
    btrow = bt[:, N_V_HEADS:2 * N_V_HEADS].T.reshape(N_V_HEADS, t // r, 1, r)
    gt_chunk = gt[CHUNK - 1::CHUNK, :N_V_HEADS].T
    gtrow = jnp.broadcast_to(gt_chunk[:, :, None], (N_V_HEADS, t // CHUNK, LANES))
    gtrow = gtrow.reshape(N_V_HEADS, t // r, r // CHUNK, LANES)
    og = _gdn_call(qk, v, z, gc, bt, gt, gcrow, btrow, gtrow, row(delta_norm_w))

    ya = _proj_call(
        "proj_odelta", og, [w_o_delta], [0], d // TN, _odelta_epilogue, [gates], [tile],
        jax.ShapeDtypeStruct((t, d), F32), tile)

    c_act = _dwconv_call(c, conv_dw_w, row(conv_dw_b), row(conv_ln_g), row(conv_ln_b))
    g_off = d // TN
    mixed = _proj_call(
        "proj_pw2", c_act, [w_pw2], [0], d // TN, _pw2_epilogue, [row(b_pw2), gates, ya],
        [pl.BlockSpec((1, TN), lambda j, i: (0, j)),
         pl.BlockSpec((TM_WIDE, TN), lambda j, i: (i, j + g_off)), wide],
        jax.ShapeDtypeStruct((t, d), BF16), wide, tm=TM_WIDE)
    return _outproj_call(mixed, w_out.astype(BF16), x, row(ln1_g), row(ln1_b))


def _moe(h, w_router, router_bias, w_gate, w_up, w_down, w_sh_gate, w_sh_up, w_sh_down, ln2_g, ln2_b):
    t, d = h.shape
    idx, wts, rank, counts = _route_call(h, w_router, router_bias.reshape(1, -1))
    tables, start = _visit_tables(counts.reshape(-1).astype(I32), t * TOP_K)
    dest = _dest_call(idx, rank, start.astype(F32).reshape(1, -1))
    xs = _scatter_call(dest, h)
    ys = _expert_call(tables, xs, w_gate, w_up, w_down)
    ysh = _shared_call(h, w_sh_gate, w_sh_up, w_sh_down)
    tm = COMBINE_TM
    dest_km = dest.reshape(t // tm, tm, TOP_K).transpose(0, 2, 1).reshape(t // tm, 1, tm * TOP_K)
    return _combine_call(dest_km, ys, wts, h, ysh, ln2_g.reshape(1, -1), ln2_b.reshape(1, -1))


def kernel(x, w_in, w_short_conv, a_log, dt_bias, delta_norm_w, w_o_delta, conv_dw_w, conv_dw_b,
           conv_ln_g, conv_ln_b, w_pw2, b_pw2, w_out, ln1_g, ln1_b, w_router, router_bias, w_gate,
           w_up, w_down, w_sh_gate, w_sh_up, w_sh_down, ln2_g, ln2_b):
    batch, seq, d = x.shape
    depth = w_in.shape[0]
    outs = []
    for bi in range(batch):
        h = x[bi]
        for li in range(depth):
            h = _token_mixing(h, w_in[li], w_short_conv[li], a_log[li], dt_bias[li], delta_norm_w[li],
                              w_o_delta[li], conv_dw_w[li], conv_dw_b[li], conv_ln_g[li], conv_ln_b[li],
                              w_pw2[li], b_pw2[li], w_out[li], ln1_g[li], ln1_b[li])
            h = _moe(h, w_router[li], router_bias[li], w_gate[li], w_up[li], w_down[li],
                     w_sh_gate[li], w_sh_up[li], w_sh_down[li], ln2_g[li], ln2_b[li])
        outs.append(h)
    return jnp.stack(outs, axis=0)
```

```python
import jax
import jax.numpy as jnp
from jax import lax
from jax.experimental import pallas as pl
from jax.experimental.pallas import tpu as pltpu

F32 = jnp.float32
BF16 = jnp.bfloat16
I32 = jnp.int32
U32 = jnp.uint32

D_MODEL = 2048
CHUNK = 64
N_QK_HEADS = 16
N_V_HEADS = 32
HEAD = 128
QK_WIDTH = N_QK_HEADS * HEAD
V_WIDTH = N_V_HEADS * HEAD
SHORT_CONV = 4
CONV_WIDTH = 31
N_EXPERTS = 256
TOP_K = 8
N_GROUPS = 8
TOPK_GROUPS = 4
D_EXPERT = 512
ROUTE_SCALE = 2.5
DN_ALPHA = 2.0 ** 0.25
LN_EPS = 1e-5
NORM_EPS = 1e-6
COL_Z = 2 * QK_WIDTH + V_WIDTH
COL_AB = COL_Z + V_WIDTH
COL_GLU = COL_AB + 2 * N_V_HEADS
COL_GATES = COL_GLU + 2 * D_MODEL

LANES = 128
SUBLANES = 8
VMEM_LIMIT = 56 * 1024 * 1024
TM = 512
TM_WIDE = 1024
TN = 512
GDN_ROWS = 256
GDN_PAIRS = 4
CONV_TM = 256
HALO = 32
CONV_RC = 64
CONV_CC = 256
ROW_ALIGN = 64
ROUTE_TM = 256
EXPERT_BLOCK = 128
SCATTER_TM = 128
COMBINE_TM = 64
WEIGHT_DMA_QUEUE = 1
WEIGHT_SLOTS = 3


def _params(sem, vmem=VMEM_LIMIT):
    return pltpu.CompilerParams(dimension_semantics=sem, vmem_limit_bytes=vmem)


def _sigmoid(x):
    return jax.nn.sigmoid(x)


def _silu(x):
    return x * jax.nn.sigmoid(x)


def _softplus(x):
    return jnp.maximum(x, 0.0) + jnp.log1p(jnp.exp(-jnp.abs(x)))


def _dot(a, b):
    return jnp.dot(a, b, preferred_element_type=F32)


def _proj_call(name, x, ws, w_offs, n_tiles, epilogue, extras, extra_specs, out_shapes, out_specs,
               scratch=(), tm=TM, tn=TN, transposed=False):
    m, k = x.shape
    single_out = not isinstance(out_shapes, (tuple, list))
    if single_out:
        out_shapes, out_specs = (out_shapes,), (out_specs,)
    nw, ne, no = len(ws), len(extras), len(out_shapes)
    needs_cast = [w.dtype != BF16 for w in ws]
    contract = (((1,), (1,)), ((), ())) if transposed else (((1,), (0,)), ((), ()))

    def body(*refs):
        x_ref = refs[0]
        w_refs = refs[1:1 + nw]
        ex_refs = refs[1 + nw:1 + nw + ne]
        out_refs = refs[1 + nw + ne:1 + nw + ne + no]
        scr = refs[1 + nw + ne + no:]
        wb_refs = scr[:sum(needs_cast)]
        rest = scr[sum(needs_cast):]
        i = pl.program_id(1)
        wsrc, c = [], 0
        for kk in range(nw):
            if needs_cast[kk]:
                wsrc.append(wb_refs[c])
                c += 1
            else:
                wsrc.append(w_refs[kk])

        @pl.when(i == 0)
        def _():
            cc = 0
            for kk in range(nw):
                if needs_cast[kk]:
                    wb_refs[cc][...] = w_refs[kk][...].astype(BF16)
                    cc += 1

        xv = x_ref[...]
        accs = [lax.dot_general(xv, wr[...], contract, preferred_element_type=F32) for wr in wsrc]
        epilogue(i, accs, ex_refs, out_refs, rest)

    in_specs = [pl.BlockSpec((tm, k), lambda j, i: (i, 0))]
    for off in w_offs:
        if transposed:
            in_specs.append(pl.BlockSpec(
                (pl.Element(tn), pl.Element(k)),
                lambda j, i, off=off: (pl.multiple_of(off + tn * j, ROW_ALIGN), 0)))
        else:
            in_specs.append(pl.BlockSpec((k, tn), lambda j, i, off=off: (0, j + off)))
    in_specs += list(extra_specs)
    w_tile = (tn, k) if transposed else (k, tn)
    scratch_shapes = [pltpu.VMEM(w_tile, BF16) for c in needs_cast if c] + list(scratch)
    res = pl.pallas_call(
        body,
        grid=(n_tiles, m // tm),
        in_specs=in_specs,
        out_specs=tuple(out_specs),
        out_shape=tuple(out_shapes),
        scratch_shapes=scratch_shapes,
        compiler_params=_params(("parallel", "arbitrary")),
        name=name,
    )(x, *ws, *extras)
    return res[0] if single_out else res


def _short_conv_silu(i, acc, cw_ref, buf, tm):
    @pl.when(i == 0)
    def _():
        buf[0:SUBLANES, :] = jnp.zeros((SUBLANES, buf.shape[1]), F32)

    buf[SUBLANES:SUBLANES + tm, :] = acc
    cw = cw_ref[...]
    y = acc * cw[SHORT_CONV - 1:SHORT_CONV, :]
    for s in range(SHORT_CONV - 1):
        y = y + buf[pl.ds(SUBLANES - (SHORT_CONV - 1) + s, tm), :] * cw[s:s + 1, :]
    buf[0:SUBLANES, :] = buf[tm:tm + SUBLANES, :]
    return _silu(y)


def _qk_epilogue(i, accs, ex, outs, scr):
    cw_ref, scale_ref = ex
    (buf,) = scr
    y = _short_conv_silu(i, accs[0], cw_ref, buf, accs[0].shape[0])
    parts = []
    for g in range(TN // HEAD):
        yg = y[:, g * HEAD:(g + 1) * HEAD]
        parts.append(yg * lax.rsqrt(jnp.sum(yg * yg, axis=-1, keepdims=True) + NORM_EPS))
    outs[0][...] = jnp.concatenate(parts, axis=1) * scale_ref[...]


def _v_epilogue(i, accs, ex, outs, scr):
    (cw_ref,) = ex
    (buf,) = scr
    outs[0][...] = _short_conv_silu(i, accs[0], cw_ref, buf, accs[0].shape[0])


def _z_epilogue(i, accs, ex, outs, scr):
    outs[0][...] = accs[0]


def _glu_epilogue(i, accs, ex, outs, scr):
    outs[0][...] = accs[0] * _sigmoid(accs[1])


def _gates_epilogue(i, accs, ex, outs, scr):
    outs[0][...] = _sigmoid(accs[0])


def _ab_epilogue(i, accs, ex, outs, scr):
    alog_ref, dtb_ref = ex
    gc_ref, bt_ref, gt_ref = outs
    acc = accs[0]
    g = -jnp.exp(alog_ref[...]) * _softplus(acc + dtb_ref[...])
    row = lax.broadcasted_iota(I32, g.shape, 0) % CHUNK
    s = 1
    while s < CHUNK:
        g = g + jnp.where(row >= s, pltpu.roll(g, s, axis=0), 0.0)
        s *= 2
    n_chunks = g.shape[0] // CHUNK
    tot = g.reshape(n_chunks, CHUNK, LANES)[:, CHUNK - 1:CHUNK, :]
    gc_ref[...] = g
    bt_ref[...] = _sigmoid(acc)
    gt_ref[...] = jnp.broadcast_to(tot, (n_chunks, CHUNK, LANES)).reshape(g.shape)


def _odelta_epilogue(i, accs, ex, outs, scr):
    (gate_ref,) = ex
    outs[0][...] = accs[0] * gate_ref[...]


def _pw2_epilogue(i, accs, ex, outs, scr):
    bias_ref, gate_ref, ya_ref = ex
    outs[0][...] = (ya_ref[...] + gate_ref[...] * (accs[0] + bias_ref[...])).astype(BF16)


class _Head:
    pass


def _gdn_kernel(q_ref, k_ref, v_ref, z_ref, gc_ref, bt_ref, gt_ref, gcrow_ref, btrow_ref, gtrow_ref,
                nw_ref, o_ref, s_ref):
    pp = pl.program_id(0)
    tb = pl.program_id(1)
    r = GDN_ROWS
    half = r // 2
    n_heads = 2 * GDN_PAIRS

    @pl.when(tb == 0)
    def _():
        s_ref[...] = jnp.zeros(s_ref.shape, F32)

    col_tok = lax.broadcasted_iota(I32, (half, r), 0)
    row_tok = lax.broadcasted_iota(I32, (half, r), 1)
    first_half = row_tok < half
    row_in_half = row_tok % half
    same = (row_in_half // CHUNK) == (col_tok // CHUNK)
    causal = same & (col_tok <= row_in_half)
    strict = same & (col_tok < row_in_half)
    lane = lax.broadcasted_iota(I32, (r, LANES), 1)
    gc_all = gc_ref[...]
    bt_all = bt_ref[...]
    gt_all = gt_ref[...]
    nt = (((1,), (1,)), ((), ()))
    tn = (((0,), (0,)), ((), ()))

    def transpose_bd(xc):
        return jnp.concatenate([jnp.where(first_half, xc, 0.0), jnp.where(first_half, 0.0, xc)], axis=0)

    heads = [_Head() for _ in range(n_heads)]
    for pi in range(GDN_PAIRS):
        q = q_ref[:, pi * HEAD:(pi + 1) * HEAD]
        k = k_ref[:, pi * HEAD:(pi + 1) * HEAD]
        k16 = k.astype(BF16)
        q16 = q.astype(BF16)
        kq = [lax.dot_general(k16[rs], jnp.concatenate([k16[rs], q16[rs]], axis=0), nt,
                              preferred_element_type=F32) for rs in (slice(0, half), slice(half, r))]
        kk = jnp.concatenate([kq[0][:, :half], kq[1][:, :half]], axis=1)
        qk = jnp.concatenate([kq[0][:, half:], kq[1][:, half:]], axis=1)
        for hd in heads[2 * pi:2 * pi + 2]:
            hd.q, hd.k, hd.kk, hd.qk = q, k, kk, qk

    for hl, hd in enumerate(heads):
        h = n_heads * pp + hl
        hd.cols = slice(hl * HEAD, (hl + 1) * HEAD)
        hd.gcol = jnp.sum(jnp.where(lane == h, gc_all, 0.0), axis=1, keepdims=True)
        hd.bcol = jnp.sum(jnp.where(lane == h + N_V_HEADS, bt_all, 0.0), axis=1, keepdims=True)
        hd.gtcol = jnp.sum(jnp.where(lane == h, gt_all, 0.0), axis=1, keepdims=True)
        gc_col_tok = jnp.concatenate([jnp.broadcast_to(hd.gcol[:half], (half, half)),
                                      jnp.broadcast_to(hd.gcol[half:], (half, half))], axis=1)
        hd.dm = jnp.exp(jnp.where(causal, gcrow_ref[hl, 0] - gc_col_tok, -jnp.inf))
        hd.p = jnp.where(strict, btrow_ref[hl, 0] * hd.kk * hd.dm, 0.0)
        hd.n = -hd.p

    for hd in heads:
        hd.res = _dot(hd.p.astype(BF16), transpose_bd(hd.p).astype(BF16))
    for hd in heads:
        hd.p = hd.res
    for _ in range(4):
        for hd in heads:
            rhs = jnp.concatenate([transpose_bd(hd.p), transpose_bd(hd.n)], axis=1).astype(BF16)
            hd.res = _dot(hd.p.astype(BF16), rhs)
        for hd in heads:
            hd.n = hd.n + hd.p + hd.res[:, r:]
            hd.p = hd.res[:, :r]
    for hd in heads:
        hd.res = _dot(hd.p.astype(BF16), transpose_bd(hd.n).astype(BF16))
    for hd in heads:
        hd.n = hd.n + hd.p + hd.res

    for hd in heads:
        hd.n16 = transpose_bd(hd.n).T.astype(BF16)
        hd.eg = jnp.exp(hd.gcol)
        hd.rhs = jnp.concatenate([v_ref[:, hd.cols] * hd.bcol, hd.k * (hd.bcol * hd.eg)], axis=1)
    for hd in heads:
        hd.sol16 = (hd.rhs + _dot(hd.n16, hd.rhs.astype(BF16))).astype(BF16)
    for hd in heads:
        hd.qkd16 = transpose_bd(hd.qk * hd.dm).T.astype(BF16)
    for hd in heads:
        hd.x = _dot(hd.qkd16, hd.sol16)
    for hl, hd in enumerate(heads):
        hd.qp16 = (hd.q * hd.eg - hd.x[:, HEAD:]).astype(BF16)
        hd.kd16 = (hd.k * jnp.exp(hd.gtcol - hd.gcol)).astype(BF16)
        hd.state = s_ref[hl]
        hd.o = []

    for j in range(r // CHUNK):
        sl = slice(j * CHUNK, (j + 1) * CHUNK)
        for hd in heads:
            hd.kc = lax.dot_general(hd.kd16[sl], hd.sol16[sl], tn, preferred_element_type=F32)
        for hd in heads:
            lhs = jnp.concatenate([hd.kc[:, HEAD:].astype(BF16), hd.qp16[sl]], axis=0)
            hd.ks_qs = _dot(lhs, hd.state.astype(BF16))
        for hl, hd in enumerate(heads):
            hd.o.append(hd.ks_qs[HEAD:] + hd.x[sl, :HEAD])
            hd.state = (hd.state * jnp.exp(gtrow_ref[hl, 0, j:j + 1, :]) + hd.kc[:, :HEAD]
                        - hd.ks_qs[:HEAD])

    for hl, hd in enumerate(heads):
        s_ref[hl] = hd.state
        o = jnp.concatenate(hd.o, axis=0)
        o = o * lax.rsqrt(jnp.mean(o * o, axis=-1, keepdims=True) + NORM_EPS) * nw_ref[...]
        o_ref[:, hd.cols] = (o * _silu(z_ref[:, hd.cols])).astype(o_ref.dtype)


def _gdn_call(qk, v, z, gc, bt, gt, gcrow, btrow, gtrow, nw):
    t = qk.shape[0]
    r = GDN_ROWS
    p = GDN_PAIRS
    k_off = N_QK_HEADS // p
    return pl.pallas_call(
        _gdn_kernel,
        grid=(N_QK_HEADS // p, t // r),
        in_specs=[
            pl.BlockSpec((r, p * HEAD), lambda pp, tb: (tb, pp)),
            pl.BlockSpec((r, p * HEAD), lambda pp, tb: (tb, k_off + pp)),
            pl.BlockSpec((r, 2 * p * HEAD), lambda pp, tb: (tb, pp)),
            pl.BlockSpec((r, 2 * p * HEAD), lambda pp, tb: (tb, pp)),
            pl.BlockSpec((r, LANES), lambda pp, tb: (tb, 0)),
            pl.BlockSpec((r, LANES), lambda pp, tb: (tb, 0)),
            pl.BlockSpec((r, LANES), lambda pp, tb: (tb, 0)),
            pl.BlockSpec((2 * p, 1, 1, r), lambda pp, tb: (pp, tb, 0, 0)),
            pl.BlockSpec((2 * p, 1, 1, r), lambda pp, tb: (pp, tb, 0, 0)),
            pl.BlockSpec((2 * p, 1, r // CHUNK, LANES), lambda pp, tb: (pp, tb, 0, 0)),
            pl.BlockSpec((1, HEAD), lambda pp, tb: (0, 0)),
        ],
        out_specs=pl.BlockSpec((r, 2 * p * HEAD), lambda pp, tb: (tb, pp)),
        out_shape=jax.ShapeDtypeStruct((t, V_WIDTH), BF16),
        scratch_shapes=[pltpu.VMEM((2 * p, HEAD, HEAD), F32)],
        compiler_params=_params(("parallel", "arbitrary")),
        name="gdn",
    )(qk, qk, v, z, gc, bt, gt, gcrow, btrow, gtrow, nw)


def _layer_norm(r, g, b):
    mu = jnp.mean(r, axis=-1, keepdims=True)
    rc = r - mu
    var = jnp.mean(rc * rc, axis=-1, keepdims=True)
    return rc * lax.rsqrt(var + LN_EPS) * g + b


def _dwconv_kernel(c_ref, w_ref, b_ref, g_ref, beta_ref, o_ref, buf, shifted, accbuf):
    i = pl.program_id(0)
    tm = CONV_TM
    ch = buf.shape[1]
    n_shift = HALO + tm - SUBLANES

    @pl.when(i == 0)
    def _():
        buf[0:HALO, :] = jnp.zeros((HALO, ch), F32)

    buf[HALO:HALO + tm, :] = c_ref[...]
    for b in range(1, SUBLANES):
        shifted[b - 1] = buf[pl.ds(b, n_shift), :]

    def row_body(rc, carry):
        r0 = pl.multiple_of(rc * CONV_RC, CONV_RC)
        for cc in range(ch // CONV_CC):
            cs = slice(cc * CONV_CC, (cc + 1) * CONV_CC)
            acc = buf[pl.ds(HALO + r0, CONV_RC), cs] * w_ref[CONV_WIDTH - 1:CONV_WIDTH, cs] + b_ref[:, cs]
            for s in range(CONV_WIDTH - 1):
                a, b = divmod(HALO - (CONV_WIDTH - 1) + s, SUBLANES)
                if b == 0:
                    src = buf[pl.ds(a * SUBLANES + r0, CONV_RC), cs]
                else:
                    src = shifted[b - 1, pl.ds(a * SUBLANES + r0, CONV_RC), cs]
                acc = acc + src * w_ref[s:s + 1, cs]
            accbuf[pl.ds(r0, CONV_RC), cs] = acc
        return carry

    lax.fori_loop(0, tm // CONV_RC, row_body, 0)
    buf[0:HALO, :] = buf[tm:tm + HALO, :]
    o_ref[...] = _silu(_layer_norm(accbuf[...], g_ref[...], beta_ref[...])).astype(o_ref.dtype)


def _dwconv_call(c, w, b, g, beta):
    t, ch = c.shape
    tm = CONV_TM
    vec = pl.BlockSpec((1, ch), lambda i: (0, 0))
    return pl.pallas_call(
        _dwconv_kernel,
        grid=(t // tm,),
        in_specs=[pl.BlockSpec((tm, ch), lambda i: (i, 0)),
                  pl.BlockSpec((CONV_WIDTH, ch), lambda i: (0, 0)), vec, vec, vec],
        out_specs=pl.BlockSpec((tm, ch), lambda i: (i, 0)),
        out_shape=jax.ShapeDtypeStruct((t, ch), BF16),
        scratch_shapes=[pltpu.VMEM((HALO + tm, ch), F32),
                        pltpu.VMEM((SUBLANES - 1, HALO + tm - SUBLANES, ch), F32),
                        pltpu.VMEM((tm, ch), F32)],
        compiler_params=_params(("arbitrary",)),
        name="dwconv_ln",
    )(c, w, b, g, beta)


def _outproj_kernel(m_ref, w_ref, x_ref, g_ref, b_ref, h_ref):
    y = _dot(m_ref[...], w_ref[...])
    h_ref[...] = _layer_norm(DN_ALPHA * x_ref[...] + y, g_ref[...], b_ref[...])


def _outproj_call(mixed, w16, x, g, b):
    t, d = x.shape
    tm = 256
    vec = pl.BlockSpec((1, d), lambda i: (0, 0))
    return pl.pallas_call(
        _outproj_kernel,
        grid=(t // tm,),
        in_specs=[pl.BlockSpec((tm, d), lambda i: (i, 0)), pl.BlockSpec((d, d), lambda i: (0, 0)),
                  pl.BlockSpec((tm, d), lambda i: (i, 0)), vec, vec],
        out_specs=pl.BlockSpec((tm, d), lambda i: (i, 0)),
        out_shape=jax.ShapeDtypeStruct((t, d), F32),
        compiler_params=_params(("parallel",)),
        name="outproj_ln1",
    )(mixed, w16, x, g, b)


def _route_kernel(h_ref, w_ref, bias_ref, idx_ref, wts_ref, rank_ref, cnt_ref, carry):
    @pl.when(pl.program_id(0) == 0)
    def _():
        carry[...] = jnp.zeros(carry.shape, F32)

    logits = jnp.dot(h_ref[...], w_ref[...], precision=lax.Precision.HIGHEST,
                     preferred_element_type=F32)
    scores = _sigmoid(logits)
    biased = scores + bias_ref[...]
    shape = biased.shape
    tm = shape[0]
    lane_i = lax.broadcasted_iota(I32, shape, 1)
    lane = lane_i.astype(F32)
    per_group = N_EXPERTS // N_GROUPS
    grp = (lane_i // per_group).astype(F32)
    neg = -jnp.inf
    big = 1e9

    def rmax(x):
        return jnp.max(x, axis=1, keepdims=True)

    def rmin(x):
        return jnp.min(x, axis=1, keepdims=True)

    def rsum(x):
        return jnp.sum(x, axis=1, keepdims=True)

    gs = jnp.zeros(shape, F32)
    for g in range(N_GROUPS):
        in_g = grp == float(g)
        m = jnp.where(in_g, biased, neg)
        m1 = rmax(m)
        i1 = rmin(jnp.where(m == m1, lane, big))
        m2 = rmax(jnp.where(lane == i1, neg, m))
        gs = jnp.where(in_g, m1 + m2, gs)
    sel_g = jnp.zeros(shape, jnp.bool_)
    cur = gs
    for _ in range(TOPK_GROUPS):
        mx = rmax(cur)
        gi = rmin(jnp.where(cur == mx, grp, big))
        hit = grp == gi
        sel_g = sel_g | hit
        cur = jnp.where(hit, neg, cur)
    masked = jnp.where(sel_g, biased, neg)
    ids, ws = [], []
    sel = jnp.zeros(shape, F32)
    for _ in range(TOP_K):
        mx = rmax(masked)
        ik = rmin(jnp.where(masked == mx, lane, big))
        hit = lane == ik
        ws.append(rsum(jnp.where(hit, scores, 0.0)))
        ids.append(ik)
        sel = jnp.where(hit, 1.0, sel)
        masked = jnp.where(hit, neg, masked)
    w = jnp.concatenate(ws, axis=1)
    wts_ref[...] = w / rsum(w) * ROUTE_SCALE
    idx_ref[...] = jnp.concatenate(ids, axis=1).astype(I32)

    earlier = (lax.broadcasted_iota(I32, (tm, tm), 1) < lax.broadcasted_iota(I32, (tm, tm), 0))
    before = _dot(earlier.astype(BF16), sel.astype(BF16)) + carry[...]
    ranks = [rsum(jnp.where(lane == ik, before, 0.0)) for ik in ids]
    rank_ref[...] = jnp.concatenate(ranks, axis=1).astype(I32)
    carry[...] = carry[...] + jnp.sum(sel, axis=0, keepdims=True)
    cnt_ref[...] = carry[...]


def _route_call(h, w_router, bias):
    t, d = h.shape
    tm = ROUTE_TM
    tok = pl.BlockSpec((tm, TOP_K), lambda i: (i, 0))
    return pl.pallas_call(
        _route_kernel,
        grid=(t // tm,),
        in_specs=[pl.BlockSpec((tm, d), lambda i: (i, 0)),
                  pl.BlockSpec((d, N_EXPERTS), lambda i: (0, 0)),
                  pl.BlockSpec((1, N_EXPERTS), lambda i: (0, 0))],
        out_specs=(tok, tok, tok, pl.BlockSpec((1, N_EXPERTS), lambda i: (0, 0))),
        out_shape=(jax.ShapeDtypeStruct((t, TOP_K), I32), jax.ShapeDtypeStruct((t, TOP_K), F32),
                   jax.ShapeDtypeStruct((t, TOP_K), I32), jax.ShapeDtypeStruct((1, N_EXPERTS), F32)),
        scratch_shapes=[pltpu.VMEM((1, N_EXPERTS), F32)],
        compiler_params=_params(("arbitrary",)),
        name="route",
    )(h, w_router, bias)


def _dest_kernel(idx_ref, rank_ref, start_ref, dest_ref):
    idx = idx_ref[...]
    shape = (idx.shape[0], N_EXPERTS)
    lane = lax.broadcasted_iota(I32, shape, 1)
    start = start_ref[...]
    cols = []
    for kk in range(TOP_K):
        hit = lane == idx[:, kk:kk + 1]
        cols.append(jnp.sum(jnp.where(hit, start, 0.0), axis=1, keepdims=True))
    dest_ref[...] = jnp.concatenate(cols, axis=1).astype(I32) + rank_ref[...]


def _dest_call(idx, rank, start_f32):
    t = idx.shape[0]
    tm = 512
    tok = pl.BlockSpec((tm, TOP_K), lambda i: (i, 0))
    return pl.pallas_call(
        _dest_kernel,
        grid=(t // tm,),
        in_specs=[tok, tok, pl.BlockSpec((1, N_EXPERTS), lambda i: (0, 0))],
        out_specs=tok,
        out_shape=jax.ShapeDtypeStruct((t, TOP_K), I32),
        compiler_params=_params(("parallel",)),
        name="dest",
    )(idx, rank, start_f32)


def _row_copy(src, src_row, dst, dst_row, sem):
    return pltpu.make_async_copy(src.at[pl.ds(src_row, 1), :], dst.at[pl.ds(dst_row, 1), :], sem)


def _pack_halves(x):
    half = x.shape[1] // 2
    lo = lax.bitcast_convert_type(x[:, :half].astype(BF16).astype(F32), U32)
    hi = lax.bitcast_convert_type(x[:, half:].astype(BF16).astype(F32), U32)
    return hi | (lo >> 16)


def _unpack_halves(w):
    lo = lax.bitcast_convert_type(w << 16, F32)
    hi = lax.bitcast_convert_type(w & jnp.uint32(0xFFFF0000), F32)
    return lo, hi


def _rows_wait(src, dst, dst_row, n_rows, sem):
    pltpu.make_async_copy(src.at[pl.ds(0, n_rows), :], dst.at[pl.ds(dst_row, n_rows), :], sem).wait()


def _scatter_kernel(dest_ref, h_ref, xs_hbm, packed, sem):
    packed[...] = _pack_halves(h_ref[...])

    def issue(tok, c):
        for kk in range(TOP_K):
            _row_copy(packed, tok, xs_hbm, dest_ref[0, 0, tok * TOP_K + kk], sem.at[tok]).start(priority=kk % 2)
        return c

    lax.fori_loop(0, SCATTER_TM, issue, 0, unroll=2)

    def wait(tok, c):
        _rows_wait(packed, xs_hbm, 0, TOP_K, sem.at[tok])
        return c

    lax.fori_loop(0, SCATTER_TM, wait, 0, unroll=8)


def _scatter_call(dest, h):
    t, d = h.shape
    tm = SCATTER_TM
    n = tm * TOP_K
    return pl.pallas_call(
        _scatter_kernel,
        grid=(t // tm,),
        in_specs=[pl.BlockSpec((1, 1, n), lambda i: (i, 0, 0), memory_space=pltpu.SMEM),
                  pl.BlockSpec((tm, d), lambda i: (i, 0))],
        out_specs=pl.BlockSpec(memory_space=pl.ANY),
        out_shape=jax.ShapeDtypeStruct((t * TOP_K, d // 2), U32),
        scratch_shapes=[pltpu.VMEM((tm, d // 2), U32), pltpu.SemaphoreType.DMA((tm,))],
        compiler_params=_params(("arbitrary",)),
        name="dispatch",
    )(dest.reshape(t // tm, 1, n), h)


def _weight_copies(w_hbm, wgu32, wd32, wsem, e, slot):
    wg_hbm, wu_hbm, wd_hbm = w_hbm
    return [pltpu.make_async_copy(wg_hbm.at[e], wgu32.at[slot, :, 0:D_EXPERT], wsem.at[slot, 0]),
            pltpu.make_async_copy(wu_hbm.at[e], wgu32.at[slot, :, D_EXPERT:2 * D_EXPERT], wsem.at[slot, 1]),
            pltpu.make_async_copy(wd_hbm.at[e], wd32.at[slot], wsem.at[slot, 2])]


def _expert_kernel(vb_ref, ve_ref, lo_ref, hi_ref, fb_ref, fe_ref, slot_ref, nxt1_ref, nxt2_ref, nv_ref,
                   x_ref, wg_hbm, wu_hbm, wd_hbm, y_ref, wgu32, wd32, wsem):
    v = pl.program_id(0)
    w_hbm = (wg_hbm, wu_hbm, wd_hbm)

    def start(e, slot):
        for cp in _weight_copies(w_hbm, wgu32, wd32, wsem, e, slot):
            cp.start(priority=WEIGHT_DMA_QUEUE)

    @pl.when(v < nv_ref[0])
    def _():
        slot = slot_ref[v]

        @pl.when(fe_ref[v] == 1)
        def _():
            @pl.when(v == 0)
            def _():
                start(ve_ref[v], slot)

                @pl.when(nxt1_ref[v] >= 0)
                def _():
                    start(nxt1_ref[v], (slot + 1) % WEIGHT_SLOTS)

            @pl.when(nxt2_ref[v] >= 0)
            def _():
                start(nxt2_ref[v], (slot + 2) % WEIGHT_SLOTS)

            for cp in _weight_copies(w_hbm, wgu32, wd32, wsem, ve_ref[v], slot):
                cp.wait()

        x_lo, x_hi = _unpack_halves(x_ref[...])
        x = jnp.concatenate([x_lo.astype(BF16), x_hi.astype(BF16)], axis=1)
        gu = _dot(x, wgu32[slot])
        act = _silu(gu[:, :D_EXPERT]) * gu[:, D_EXPERT:]
        y = _pack_halves(_dot(act.astype(BF16), wd32[slot]))
        rows = lax.broadcasted_iota(I32, y.shape, 0)
        mine = (rows >= lo_ref[v]) & (rows < hi_ref[v])

        @pl.when(fb_ref[v] == 1)
        def _():
            y_ref[...] = jnp.where(mine, y, jnp.uint32(0))

        @pl.when(fb_ref[v] == 0)
        def _():
            y_ref[...] = jnp.where(mine, y, y_ref[...])


def _expert_call(tables, xs, w_gate, w_up, w_down):
    n_rows, dp = xs.shape
    d = 2 * dp
    blk = EXPERT_BLOCK
    n_visits = tables[0].shape[0]
    any_spec = pl.BlockSpec(memory_space=pl.ANY)
    grid_spec = pltpu.PrefetchScalarGridSpec(
        num_scalar_prefetch=len(tables),
        grid=(n_visits,),
        in_specs=[pl.BlockSpec((blk, dp), lambda v, vb, *_: (vb[v], 0)), any_spec, any_spec, any_spec],
        out_specs=pl.BlockSpec((blk, dp), lambda v, vb, *_: (vb[v], 0)),
        scratch_shapes=[
            pltpu.VMEM((WEIGHT_SLOTS, d, 2 * D_EXPERT), F32),
            pltpu.VMEM((WEIGHT_SLOTS, D_EXPERT, d), F32),
            pltpu.SemaphoreType.DMA((WEIGHT_SLOTS, 3)),
        ],
    )
    return pl.pallas_call(
        _expert_kernel,
        grid_spec=grid_spec,
        out_shape=jax.ShapeDtypeStruct((n_rows, dp), U32),
        compiler_params=_params(("arbitrary",)),
        name="experts",
    )(*tables, xs, w_gate, w_up, w_down)


def _visit_tables(counts, n_rows):
    blk = EXPERT_BLOCK
    n_visits = n_rows // blk + N_EXPERTS - 1
    end = jnp.cumsum(counts)
    start = end - counts
    nonempty = counts > 0
    first_blk = start // blk
    last_blk = jnp.maximum(end - 1, 0) // blk
    nvis = jnp.where(nonempty, last_blk - first_blk + 1, 0)
    vis_end = jnp.cumsum(nvis)
    vis_start = vis_end - nvis
    n_used = vis_end[-1]
    ids = jnp.arange(N_EXPERTS, dtype=I32)
    ordinal = jnp.cumsum(nonempty.astype(I32)) - 1
    nxt_incl = lax.cummin(jnp.where(nonempty, ids, N_EXPERTS), axis=0, reverse=True)
    nxt_e = jnp.concatenate([nxt_incl[1:], jnp.full((1,), N_EXPERTS, I32)])
    nxt_e = jnp.where(nxt_e >= N_EXPERTS, -1, nxt_e)
    nxt2_e = jnp.sum(jnp.where(nxt_e[:, None] == ids[None, :], nxt_e[None, :], 0), axis=1)
    nxt2_e = jnp.where(nxt_e >= 0, nxt2_e, -1)
    v = jnp.arange(n_visits, dtype=I32)
    vc = jnp.clip(v, 0, jnp.maximum(n_used - 1, 0))
    ve = jnp.minimum(jnp.sum((vis_end[None, :] <= vc[:, None]).astype(I32), axis=1), N_EXPERTS - 1)
    onehot = ve[:, None] == ids[None, :]
    look = lambda tab: jnp.sum(jnp.where(onehot, tab.astype(I32)[None, :], 0), axis=1)
    v_first = look(vis_start)
    vb = look(first_blk) + (vc - v_first)
    lo = jnp.clip(look(start) - vb * blk, 0, blk)
    hi = jnp.clip(look(end) - vb * blk, 0, blk)
    fb = jnp.concatenate([jnp.ones((1,), I32), (vb[1:] != vb[:-1]).astype(I32)])
    fe = (vc == v_first).astype(I32)
    tables = (vb, ve, lo, hi, fb, fe, look(ordinal % WEIGHT_SLOTS), look(nxt_e), look(nxt2_e),
              n_used.astype(I32).reshape(1))
    return tables, start


def _combine_kernel(dest_ref, dnext_ref, ys_hbm, wts_ref, h_ref, wg_ref, wu_ref, wd_ref, g_ref, b_ref,
                    o_ref, buf, sem):
    tm = COMBINE_TM
    i = pl.program_id(0)
    slot = i % 2

    def issue(d_ref, s, kk, r):
        a = kk * tm + r
        _row_copy(ys_hbm, d_ref[0, 0, a], buf.at[s], a, sem.at[s, kk]).start(priority=kk % 2)

    @pl.when(i == 0)
    def _():
        def body(r, c):
            for kk in range(TOP_K):
                issue(dest_ref, slot, kk, r)
            return c

        lax.fori_loop(0, tm, body, 0, unroll=2)

    h = h_ref[...]
    x = h.astype(BF16)
    act = _silu(_dot(x, wg_ref[...])) * _dot(x, wu_ref[...])
    acc = DN_ALPHA * h + _dot(act.astype(BF16), wd_ref[...])
    half = acc.shape[1] // 2
    acc_lo, acc_hi = acc[:, :half], acc[:, half:]
    wts = wts_ref[...]
    for kk in range(TOP_K):
        for r in range(tm):
            issue(dnext_ref, 1 - slot, kk, r)
        _rows_wait(ys_hbm, buf.at[slot], kk * tm, tm, sem.at[slot, kk])
        y_lo, y_hi = _unpack_halves(buf[slot, kk * tm:(kk + 1) * tm, :])
        acc_lo = acc_lo + wts[:, kk:kk + 1] * y_lo
        acc_hi = acc_hi + wts[:, kk:kk + 1] * y_hi
    o_ref[...] = _layer_norm(jnp.concatenate([acc_lo, acc_hi], axis=1), g_ref[...], b_ref[...])

    @pl.when(i == pl.num_programs(0) - 1)
    def _():
        for kk in range(TOP_K):
            _rows_wait(ys_hbm, buf.at[1 - slot], kk * tm, tm, sem.at[1 - slot, kk])


def _combine_call(dest_km, ys, wts, h, w_sh_gate, w_sh_up, w_sh_down, g, b):
    t, d = h.shape
    ds = w_sh_gate.shape[1]
    tm = COMBINE_TM
    n = tm * TOP_K
    n_tiles = t // tm
    vec = pl.BlockSpec((1, d), lambda i: (0, 0))
    return pl.pallas_call(
        _combine_kernel,
        grid=(n_tiles,),
        in_specs=[
            pl.BlockSpec((1, 1, n), lambda i: (i, 0, 0), memory_space=pltpu.SMEM),
            pl.BlockSpec((1, 1, n), lambda i: (jnp.minimum(i + 1, n_tiles - 1), 0, 0),
                         memory_space=pltpu.SMEM),
            pl.BlockSpec(memory_space=pl.ANY),
            pl.BlockSpec((tm, TOP_K), lambda i: (i, 0)),
            pl.BlockSpec((tm, d), lambda i: (i, 0)),
            pl.BlockSpec((d, ds), lambda i: (0, 0)),
            pl.BlockSpec((d, ds), lambda i: (0, 0)),
            pl.BlockSpec((ds, d), lambda i: (0, 0)),
            vec, vec,
        ],
        out_specs=pl.BlockSpec((tm, d), lambda i: (i, 0)),
        out_shape=jax.ShapeDtypeStruct((t, d), F32),
        scratch_shapes=[pltpu.VMEM((2, n, d // 2), U32), pltpu.SemaphoreType.DMA((2, TOP_K))],
        compiler_params=_params(("arbitrary",)),
        name="combine_ln2",
    )(dest_km, dest_km, ys, wts, h, w_sh_gate, w_sh_up, w_sh_down, g, b)


def _token_mixing(x, w_in, w_short_conv, a_log, dt_bias, delta_norm_w, w_o_delta, conv_dw_w,
                  conv_dw_b, conv_ln_g, conv_ln_b, w_pw2, b_pw2, w_out, ln1_g, ln1_b):
    t, d = x.shape
    x16 = x.astype(BF16)
    w_t = w_in.T
    row = lambda a: a.reshape(1, -1)
    tile = pl.BlockSpec((TM, TN), lambda j, i: (i, j))
    wide = pl.BlockSpec((TM_WIDE, TN), lambda j, i: (i, j))
    conv_buf = pltpu.VMEM((TM_WIDE + SUBLANES, TN), F32)

    scale = jnp.concatenate([jnp.full((1, QK_WIDTH), HEAD ** -0.5, F32), jnp.ones((1, QK_WIDTH), F32)], axis=1)
    qk = _proj_call(
        "proj_qk", x16, [w_t], [0], 2 * QK_WIDTH // TN, _qk_epilogue,
        [w_short_conv, scale],
        [pl.BlockSpec((SHORT_CONV, TN), lambda j, i: (0, j)), pl.BlockSpec((1, TN), lambda j, i: (0, j))],
        jax.ShapeDtypeStruct((t, 2 * QK_WIDTH), F32), wide, scratch=[conv_buf], tm=TM_WIDE, transposed=True)
    v_off = 2 * QK_WIDTH // TN
    v = _proj_call(
        "proj_v", x16, [w_t], [2 * QK_WIDTH], V_WIDTH // TN, _v_epilogue,
        [w_short_conv], [pl.BlockSpec((SHORT_CONV, TN), lambda j, i: (0, j + v_off))],
        jax.ShapeDtypeStruct((t, V_WIDTH), F32), wide, scratch=[conv_buf], tm=TM_WIDE, transposed=True)
    z = _proj_call(
        "proj_z", x16, [w_t], [COL_Z], V_WIDTH // TN, _z_epilogue, [], [],
        jax.ShapeDtypeStruct((t, V_WIDTH), F32), wide, tm=TM_WIDE, transposed=True)

    pad_h = lambda a: jnp.pad(a.reshape(1, -1), ((0, 0), (0, LANES - N_V_HEADS)))
    lane_vec = pl.BlockSpec((1, LANES), lambda j, i: (0, 0))
    lane_tile = pl.BlockSpec((TM, LANES), lambda j, i: (i, 0))
    lane_shape = jax.ShapeDtypeStruct((t, LANES), F32)
    gc, bt, gt = _proj_call(
        "proj_ab", x16, [w_t], [COL_AB], 1, _ab_epilogue, [pad_h(a_log), pad_h(dt_bias)],
        [lane_vec, lane_vec], (lane_shape, lane_shape, lane_shape), (lane_tile, lane_tile, lane_tile),
        tn=LANES, transposed=True)

    c = _proj_call(
        "proj_glu", x16, [w_t, w_t], [COL_GLU, COL_GLU + d], d // TN, _glu_epilogue, [], [],
        jax.ShapeDtypeStruct((t, d), F32), wide, tm=TM_WIDE, transposed=True)
    gates = _proj_call(
        "proj_gates", x16, [w_t], [COL_GATES], 2 * d // TN, _gates_epilogue, [], [],
        jax.ShapeDtypeStruct((t, 2 * d), F32), wide, tm=TM_WIDE, transposed=True)

    r = GDN_ROWS
    gcrow = gc[:, :N_V_HEADS].T.reshape(N_V_HEADS, t // r, 1, r)
    btrow = bt[:, N_V_HEADS:2 * N_V_HEADS].T.reshape(N_V_HEADS, t // r, 1, r)
    gt_chunk = gt[CHUNK - 1::CHUNK, :N_V_HEADS].T
    gtrow = jnp.broadcast_to(gt_chunk[:, :, None], (N_V_HEADS, t // CHUNK, LANES))
    gtrow = gtrow.reshape(N_V_HEADS, t // r, r // CHUNK, LANES)
    og = _gdn_call(qk, v, z, gc, bt, gt, gcrow, btrow, gtrow, row(delta_norm_w))

    ya = _proj_call(
        "proj_odelta", og, [w_o_delta], [0], d // TN, _odelta_epilogue, [gates], [tile],
        jax.ShapeDtypeStruct((t, d), F32), tile)

    c_act = _dwconv_call(c, conv_dw_w, row(conv_dw_b), row(conv_ln_g), row(conv_ln_b))
    g_off = d // TN
    mixed = _proj_call(
        "proj_pw2", c_act, [w_pw2], [0], d // TN, _pw2_epilogue, [row(b_pw2), gates, ya],
        [pl.BlockSpec((1, TN), lambda j, i: (0, j)),
         pl.BlockSpec((TM_WIDE, TN), lambda j, i: (i, j + g_off)), wide],
        jax.ShapeDtypeStruct((t, d), BF16), wide, tm=TM_WIDE)
    return _outproj_call(mixed, w_out.astype(BF16), x, row(ln1_g), row(ln1_b))


def _moe(h, w_router, router_bias, w_gate, w_up, w_down, w_sh_gate, w_sh_up, w_sh_down, ln2_g, ln2_b):
    t, d = h.shape
    idx, wts, rank, counts = _route_call(h, w_router, router_bias.reshape(1, -1))
    tables, start = _visit_tables(counts.reshape(-1).astype(I32), t * TOP_K)
    dest = _dest_call(idx, rank, start.astype(F32).reshape(1, -1))
    xs = _scatter_call(dest, h)
    ys = _expert_call(tables, xs, w_gate, w_up, w_down)
    tm = COMBINE_TM
    dest_km = dest.reshape(t // tm, tm, TOP_K).transpose(0, 2, 1).reshape(t // tm, 1, tm * TOP_K)
    return _combine_call(dest_km, ys, wts, h, w_sh_gate, w_sh_up, w_sh_down,
                         ln2_g.reshape(1, -1), ln2_b.reshape(1, -1))


def kernel(x, w_in, w_short_conv, a_log, dt_bias, delta_norm_w, w_o_delta, conv_dw_w, conv_dw_b,
           conv_ln_g, conv_ln_b, w_pw2, b_pw2, w_out, ln1_g, ln1_b, w_router, router_bias, w_gate,
           w_up, w_down, w_sh_gate, w_sh_up, w_sh_down, ln2_g, ln2_b):
    batch, seq, d = x.shape
    depth = w_in.shape[0]
    outs = []
    for bi in range(batch):
        h = x[bi]
        for li in range(depth):
            h = _token_mixing(h, w_in[li], w_short_conv[li], a_log[li], dt_bias[li], delta_norm_w[li],
                              w_o_delta[li], conv_dw_w[li], conv_dw_b[li], conv_ln_g[li], conv_ln_b[li],
                              w_pw2[li], b_pw2[li], w_out[li], ln1_g[li], ln1_b[li])
            h = _moe(h, w_router[li], router_bias[li], w_gate[li], w_up[li], w_down[li],
                     w_sh_gate[li], w_sh_up[li], w_sh_down[li], ln2_g[li], ln2_b[li])
        outs.append(h)
    return jnp.stack(outs, axis=0)
```

```python
import jax
import jax.numpy as jnp
from jax import lax
from jax.experimental import pallas as pl
from jax.experimental.pallas import tpu as pltpu

F32 = jnp.float32
BF16 = jnp.bfloat16
I32 = jnp.int32
U32 = jnp.uint32

D_MODEL = 2048
CHUNK = 64
N_QK_HEADS = 16
N_V_HEADS = 32
HEAD = 128
QK_WIDTH = N_QK_HEADS * HEAD
V_WIDTH = N_V_HEADS * HEAD
SHORT_CONV = 4
CONV_WIDTH = 31
N_EXPERTS = 256
TOP_K = 8
N_GROUPS = 8
TOPK_GROUPS = 4
D_EXPERT = 512
ROUTE_SCALE = 2.5
DN_ALPHA = 2.0 ** 0.25
LN_EPS = 1e-5
NORM_EPS = 1e-6
COL_Z = 2 * QK_WIDTH + V_WIDTH
COL_AB = COL_Z + V_WIDTH
COL_GLU = COL_AB + 2 * N_V_HEADS
COL_GATES = COL_GLU + 2 * D_MODEL

LANES = 128
SUBLANES = 8
VMEM_LIMIT = 56 * 1024 * 1024
TM = 512
TM_WIDE = 1024
TN = 512
TM_X = 512
TN_X = 1024
GDN_ROWS = 256
GDN_PAIRS = 4
CONV_TM = 256
HALO = 32
CONV_RC = 64
CONV_CC = 256
ROW_ALIGN = 64
ROUTE_TM = 256
EXPERT_BLOCK = 128
SCATTER_TM = 128
COMBINE_TM = 64
WEIGHT_DMA_QUEUE = 1
WEIGHT_SLOTS = 3


def _params(sem, vmem=VMEM_LIMIT):
    return pltpu.CompilerParams(dimension_semantics=sem, vmem_limit_bytes=vmem)


def _sigmoid(x):
    return jax.nn.sigmoid(x)


def _silu(x):
    return x * jax.nn.sigmoid(x)


def _softplus(x):
    return jnp.maximum(x, 0.0) + jnp.log1p(jnp.exp(-jnp.abs(x)))


def _dot(a, b):
    return jnp.dot(a, b, preferred_element_type=F32)


def _proj_call(name, x, ws, w_offs, n_tiles, epilogue, extras, extra_specs, out_shapes, out_specs,
               scratch=(), tm=TM, tn=TN, transposed=False):
    m, k = x.shape
    single_out = not isinstance(out_shapes, (tuple, list))
    if single_out:
        out_shapes, out_specs = (out_shapes,), (out_specs,)
    nw, ne, no = len(ws), len(extras), len(out_shapes)
    needs_cast = [w.dtype != BF16 for w in ws]
    contract = (((1,), (1,)), ((), ())) if transposed else (((1,), (0,)), ((), ()))

    def body(*refs):
        x_ref = refs[0]
        w_refs = refs[1:1 + nw]
        ex_refs = refs[1 + nw:1 + nw + ne]
        out_refs = refs[1 + nw + ne:1 + nw + ne + no]
        scr = refs[1 + nw + ne + no:]
        wb_refs = scr[:sum(needs_cast)]
        rest = scr[sum(needs_cast):]
        i = pl.program_id(1)
        wsrc, c = [], 0
        for kk in range(nw):
            if needs_cast[kk]:
                wsrc.append(wb_refs[c])
                c += 1
            else:
                wsrc.append(w_refs[kk])

        @pl.when(i == 0)
        def _():
            cc = 0
            for kk in range(nw):
                if needs_cast[kk]:
                    wb_refs[cc][...] = w_refs[kk][...].astype(BF16)
                    cc += 1

        xv = x_ref[...]
        accs = [lax.dot_general(xv, wr[...], contract, preferred_element_type=F32) for wr in wsrc]
        epilogue(i, accs, ex_refs, out_refs, rest)

    in_specs = [pl.BlockSpec((tm, k), lambda j, i: (i, 0))]
    for off in w_offs:
        if transposed:
            in_specs.append(pl.BlockSpec(
                (pl.Element(tn), pl.Element(k)),
                lambda j, i, off=off: (pl.multiple_of(off + tn * j, ROW_ALIGN), 0)))
        else:
            in_specs.append(pl.BlockSpec((k, tn), lambda j, i, off=off: (0, j + off)))
    in_specs += list(extra_specs)
    w_tile = (tn, k) if transposed else (k, tn)
    scratch_shapes = [pltpu.VMEM(w_tile, BF16) for c in needs_cast if c] + list(scratch)
    res = pl.pallas_call(
        body,
        grid=(n_tiles, m // tm),
        in_specs=in_specs,
        out_specs=tuple(out_specs),
        out_shape=tuple(out_shapes),
        scratch_shapes=scratch_shapes,
        compiler_params=_params(("parallel", "arbitrary")),
        name=name,
    )(x, *ws, *extras)
    return res[0] if single_out else res


def _short_conv_silu(i, acc, cw_ref, buf, tm):
    @pl.when(i == 0)
    def _():
        buf[0:SUBLANES, :] = jnp.zeros((SUBLANES, buf.shape[1]), F32)

    buf[SUBLANES:SUBLANES + tm, :] = acc
    cw = cw_ref[...]
    y = acc * cw[SHORT_CONV - 1:SHORT_CONV, :]
    for s in range(SHORT_CONV - 1):
        y = y + buf[pl.ds(SUBLANES - (SHORT_CONV - 1) + s, tm), :] * cw[s:s + 1, :]
    buf[0:SUBLANES, :] = buf[tm:tm + SUBLANES, :]
    return _silu(y)


def _qk_epilogue(i, accs, ex, outs, scr):
    cw_ref, scale_ref = ex
    (buf,) = scr
    y = _short_conv_silu(i, accs[0], cw_ref, buf, accs[0].shape[0])
    parts = []
    for g in range(y.shape[1] // HEAD):
        yg = y[:, g * HEAD:(g + 1) * HEAD]
        parts.append(yg * lax.rsqrt(jnp.sum(yg * yg, axis=-1, keepdims=True) + NORM_EPS))
    outs[0][...] = jnp.concatenate(parts, axis=1) * scale_ref[...]


def _v_epilogue(i, accs, ex, outs, scr):
    (cw_ref,) = ex
    (buf,) = scr
    outs[0][...] = _short_conv_silu(i, accs[0], cw_ref, buf, accs[0].shape[0])


def _z_epilogue(i, accs, ex, outs, scr):
    outs[0][...] = accs[0]


def _glu_epilogue(i, accs, ex, outs, scr):
    outs[0][...] = accs[0] * _sigmoid(accs[1])


def _gates_epilogue(i, accs, ex, outs, scr):
    outs[0][...] = _sigmoid(accs[0])


def _ab_epilogue(i, accs, ex, outs, scr):
    alog_ref, dtb_ref = ex
    gc_ref, bt_ref, gt_ref = outs
    acc = accs[0]
    g = -jnp.exp(alog_ref[...]) * _softplus(acc + dtb_ref[...])
    row = lax.broadcasted_iota(I32, g.shape, 0) % CHUNK
    s = 1
    while s < CHUNK:
        g = g + jnp.where(row >= s, pltpu.roll(g, s, axis=0), 0.0)
        s *= 2
    n_chunks = g.shape[0] // CHUNK
    tot = g.reshape(n_chunks, CHUNK, LANES)[:, CHUNK - 1:CHUNK, :]
    gc_ref[...] = g
    bt_ref[...] = _sigmoid(acc)
    gt_ref[...] = jnp.broadcast_to(tot, (n_chunks, CHUNK, LANES)).reshape(g.shape)


def _odelta_epilogue(i, accs, ex, outs, scr):
    (gate_ref,) = ex
    outs[0][...] = accs[0] * gate_ref[...]


def _pw2_epilogue(i, accs, ex, outs, scr):
    bias_ref, gate_ref, ya_ref = ex
    outs[0][...] = (ya_ref[...] + gate_ref[...] * (accs[0] + bias_ref[...])).astype(BF16)


class _Head:
    pass


def _gdn_kernel(q_ref, k_ref, v_ref, z_ref, gc_ref, bt_ref, gt_ref, gcrow_ref, btrow_ref, gtrow_ref,
                nw_ref, o_ref, s_ref):
    pp = pl.program_id(0)
    tb = pl.program_id(1)
    r = GDN_ROWS
    half = r // 2
    n_heads = 2 * GDN_PAIRS

    @pl.when(tb == 0)
    def _():
        s_ref[...] = jnp.zeros(s_ref.shape, F32)

    col_tok = lax.broadcasted_iota(I32, (half, r), 0)
    row_tok = lax.broadcasted_iota(I32, (half, r), 1)
    first_half = row_tok < half
    row_in_half = row_tok % half
    same = (row_in_half // CHUNK) == (col_tok // CHUNK)
    causal = same & (col_tok <= row_in_half)
    strict = same & (col_tok < row_in_half)
    lane = lax.broadcasted_iota(I32, (r, LANES), 1)
    gc_all = gc_ref[...]
    bt_all = bt_ref[...]
    gt_all = gt_ref[...]
    nt = (((1,), (1,)), ((), ()))
    tn = (((0,), (0,)), ((), ()))

    def transpose_bd(xc):
        return jnp.concatenate([jnp.where(first_half, xc, 0.0), jnp.where(first_half, 0.0, xc)], axis=0)

    heads = [_Head() for _ in range(n_heads)]
    for pi in range(GDN_PAIRS):
        q = q_ref[:, pi * HEAD:(pi + 1) * HEAD]
        k = k_ref[:, pi * HEAD:(pi + 1) * HEAD]
        k16 = k.astype(BF16)
        q16 = q.astype(BF16)
        kq = [lax.dot_general(k16[rs], jnp.concatenate([k16[rs], q16[rs]], axis=0), nt,
                              preferred_element_type=F32) for rs in (slice(0, half), slice(half, r))]
        kk = jnp.concatenate([kq[0][:, :half], kq[1][:, :half]], axis=1)
        qk = jnp.concatenate([kq[0][:, half:], kq[1][:, half:]], axis=1)
        for hd in heads[2 * pi:2 * pi + 2]:
            hd.q, hd.k, hd.kk, hd.qk = q, k, kk, qk

    for hl, hd in enumerate(heads):
        h = n_heads * pp + hl
        hd.cols = slice(hl * HEAD, (hl + 1) * HEAD)
        hd.gcol = jnp.sum(jnp.where(lane == h, gc_all, 0.0), axis=1, keepdims=True)
        hd.bcol = jnp.sum(jnp.where(lane == h + N_V_HEADS, bt_all, 0.0), axis=1, keepdims=True)
        hd.gtcol = jnp.sum(jnp.where(lane == h, gt_all, 0.0), axis=1, keepdims=True)
        gc_col_tok = jnp.concatenate([jnp.broadcast_to(hd.gcol[:half], (half, half)),
                                      jnp.broadcast_to(hd.gcol[half:], (half, half))], axis=1)
        hd.dm = jnp.exp(jnp.where(causal, gcrow_ref[hl, 0] - gc_col_tok, -jnp.inf))
        hd.p = jnp.where(strict, btrow_ref[hl, 0] * hd.kk * hd.dm, 0.0)
        hd.n = -hd.p

    for hd in heads:
        hd.res = _dot(hd.p.astype(BF16), transpose_bd(hd.p).astype(BF16))
    for hd in heads:
        hd.p = hd.res
    for _ in range(4):
        for hd in heads:
            rhs = jnp.concatenate([transpose_bd(hd.p), transpose_bd(hd.n)], axis=1).astype(BF16)
            hd.res = _dot(hd.p.astype(BF16), rhs)
        for hd in heads:
            hd.n = hd.n + hd.p + hd.res[:, r:]
            hd.p = hd.res[:, :r]
    for hd in heads:
        hd.res = _dot(hd.p.astype(BF16), transpose_bd(hd.n).astype(BF16))
    for hd in heads:
        hd.n = hd.n + hd.p + hd.res

    for hd in heads:
        hd.n16 = transpose_bd(hd.n).T.astype(BF16)
        hd.eg = jnp.exp(hd.gcol)
        hd.rhs = jnp.concatenate([v_ref[:, hd.cols] * hd.bcol, hd.k * (hd.bcol * hd.eg)], axis=1)
    for hd in heads:
        hd.sol16 = (hd.rhs + _dot(hd.n16, hd.rhs.astype(BF16))).astype(BF16)
    for hd in heads:
        hd.qkd16 = transpose_bd(hd.qk * hd.dm).T.astype(BF16)
    for hd in heads:
        hd.x = _dot(hd.qkd16, hd.sol16)
    for hl, hd in enumerate(heads):
        hd.qp16 = (hd.q * hd.eg - hd.x[:, HEAD:]).astype(BF16)
        hd.kd16 = (hd.k * jnp.exp(hd.gtcol - hd.gcol)).astype(BF16)
        hd.state = s_ref[hl]
        hd.o = []

    for j in range(r // CHUNK):
        sl = slice(j * CHUNK, (j + 1) * CHUNK)
        for hd in heads:
            hd.kc = lax.dot_general(hd.kd16[sl], hd.sol16[sl], tn, preferred_element_type=F32)
        for hd in heads:
            lhs = jnp.concatenate([hd.kc[:, HEAD:].astype(BF16), hd.qp16[sl]], axis=0)
            hd.ks_qs = _dot(lhs, hd.state.astype(BF16))
        for hl, hd in enumerate(heads):
            hd.o.append(hd.ks_qs[HEAD:] + hd.x[sl, :HEAD])
            hd.state = (hd.state * jnp.exp(gtrow_ref[hl, 0, j:j + 1, :]) + hd.kc[:, :HEAD]
                        - hd.ks_qs[:HEAD])

    for hl, hd in enumerate(heads):
        s_ref[hl] = hd.state
        o = jnp.concatenate(hd.o, axis=0)
        o = o * lax.rsqrt(jnp.mean(o * o, axis=-1, keepdims=True) + NORM_EPS) * nw_ref[...]
        o_ref[:, hd.cols] = (o * _silu(z_ref[:, hd.cols])).astype(o_ref.dtype)


def _gdn_call(qk, v, z, gc, bt, gt, gcrow, btrow, gtrow, nw):
    t = qk.shape[0]
    r = GDN_ROWS
    p = GDN_PAIRS
    k_off = N_QK_HEADS // p
    return pl.pallas_call(
        _gdn_kernel,
        grid=(N_QK_HEADS // p, t // r),
        in_specs=[
            pl.BlockSpec((r, p * HEAD), lambda pp, tb: (tb, pp)),
            pl.BlockSpec((r, p * HEAD), lambda pp, tb: (tb, k_off + pp)),
            pl.BlockSpec((r, 2 * p * HEAD), lambda pp, tb: (tb, pp)),
            pl.BlockSpec((r, 2 * p * HEAD), lambda pp, tb: (tb, pp)),
            pl.BlockSpec((r, LANES), lambda pp, tb: (tb, 0)),
            pl.BlockSpec((r, LANES), lambda pp, tb: (tb, 0)),
            pl.BlockSpec((r, LANES), lambda pp, tb: (tb, 0)),
            pl.BlockSpec((2 * p, 1, 1, r), lambda pp, tb: (pp, tb, 0, 0)),
            pl.BlockSpec((2 * p, 1, 1, r), lambda pp, tb: (pp, tb, 0, 0)),
            pl.BlockSpec((2 * p, 1, r // CHUNK, LANES), lambda pp, tb: (pp, tb, 0, 0)),
            pl.BlockSpec((1, HEAD), lambda pp, tb: (0, 0)),
        ],
        out_specs=pl.BlockSpec((r, 2 * p * HEAD), lambda pp, tb: (tb, pp)),
        out_shape=jax.ShapeDtypeStruct((t, V_WIDTH), BF16),
        scratch_shapes=[pltpu.VMEM((2 * p, HEAD, HEAD), F32)],
        compiler_params=_params(("parallel", "arbitrary")),
        name="gdn",
    )(qk, qk, v, z, gc, bt, gt, gcrow, btrow, gtrow, nw)


def _layer_norm(r, g, b):
    mu = jnp.mean(r, axis=-1, keepdims=True)
    rc = r - mu
    var = jnp.mean(rc * rc, axis=-1, keepdims=True)
    return rc * lax.rsqrt(var + LN_EPS) * g + b


def _dwconv_kernel(c_ref, w_ref, b_ref, g_ref, beta_ref, o_ref, buf, shifted, accbuf):
    i = pl.program_id(0)
    tm = CONV_TM
    ch = buf.shape[1]
    n_shift = HALO + tm - SUBLANES

    @pl.when(i == 0)
    def _():
        buf[0:HALO, :] = jnp.zeros((HALO, ch), F32)

    buf[HALO:HALO + tm, :] = c_ref[...]
    for b in range(1, SUBLANES):
        shifted[b - 1] = buf[pl.ds(b, n_shift), :]

    def row_body(rc, carry):
        r0 = pl.multiple_of(rc * CONV_RC, CONV_RC)
        for cc in range(ch // CONV_CC):
            cs = slice(cc * CONV_CC, (cc + 1) * CONV_CC)
            acc = buf[pl.ds(HALO + r0, CONV_RC), cs] * w_ref[CONV_WIDTH - 1:CONV_WIDTH, cs] + b_ref[:, cs]
            for s in range(CONV_WIDTH - 1):
                a, b = divmod(HALO - (CONV_WIDTH - 1) + s, SUBLANES)
                if b == 0:
                    src = buf[pl.ds(a * SUBLANES + r0, CONV_RC), cs]
                else:
                    src = shifted[b - 1, pl.ds(a * SUBLANES + r0, CONV_RC), cs]
                acc = acc + src * w_ref[s:s + 1, cs]
            accbuf[pl.ds(r0, CONV_RC), cs] = acc
        return carry

    lax.fori_loop(0, tm // CONV_RC, row_body, 0)
    buf[0:HALO, :] = buf[tm:tm + HALO, :]
    o_ref[...] = _silu(_layer_norm(accbuf[...], g_ref[...], beta_ref[...])).astype(o_ref.dtype)


def _dwconv_call(c, w, b, g, beta):
    t, ch = c.shape
    tm = CONV_TM
    vec = pl.BlockSpec((1, ch), lambda i: (0, 0))
    return pl.pallas_call(
        _dwconv_kernel,
        grid=(t // tm,),
        in_specs=[pl.BlockSpec((tm, ch), lambda i: (i, 0)),
                  pl.BlockSpec((CONV_WIDTH, ch), lambda i: (0, 0)), vec, vec, vec],
        out_specs=pl.BlockSpec((tm, ch), lambda i: (i, 0)),
        out_shape=jax.ShapeDtypeStruct((t, ch), BF16),
        scratch_shapes=[pltpu.VMEM((HALO + tm, ch), F32),
                        pltpu.VMEM((SUBLANES - 1, HALO + tm - SUBLANES, ch), F32),
                        pltpu.VMEM((tm, ch), F32)],
        compiler_params=_params(("arbitrary",)),
        name="dwconv_ln",
    )(c, w, b, g, beta)


def _outproj_kernel(m_ref, w_ref, x_ref, g_ref, b_ref, h_ref):
    y = _dot(m_ref[...], w_ref[...])
    h_ref[...] = _layer_norm(DN_ALPHA * x_ref[...] + y, g_ref[...], b_ref[...])


def _outproj_call(mixed, w16, x, g, b):
    t, d = x.shape
    tm = 256
    vec = pl.BlockSpec((1, d), lambda i: (0, 0))
    return pl.pallas_call(
        _outproj_kernel,
        grid=(t // tm,),
        in_specs=[pl.BlockSpec((tm, d), lambda i: (i, 0)), pl.BlockSpec((d, d), lambda i: (0, 0)),
                  pl.BlockSpec((tm, d), lambda i: (i, 0)), vec, vec],
        out_specs=pl.BlockSpec((tm, d), lambda i: (i, 0)),
        out_shape=jax.ShapeDtypeStruct((t, d), F32),
        compiler_params=_params(("parallel",)),
        name="outproj_ln1",
    )(mixed, w16, x, g, b)


def _route_kernel(h_ref, w_ref, bias_ref, idx_ref, wts_ref, rank_ref, cnt_ref, carry):
    @pl.when(pl.program_id(0) == 0)
    def _():
        carry[...] = jnp.zeros(carry.shape, F32)

    logits = jnp.dot(h_ref[...], w_ref[...], precision=lax.Precision.HIGHEST,
                     preferred_element_type=F32)
    scores = _sigmoid(logits)
    biased = scores + bias_ref[...]
    shape = biased.shape
    tm = shape[0]
    lane_i = lax.broadcasted_iota(I32, shape, 1)
    lane = lane_i.astype(F32)
    per_group = N_EXPERTS // N_GROUPS
    grp = (lane_i // per_group).astype(F32)
    neg = -jnp.inf
    big = 1e9

    def rmax(x):
        return jnp.max(x, axis=1, keepdims=True)

    def rmin(x):
        return jnp.min(x, axis=1, keepdims=True)

    def rsum(x):
        return jnp.sum(x, axis=1, keepdims=True)

    gs = jnp.zeros(shape, F32)
    for g in range(N_GROUPS):
        in_g = grp == float(g)
        m = jnp.where(in_g, biased, neg)
        m1 = rmax(m)
        i1 = rmin(jnp.where(m == m1, lane, big))
        m2 = rmax(jnp.where(lane == i1, neg, m))
        gs = jnp.where(in_g, m1 + m2, gs)
    sel_g = jnp.zeros(shape, jnp.bool_)
    cur = gs
    for _ in range(TOPK_GROUPS):
        mx = rmax(cur)
        gi = rmin(jnp.where(cur == mx, grp, big))
        hit = grp == gi
        sel_g = sel_g | hit
        cur = jnp.where(hit, neg, cur)
    masked = jnp.where(sel_g, biased, neg)
    ids, ws = [], []
    sel = jnp.zeros(shape, F32)
    for _ in range(TOP_K):
        mx = rmax(masked)
        ik = rmin(jnp.where(masked == mx, lane, big))
        hit = lane == ik
        ws.append(rsum(jnp.where(hit, scores, 0.0)))
        ids.append(ik)
        sel = jnp.where(hit, 1.0, sel)
        masked = jnp.where(hit, neg, masked)
    w = jnp.concatenate(ws, axis=1)
    wts_ref[...] = w / rsum(w) * ROUTE_SCALE
    idx_ref[...] = jnp.concatenate(ids, axis=1).astype(I32)

    earlier = (lax.broadcasted_iota(I32, (tm, tm), 1) < lax.broadcasted_iota(I32, (tm, tm), 0))
    before = _dot(earlier.astype(BF16), sel.astype(BF16)) + carry[...]
    ranks = [rsum(jnp.where(lane == ik, before, 0.0)) for ik in ids]
    rank_ref[...] = jnp.concatenate(ranks, axis=1).astype(I32)
    carry[...] = carry[...] + jnp.sum(sel, axis=0, keepdims=True)
    cnt_ref[...] = carry[...]


def _route_call(h, w_router, bias):
    t, d = h.shape
    tm = ROUTE_TM
    tok = pl.BlockSpec((tm, TOP_K), lambda i: (i, 0))
    return pl.pallas_call(
        _route_kernel,
        grid=(t // tm,),
        in_specs=[pl.BlockSpec((tm, d), lambda i: (i, 0)),
                  pl.BlockSpec((d, N_EXPERTS), lambda i: (0, 0)),
                  pl.BlockSpec((1, N_EXPERTS), lambda i: (0, 0))],
        out_specs=(tok, tok, tok, pl.BlockSpec((1, N_EXPERTS), lambda i: (0, 0))),
        out_shape=(jax.ShapeDtypeStruct((t, TOP_K), I32), jax.ShapeDtypeStruct((t, TOP_K), F32),
                   jax.ShapeDtypeStruct((t, TOP_K), I32), jax.ShapeDtypeStruct((1, N_EXPERTS), F32)),
        scratch_shapes=[pltpu.VMEM((1, N_EXPERTS), F32)],
        compiler_params=_params(("arbitrary",)),
        name="route",
    )(h, w_router, bias)


def _dest_kernel(idx_ref, rank_ref, start_ref, dest_ref):
    idx = idx_ref[...]
    shape = (idx.shape[0], N_EXPERTS)
    lane = lax.broadcasted_iota(I32, shape, 1)
    start = start_ref[...]
    cols = []
    for kk in range(TOP_K):
        hit = lane == idx[:, kk:kk + 1]
        cols.append(jnp.sum(jnp.where(hit, start, 0.0), axis=1, keepdims=True))
    dest_ref[...] = jnp.concatenate(cols, axis=1).astype(I32) + rank_ref[...]


def _dest_call(idx, rank, start_f32):
    t = idx.shape[0]
    tm = 512
    tok = pl.BlockSpec((tm, TOP_K), lambda i: (i, 0))
    return pl.pallas_call(
        _dest_kernel,
        grid=(t // tm,),
        in_specs=[tok, tok, pl.BlockSpec((1, N_EXPERTS), lambda i: (0, 0))],
        out_specs=tok,
        out_shape=jax.ShapeDtypeStruct((t, TOP_K), I32),
        compiler_params=_params(("parallel",)),
        name="dest",
    )(idx, rank, start_f32)


def _row_copy(src, src_row, dst, dst_row, sem):
    return pltpu.make_async_copy(src.at[pl.ds(src_row, 1), :], dst.at[pl.ds(dst_row, 1), :], sem)


def _pack_halves(x):
    half = x.shape[1] // 2
    lo = lax.bitcast_convert_type(x[:, :half].astype(BF16).astype(F32), U32)
    hi = lax.bitcast_convert_type(x[:, half:].astype(BF16).astype(F32), U32)
    return hi | (lo >> 16)


def _unpack_halves(w):
    lo = lax.bitcast_convert_type(w << 16, F32)
    hi = lax.bitcast_convert_type(w & jnp.uint32(0xFFFF0000), F32)
    return lo, hi


def _rows_wait(src, dst, dst_row, n_rows, sem):
    pltpu.make_async_copy(src.at[pl.ds(0, n_rows), :], dst.at[pl.ds(dst_row, n_rows), :], sem).wait()


def _scatter_kernel(dest_ref, h_ref, xs_hbm, packed, sem):
    packed[...] = _pack_halves(h_ref[...])

    def issue(tok, c):
        for kk in range(TOP_K):
            _row_copy(packed, tok, xs_hbm, dest_ref[0, 0, tok * TOP_K + kk], sem.at[tok]).start(priority=kk % 2)
        return c

    lax.fori_loop(0, SCATTER_TM, issue, 0, unroll=2)

    def wait(tok, c):
        _rows_wait(packed, xs_hbm, 0, TOP_K, sem.at[tok])
        return c

    lax.fori_loop(0, SCATTER_TM, wait, 0, unroll=8)


def _scatter_call(dest, h):
    t, d = h.shape
    tm = SCATTER_TM
    n = tm * TOP_K
    return pl.pallas_call(
        _scatter_kernel,
        grid=(t // tm,),
        in_specs=[pl.BlockSpec((1, 1, n), lambda i: (i, 0, 0), memory_space=pltpu.SMEM),
                  pl.BlockSpec((tm, d), lambda i: (i, 0))],
        out_specs=pl.BlockSpec(memory_space=pl.ANY),
        out_shape=jax.ShapeDtypeStruct((t * TOP_K, d // 2), U32),
        scratch_shapes=[pltpu.VMEM((tm, d // 2), U32), pltpu.SemaphoreType.DMA((tm,))],
        compiler_params=_params(("arbitrary",)),
        name="dispatch",
    )(dest.reshape(t // tm, 1, n), h)


def _weight_copies(w_hbm, wgu32, wd32, wsem, e, slot):
    wg_hbm, wu_hbm, wd_hbm = w_hbm
    return [pltpu.make_async_copy(wg_hbm.at[e], wgu32.at[slot, :, 0:D_EXPERT], wsem.at[slot, 0]),
            pltpu.make_async_copy(wu_hbm.at[e], wgu32.at[slot, :, D_EXPERT:2 * D_EXPERT], wsem.at[slot, 1]),
            pltpu.make_async_copy(wd_hbm.at[e], wd32.at[slot], wsem.at[slot, 2])]


def _expert_kernel(vb_ref, ve_ref, lo_ref, hi_ref, fb_ref, fe_ref, slot_ref, nxt1_ref, nxt2_ref, nv_ref,
                   x_ref, wg_hbm, wu_hbm, wd_hbm, y_ref, wgu32, wd32, wsem):
    v = pl.program_id(0)
    w_hbm = (wg_hbm, wu_hbm, wd_hbm)

    def start(e, slot):
        for cp in _weight_copies(w_hbm, wgu32, wd32, wsem, e, slot):
            cp.start(priority=WEIGHT_DMA_QUEUE)

    @pl.when(v < nv_ref[0])
    def _():
        slot = slot_ref[v]

        @pl.when(fe_ref[v] == 1)
        def _():
            @pl.when(v == 0)
            def _():
                start(ve_ref[v], slot)

                @pl.when(nxt1_ref[v] >= 0)
                def _():
                    start(nxt1_ref[v], (slot + 1) % WEIGHT_SLOTS)

            @pl.when(nxt2_ref[v] >= 0)
            def _():
                start(nxt2_ref[v], (slot + 2) % WEIGHT_SLOTS)

            for cp in _weight_copies(w_hbm, wgu32, wd32, wsem, ve_ref[v], slot):
                cp.wait()

        x_lo, x_hi = _unpack_halves(x_ref[...])
        x = jnp.concatenate([x_lo.astype(BF16), x_hi.astype(BF16)], axis=1)
        gu = _dot(x, wgu32[slot])
        act = _silu(gu[:, :D_EXPERT]) * gu[:, D_EXPERT:]
        y = _pack_halves(_dot(act.astype(BF16), wd32[slot]))
        rows = lax.broadcasted_iota(I32, y.shape, 0)
        mine = (rows >= lo_ref[v]) & (rows < hi_ref[v])

        @pl.when(fb_ref[v] == 1)
        def _():
            y_ref[...] = jnp.where(mine, y, jnp.uint32(0))

        @pl.when(fb_ref[v] == 0)
        def _():
            y_ref[...] = jnp.where(mine, y, y_ref[...])


def _expert_call(tables, xs, w_gate, w_up, w_down):
    n_rows, dp = xs.shape
    d = 2 * dp
    blk = EXPERT_BLOCK
    n_visits = tables[0].shape[0]
    any_spec = pl.BlockSpec(memory_space=pl.ANY)
    grid_spec = pltpu.PrefetchScalarGridSpec(
        num_scalar_prefetch=len(tables),
        grid=(n_visits,),
        in_specs=[pl.BlockSpec((blk, dp), lambda v, vb, *_: (vb[v], 0)), any_spec, any_spec, any_spec],
        out_specs=pl.BlockSpec((blk, dp), lambda v, vb, *_: (vb[v], 0)),
        scratch_shapes=[
            pltpu.VMEM((WEIGHT_SLOTS, d, 2 * D_EXPERT), F32),
            pltpu.VMEM((WEIGHT_SLOTS, D_EXPERT, d), F32),
            pltpu.SemaphoreType.DMA((WEIGHT_SLOTS, 3)),
        ],
    )
    return pl.pallas_call(
        _expert_kernel,
        grid_spec=grid_spec,
        out_shape=jax.ShapeDtypeStruct((n_rows, dp), U32),
        compiler_params=_params(("arbitrary",)),
        name="experts",
    )(*tables, xs, w_gate, w_up, w_down)


def _visit_tables(counts, n_rows):
    blk = EXPERT_BLOCK
    n_visits = n_rows // blk + N_EXPERTS - 1
    end = jnp.cumsum(counts)
    start = end - counts
    nonempty = counts > 0
    first_blk = start // blk
    last_blk = jnp.maximum(end - 1, 0) // blk
    nvis = jnp.where(nonempty, last_blk - first_blk + 1, 0)
    vis_end = jnp.cumsum(nvis)
    vis_start = vis_end - nvis
    n_used = vis_end[-1]
    ids = jnp.arange(N_EXPERTS, dtype=I32)
    ordinal = jnp.cumsum(nonempty.astype(I32)) - 1
    nxt_incl = lax.cummin(jnp.where(nonempty, ids, N_EXPERTS), axis=0, reverse=True)
    nxt_e = jnp.concatenate([nxt_incl[1:], jnp.full((1,), N_EXPERTS, I32)])
    nxt_e = jnp.where(nxt_e >= N_EXPERTS, -1, nxt_e)
    nxt2_e = jnp.sum(jnp.where(nxt_e[:, None] == ids[None, :], nxt_e[None, :], 0), axis=1)
    nxt2_e = jnp.where(nxt_e >= 0, nxt2_e, -1)
    v = jnp.arange(n_visits, dtype=I32)
    vc = jnp.clip(v, 0, jnp.maximum(n_used - 1, 0))
    ve = jnp.minimum(jnp.sum((vis_end[None, :] <= vc[:, None]).astype(I32), axis=1), N_EXPERTS - 1)
    onehot = ve[:, None] == ids[None, :]
    look = lambda tab: jnp.sum(jnp.where(onehot, tab.astype(I32)[None, :], 0), axis=1)
    v_first = look(vis_start)
    vb = look(first_blk) + (vc - v_first)
    lo = jnp.clip(look(start) - vb * blk, 0, blk)
    hi = jnp.clip(look(end) - vb * blk, 0, blk)
    fb = jnp.concatenate([jnp.ones((1,), I32), (vb[1:] != vb[:-1]).astype(I32)])
    fe = (vc == v_first).astype(I32)
    tables = (vb, ve, lo, hi, fb, fe, look(ordinal % WEIGHT_SLOTS), look(nxt_e), look(nxt2_e),
              n_used.astype(I32).reshape(1))
    return tables, start


def _combine_kernel(dest_ref, dnext_ref, ys_hbm, wts_ref, h_ref, wg_ref, wu_ref, wd_ref, g_ref, b_ref,
                    o_ref, buf, sem):
    tm = COMBINE_TM
    i = pl.program_id(0)
    slot = i % 2

    def issue(d_ref, s, kk, r):
        a = kk * tm + r
        _row_copy(ys_hbm, d_ref[0, 0, a], buf.at[s], a, sem.at[s, kk]).start(priority=kk % 2)

    @pl.when(i == 0)
    def _():
        def body(r, c):
            for kk in range(TOP_K):
                issue(dest_ref, slot, kk, r)
            return c

        lax.fori_loop(0, tm, body, 0, unroll=2)

    h = h_ref[...]
    x = h.astype(BF16)
    act = _silu(_dot(x, wg_ref[...])) * _dot(x, wu_ref[...])
    acc = DN_ALPHA * h + _dot(act.astype(BF16), wd_ref[...])
    half = acc.shape[1] // 2
    acc_lo, acc_hi = acc[:, :half], acc[:, half:]
    wts = wts_ref[...]
    for kk in range(TOP_K):
        for r in range(tm):
            issue(dnext_ref, 1 - slot, kk, r)
        _rows_wait(ys_hbm, buf.at[slot], kk * tm, tm, sem.at[slot, kk])
        y_lo, y_hi = _unpack_halves(buf[slot, kk * tm:(kk + 1) * tm, :])
        acc_lo = acc_lo + wts[:, kk:kk + 1] * y_lo
        acc_hi = acc_hi + wts[:, kk:kk + 1] * y_hi
    o_ref[...] = _layer_norm(jnp.concatenate([acc_lo, acc_hi], axis=1), g_ref[...], b_ref[...])

    @pl.when(i == pl.num_programs(0) - 1)
    def _():
        for kk in range(TOP_K):
            _rows_wait(ys_hbm, buf.at[1 - slot], kk * tm, tm, sem.at[1 - slot, kk])


def _combine_call(dest_km, ys, wts, h, w_sh_gate, w_sh_up, w_sh_down, g, b):
    t, d = h.shape
    ds = w_sh_gate.shape[1]
    tm = COMBINE_TM
    n = tm * TOP_K
    n_tiles = t // tm
    vec = pl.BlockSpec((1, d), lambda i: (0, 0))
    return pl.pallas_call(
        _combine_kernel,
        grid=(n_tiles,),
        in_specs=[
            pl.BlockSpec((1, 1, n), lambda i: (i, 0, 0), memory_space=pltpu.SMEM),
            pl.BlockSpec((1, 1, n), lambda i: (jnp.minimum(i + 1, n_tiles - 1), 0, 0),
                         memory_space=pltpu.SMEM),
            pl.BlockSpec(memory_space=pl.ANY),
            pl.BlockSpec((tm, TOP_K), lambda i: (i, 0)),
            pl.BlockSpec((tm, d), lambda i: (i, 0)),
            pl.BlockSpec((d, ds), lambda i: (0, 0)),
            pl.BlockSpec((d, ds), lambda i: (0, 0)),
            pl.BlockSpec((ds, d), lambda i: (0, 0)),
            vec, vec,
        ],
        out_specs=pl.BlockSpec((tm, d), lambda i: (i, 0)),
        out_shape=jax.ShapeDtypeStruct((t, d), F32),
        scratch_shapes=[pltpu.VMEM((2, n, d // 2), U32), pltpu.SemaphoreType.DMA((2, TOP_K))],
        compiler_params=_params(("arbitrary",)),
        name="combine_ln2",
    )(dest_km, dest_km, ys, wts, h, w_sh_gate, w_sh_up, w_sh_down, g, b)


def _token_mixing(x, w_in, w_short_conv, a_log, dt_bias, delta_norm_w, w_o_delta, conv_dw_w,
                  conv_dw_b, conv_ln_g, conv_ln_b, w_pw2, b_pw2, w_out, ln1_g, ln1_b):
    t, d = x.shape
    x16 = x.astype(BF16)
    w_t = w_in.T
    row = lambda a: a.reshape(1, -1)
    tile = pl.BlockSpec((TM, TN), lambda j, i: (i, j))
    wide = pl.BlockSpec((TM_WIDE, TN), lambda j, i: (i, j))
    xtile = pl.BlockSpec((TM_X, TN_X), lambda j, i: (i, j))
    conv_buf = pltpu.VMEM((TM_X + SUBLANES, TN_X), F32)

    scale = jnp.concatenate([jnp.full((1, QK_WIDTH), HEAD ** -0.5, F32), jnp.ones((1, QK_WIDTH), F32)], axis=1)
    qk = _proj_call(
        "proj_qk", x16, [w_t], [0], 2 * QK_WIDTH // TN_X, _qk_epilogue,
        [w_short_conv, scale],
        [pl.BlockSpec((SHORT_CONV, TN_X), lambda j, i: (0, j)), pl.BlockSpec((1, TN_X), lambda j, i: (0, j))],
        jax.ShapeDtypeStruct((t, 2 * QK_WIDTH), F32), xtile, scratch=[conv_buf], tm=TM_X, tn=TN_X,
        transposed=True)
    v_off = 2 * QK_WIDTH // TN_X
    v = _proj_call(
        "proj_v", x16, [w_t], [2 * QK_WIDTH], V_WIDTH // TN_X, _v_epilogue,
        [w_short_conv], [pl.BlockSpec((SHORT_CONV, TN_X), lambda j, i: (0, j + v_off))],
        jax.ShapeDtypeStruct((t, V_WIDTH), F32), xtile, scratch=[conv_buf], tm=TM_X, tn=TN_X,
        transposed=True)
    z = _proj_call(
        "proj_z", x16, [w_t], [COL_Z], V_WIDTH // TN_X, _z_epilogue, [], [],
        jax.ShapeDtypeStruct((t, V_WIDTH), F32), xtile, tm=TM_X, tn=TN_X, transposed=True)

    pad_h = lambda a: jnp.pad(a.reshape(1, -1), ((0, 0), (0, LANES - N_V_HEADS)))
    lane_vec = pl.BlockSpec((1, LANES), lambda j, i: (0, 0))
    lane_tile = pl.BlockSpec((TM, LANES), lambda j, i: (i, 0))
    lane_shape = jax.ShapeDtypeStruct((t, LANES), F32)
    gc, bt, gt = _proj_call(
        "proj_ab", x16, [w_t], [COL_AB], 1, _ab_epilogue, [pad_h(a_log), pad_h(dt_bias)],
        [lane_vec, lane_vec], (lane_shape, lane_shape, lane_shape), (lane_tile, lane_tile, lane_tile),
        tn=LANES, transposed=True)

    c = _proj_call(
        "proj_glu", x16, [w_t, w_t], [COL_GLU, COL_GLU + d], d // TN, _glu_epilogue, [], [],
        jax.ShapeDtypeStruct((t, d), F32), wide, tm=TM_WIDE, transposed=True)
    gates = _proj_call(
        "proj_gates", x16, [w_t], [COL_GATES], 2 * d // TN_X, _gates_epilogue, [], [],
        jax.ShapeDtypeStruct((t, 2 * d), F32), xtile, tm=TM_X, tn=TN_X, transposed=True)

    r = GDN_ROWS
    gcrow = gc[:, :N_V_HEADS].T.reshape(N_V_HEADS, t // r, 1, r)
    btrow = bt[:, N_V_HEADS:2 * N_V_HEADS].T.reshape(N_V_HEADS, t // r, 1, r)
    gt_chunk = gt[CHUNK - 1::CHUNK, :N_V_HEADS].T
    gtrow = jnp.broadcast_to(gt_chunk[:, :, None], (N_V_HEADS, t // CHUNK, LANES))
    gtrow = gtrow.reshape(N_V_HEADS, t // r, r // CHUNK, LANES)
    og = _gdn_call(qk, v, z, gc, bt, gt, gcrow, btrow, gtrow, row(delta_norm_w))

    ya = _proj_call(
        "proj_odelta", og, [w_o_delta], [0], d // TN, _odelta_epilogue, [gates], [tile],
        jax.ShapeDtypeStruct((t, d), F32), tile)

    c_act = _dwconv_call(c, conv_dw_w, row(conv_dw_b), row(conv_ln_g), row(conv_ln_b))
    g_off = d // TN_X
    mixed = _proj_call(
        "proj_pw2", c_act, [w_pw2], [0], d // TN_X, _pw2_epilogue, [row(b_pw2), gates, ya],
        [pl.BlockSpec((1, TN_X), lambda j, i: (0, j)),
         pl.BlockSpec((TM_X, TN_X), lambda j, i: (i, j + g_off)), xtile],
        jax.ShapeDtypeStruct((t, d), BF16), xtile, tm=TM_X, tn=TN_X)
    return _outproj_call(mixed, w_out.astype(BF16), x, row(ln1_g), row(ln1_b))


def _moe(h, w_router, router_bias, w_gate, w_up, w_down, w_sh_gate, w_sh_up, w_sh_down, ln2_g, ln2_b):
    t, d = h.shape
    idx, wts, rank, counts = _route_call(h, w_router, router_bias.reshape(1, -1))
    tables, start = _visit_tables(counts.reshape(-1).astype(I32), t * TOP_K)
    dest = _dest_call(idx, rank, start.astype(F32).reshape(1, -1))
    xs = _scatter_call(dest, h)
    ys = _expert_call(tables, xs, w_gate, w_up, w_down)
    tm = COMBINE_TM
    dest_km = dest.reshape(t // tm, tm, TOP_K).transpose(0, 2, 1).reshape(t // tm, 1, tm * TOP_K)
    return _combine_call(dest_km, ys, wts, h, w_sh_gate, w_sh_up, w_sh_down,
                         ln2_g.reshape(1, -1), ln2_b.reshape(1, -1))


def kernel(x, w_in, w_short_conv, a_log, dt_bias, delta_norm_w, w_o_delta, conv_dw_w, conv_dw_b,
           conv_ln_g, conv_ln_b, w_pw2, b_pw2, w_out, ln1_g, ln1_b, w_router, router_bias, w_gate,
           w_up, w_down, w_sh_gate, w_sh_up, w_sh_down, ln2_g, ln2_b):
    batch, seq, d = x.shape
    depth = w_in.shape[0]
    outs = []
    for bi in range(batch):
        h = x[bi]
        for li in range(depth):
            h = _token_mixing(h, w_in[li], w_short_conv[li], a_log[li], dt_bias[li], delta_norm_w[li],
                              w_o_delta[li], conv_dw_w[li], conv_dw_b[li], conv_ln_g[li], conv_ln_b[li],
                              w_pw2[li], b_pw2[li], w_out[li], ln1_g[li], ln1_b[li])
            h = _moe(h, w_router[li], router_bias[li], w_gate[li], w_up[li], w_down[li],
                     w_sh_gate[li], w_sh_up[li], w_sh_down[li], ln2_g[li], ln2_b[li])
        outs.append(h)
    return jnp.stack(outs, axis=0)
```

```python
import jax
import jax.numpy as jnp
from jax import lax
from jax.experimental import pallas as pl
from jax.experimental.pallas import tpu as pltpu

F32 = jnp.float32
BF16 = jnp.bfloat16
I32 = jnp.int32
U32 = jnp.uint32

D_MODEL = 2048
CHUNK = 64
N_QK_HEADS = 16
N_V_HEADS = 32
HEAD = 128
QK_WIDTH = N_QK_HEADS * HEAD
V_WIDTH = N_V_HEADS * HEAD
SHORT_CONV = 4
CONV_WIDTH = 31
N_EXPERTS = 256
TOP_K = 8
N_GROUPS = 8
TOPK_GROUPS = 4
D_EXPERT = 512
ROUTE_SCALE = 2.5
DN_ALPHA = 2.0 ** 0.25
LN_EPS = 1e-5
NORM_EPS = 1e-6
COL_Z = 2 * QK_WIDTH + V_WIDTH
COL_AB = COL_Z + V_WIDTH
COL_GLU = COL_AB + 2 * N_V_HEADS
COL_GATES = COL_GLU + 2 * D_MODEL

LANES = 128
SUBLANES = 8
VMEM_LIMIT = 56 * 1024 * 1024
TM = 512
TM_WIDE = 1024
TN = 512
TM_X = 512
TN_X = 1024
GDN_ROWS = 256
GDN_PAIRS = 4
CONV_TM = 256
HALO = 32
CONV_RC = 64
CONV_CC = 256
ROW_ALIGN = 64
ROUTE_TM = 256
EXPERT_BLOCK = 128
SCATTER_TM = 256
COMBINE_TM = 128
WEIGHT_DMA_QUEUE = 1
WEIGHT_SLOTS = 3


def _params(sem, vmem=VMEM_LIMIT):
    return pltpu.CompilerParams(dimension_semantics=sem, vmem_limit_bytes=vmem)


def _sigmoid(x):
    return jax.nn.sigmoid(x)


def _silu(x):
    return x * jax.nn.sigmoid(x)


def _softplus(x):
    return jnp.maximum(x, 0.0) + jnp.log1p(jnp.exp(-jnp.abs(x)))


def _dot(a, b):
    return jnp.dot(a, b, preferred_element_type=F32)


def _proj_call(name, x, ws, w_offs, n_tiles, epilogue, extras, extra_specs, out_shapes, out_specs,
               scratch=(), tm=TM, tn=TN, transposed=False):
    m, k = x.shape
    single_out = not isinstance(out_shapes, (tuple, list))
    if single_out:
        out_shapes, out_specs = (out_shapes,), (out_specs,)
    nw, ne, no = len(ws), len(extras), len(out_shapes)
    needs_cast = [w.dtype != BF16 for w in ws]
    contract = (((1,), (1,)), ((), ())) if transposed else (((1,), (0,)), ((), ()))

    def body(*refs):
        x_ref = refs[0]
        w_refs = refs[1:1 + nw]
        ex_refs = refs[1 + nw:1 + nw + ne]
        out_refs = refs[1 + nw + ne:1 + nw + ne + no]
        scr = refs[1 + nw + ne + no:]
        wb_refs = scr[:sum(needs_cast)]
        rest = scr[sum(needs_cast):]
        i = pl.program_id(1)
        wsrc, c = [], 0
        for kk in range(nw):
            if needs_cast[kk]:
                wsrc.append(wb_refs[c])
                c += 1
            else:
                wsrc.append(w_refs[kk])

        @pl.when(i == 0)
        def _():
            cc = 0
            for kk in range(nw):
                if needs_cast[kk]:
                    wb_refs[cc][...] = w_refs[kk][...].astype(BF16)
                    cc += 1

        xv = x_ref[...]
        accs = [lax.dot_general(xv, wr[...], contract, preferred_element_type=F32) for wr in wsrc]
        epilogue(i, accs, ex_refs, out_refs, rest)

    in_specs = [pl.BlockSpec((tm, k), lambda j, i: (i, 0))]
    for off in w_offs:
        if transposed:
            in_specs.append(pl.BlockSpec(
                (pl.Element(tn), pl.Element(k)),
                lambda j, i, off=off: (pl.multiple_of(off + tn * j, ROW_ALIGN), 0)))
        else:
            in_specs.append(pl.BlockSpec((k, tn), lambda j, i, off=off: (0, j + off)))
    in_specs += list(extra_specs)
    w_tile = (tn, k) if transposed else (k, tn)
    scratch_shapes = [pltpu.VMEM(w_tile, BF16) for c in needs_cast if c] + list(scratch)
    res = pl.pallas_call(
        body,
        grid=(n_tiles, m // tm),
        in_specs=in_specs,
        out_specs=tuple(out_specs),
        out_shape=tuple(out_shapes),
        scratch_shapes=scratch_shapes,
        compiler_params=_params(("parallel", "arbitrary")),
        name=name,
    )(x, *ws, *extras)
    return res[0] if single_out else res


def _short_conv_silu(i, acc, cw_ref, buf, tm):
    @pl.when(i == 0)
    def _():
        buf[0:SUBLANES, :] = jnp.zeros((SUBLANES, buf.shape[1]), F32)

    buf[SUBLANES:SUBLANES + tm, :] = acc
    cw = cw_ref[...]
    y = acc * cw[SHORT_CONV - 1:SHORT_CONV, :]
    for s in range(SHORT_CONV - 1):
        y = y + buf[pl.ds(SUBLANES - (SHORT_CONV - 1) + s, tm), :] * cw[s:s + 1, :]
    buf[0:SUBLANES, :] = buf[tm:tm + SUBLANES, :]
    return _silu(y)


def _qk_epilogue(i, accs, ex, outs, scr):
    cw_ref, scale_ref = ex
    (buf,) = scr
    y = _short_conv_silu(i, accs[0], cw_ref, buf, accs[0].shape[0])
    parts = []
    for g in range(y.shape[1] // HEAD):
        yg = y[:, g * HEAD:(g + 1) * HEAD]
        parts.append(yg * lax.rsqrt(jnp.sum(yg * yg, axis=-1, keepdims=True) + NORM_EPS))
    outs[0][...] = jnp.concatenate(parts, axis=1) * scale_ref[...]


def _v_epilogue(i, accs, ex, outs, scr):
    (cw_ref,) = ex
    (buf,) = scr
    outs[0][...] = _short_conv_silu(i, accs[0], cw_ref, buf, accs[0].shape[0])


def _z_epilogue(i, accs, ex, outs, scr):
    outs[0][...] = accs[0]


def _glu_epilogue(i, accs, ex, outs, scr):
    outs[0][...] = accs[0] * _sigmoid(accs[1])


def _gates_epilogue(i, accs, ex, outs, scr):
    outs[0][...] = _sigmoid(accs[0])


def _ab_epilogue(i, accs, ex, outs, scr):
    alog_ref, dtb_ref = ex
    gc_ref, bt_ref, gt_ref = outs
    acc = accs[0]
    g = -jnp.exp(alog_ref[...]) * _softplus(acc + dtb_ref[...])
    row = lax.broadcasted_iota(I32, g.shape, 0) % CHUNK
    s = 1
    while s < CHUNK:
        g = g + jnp.where(row >= s, pltpu.roll(g, s, axis=0), 0.0)
        s *= 2
    n_chunks = g.shape[0] // CHUNK
    tot = g.reshape(n_chunks, CHUNK, LANES)[:, CHUNK - 1:CHUNK, :]
    gc_ref[...] = g
    bt_ref[...] = _sigmoid(acc)
    gt_ref[...] = jnp.broadcast_to(tot, (n_chunks, CHUNK, LANES)).reshape(g.shape)


def _odelta_epilogue(i, accs, ex, outs, scr):
    (gate_ref,) = ex
    outs[0][...] = accs[0] * gate_ref[...]


def _pw2_epilogue(i, accs, ex, outs, scr):
    bias_ref, gate_ref, ya_ref = ex
    outs[0][...] = (ya_ref[...] + gate_ref[...] * (accs[0] + bias_ref[...])).astype(BF16)


class _Head:
    pass


def _gdn_kernel(q_ref, k_ref, v_ref, z_ref, gc_ref, bt_ref, gt_ref, gcrow_ref, btrow_ref, gtrow_ref,
                nw_ref, o_ref, s_ref):
    pp = pl.program_id(0)
    tb = pl.program_id(1)
    r = GDN_ROWS
    half = r // 2
    n_heads = 2 * GDN_PAIRS

    @pl.when(tb == 0)
    def _():
        s_ref[...] = jnp.zeros(s_ref.shape, F32)

    col_tok = lax.broadcasted_iota(I32, (half, r), 0)
    row_tok = lax.broadcasted_iota(I32, (half, r), 1)
    first_half = row_tok < half
    row_in_half = row_tok % half
    same = (row_in_half // CHUNK) == (col_tok // CHUNK)
    causal = same & (col_tok <= row_in_half)
    strict = same & (col_tok < row_in_half)
    lane = lax.broadcasted_iota(I32, (r, LANES), 1)
    gc_all = gc_ref[...]
    bt_all = bt_ref[...]
    gt_all = gt_ref[...]
    nt = (((1,), (1,)), ((), ()))
    tn = (((0,), (0,)), ((), ()))

    def transpose_bd(xc):
        return jnp.concatenate([jnp.where(first_half, xc, 0.0), jnp.where(first_half, 0.0, xc)], axis=0)

    heads = [_Head() for _ in range(n_heads)]
    for pi in range(GDN_PAIRS):
        q = q_ref[:, pi * HEAD:(pi + 1) * HEAD]
        k = k_ref[:, pi * HEAD:(pi + 1) * HEAD]
        k16 = k.astype(BF16)
        q16 = q.astype(BF16)
        kq = [lax.dot_general(k16[rs], jnp.concatenate([k16[rs], q16[rs]], axis=0), nt,
                              preferred_element_type=F32) for rs in (slice(0, half), slice(half, r))]
        kk = jnp.concatenate([kq[0][:, :half], kq[1][:, :half]], axis=1)
        qk = jnp.concatenate([kq[0][:, half:], kq[1][:, half:]], axis=1)
        for hd in heads[2 * pi:2 * pi + 2]:
            hd.q, hd.k, hd.kk, hd.qk = q, k, kk, qk

    for hl, hd in enumerate(heads):
        h = n_heads * pp + hl
        hd.cols = slice(hl * HEAD, (hl + 1) * HEAD)
        hd.gcol = jnp.sum(jnp.where(lane == h, gc_all, 0.0), axis=1, keepdims=True)
        hd.bcol = jnp.sum(jnp.where(lane == h + N_V_HEADS, bt_all, 0.0), axis=1, keepdims=True)
        hd.gtcol = jnp.sum(jnp.where(lane == h, gt_all, 0.0), axis=1, keepdims=True)
        gc_col_tok = jnp.concatenate([jnp.broadcast_to(hd.gcol[:half], (half, half)),
                                      jnp.broadcast_to(hd.gcol[half:], (half, half))], axis=1)
        hd.dm = jnp.exp(jnp.where(causal, gcrow_ref[hl, 0] - gc_col_tok, -jnp.inf))
        hd.p = jnp.where(strict, btrow_ref[hl, 0] * hd.kk * hd.dm, 0.0)
        hd.n = -hd.p

    for hd in heads:
        hd.res = _dot(hd.p.astype(BF16), transpose_bd(hd.p).astype(BF16))
    for hd in heads:
        hd.p = hd.res
    for _ in range(4):
        for hd in heads:
            rhs = jnp.concatenate([transpose_bd(hd.p), transpose_bd(hd.n)], axis=1).astype(BF16)
            hd.res = _dot(hd.p.astype(BF16), rhs)
        for hd in heads:
            hd.n = hd.n + hd.p + hd.res[:, r:]
            hd.p = hd.res[:, :r]
    for hd in heads:
        hd.res = _dot(hd.p.astype(BF16), transpose_bd(hd.n).astype(BF16))
    for hd in heads:
        hd.n = hd.n + hd.p + hd.res

    for hd in heads:
        hd.n16 = transpose_bd(hd.n).T.astype(BF16)
        hd.eg = jnp.exp(hd.gcol)
        hd.rhs = jnp.concatenate([v_ref[:, hd.cols] * hd.bcol, hd.k * (hd.bcol * hd.eg)], axis=1)
    for hd in heads:
        hd.sol16 = (hd.rhs + _dot(hd.n16, hd.rhs.astype(BF16))).astype(BF16)
    for hd in heads:
        hd.qkd16 = transpose_bd(hd.qk * hd.dm).T.astype(BF16)
    for hd in heads:
        hd.x = _dot(hd.qkd16, hd.sol16)
    for hl, hd in enumerate(heads):
        hd.qp16 = (hd.q * hd.eg - hd.x[:, HEAD:]).astype(BF16)
        hd.kd16 = (hd.k * jnp.exp(hd.gtcol - hd.gcol)).astype(BF16)
        hd.state = s_ref[hl]
        hd.o = []

    for j in range(r // CHUNK):
        sl = slice(j * CHUNK, (j + 1) * CHUNK)
        for hd in heads:
            hd.kc = lax.dot_general(hd.kd16[sl], hd.sol16[sl], tn, preferred_element_type=F32)
        for hd in heads:
            lhs = jnp.concatenate([hd.kc[:, HEAD:].astype(BF16), hd.qp16[sl]], axis=0)
            hd.ks_qs = _dot(lhs, hd.state.astype(BF16))
        for hl, hd in enumerate(heads):
            hd.o.append(hd.ks_qs[HEAD:] + hd.x[sl, :HEAD])
            hd.state = (hd.state * jnp.exp(gtrow_ref[hl, 0, j:j + 1, :]) + hd.kc[:, :HEAD]
                        - hd.ks_qs[:HEAD])

    for hl, hd in enumerate(heads):
        s_ref[hl] = hd.state
        o = jnp.concatenate(hd.o, axis=0)
        o = o * lax.rsqrt(jnp.mean(o * o, axis=-1, keepdims=True) + NORM_EPS) * nw_ref[...]
        o_ref[:, hd.cols] = (o * _silu(z_ref[:, hd.cols])).astype(o_ref.dtype)


def _gdn_call(qk, v, z, gc, bt, gt, gcrow, btrow, gtrow, nw):
    t = qk.shape[0]
    r = GDN_ROWS
    p = GDN_PAIRS
    k_off = N_QK_HEADS // p
    return pl.pallas_call(
        _gdn_kernel,
        grid=(N_QK_HEADS // p, t // r),
        in_specs=[
            pl.BlockSpec((r, p * HEAD), lambda pp, tb: (tb, pp)),
            pl.BlockSpec((r, p * HEAD), lambda pp, tb: (tb, k_off + pp)),
            pl.BlockSpec((r, 2 * p * HEAD), lambda pp, tb: (tb, pp)),
            pl.BlockSpec((r, 2 * p * HEAD), lambda pp, tb: (tb, pp)),
            pl.BlockSpec((r, LANES), lambda pp, tb: (tb, 0)),
            pl.BlockSpec((r, LANES), lambda pp, tb: (tb, 0)),
            pl.BlockSpec((r, LANES), lambda pp, tb: (tb, 0)),
            pl.BlockSpec((2 * p, 1, 1, r), lambda pp, tb: (pp, tb, 0, 0)),
            pl.BlockSpec((2 * p, 1, 1, r), lambda pp, tb: (pp, tb, 0, 0)),
            pl.BlockSpec((2 * p, 1, r // CHUNK, LANES), lambda pp, tb: (pp, tb, 0, 0)),
            pl.BlockSpec((1, HEAD), lambda pp, tb: (0, 0)),
        ],
        out_specs=pl.BlockSpec((r, 2 * p * HEAD), lambda pp, tb: (tb, pp)),
        out_shape=jax.ShapeDtypeStruct((t, V_WIDTH), BF16),
        scratch_shapes=[pltpu.VMEM((2 * p, HEAD, HEAD), F32)],
        compiler_params=_params(("parallel", "arbitrary")),
        name="gdn",
    )(qk, qk, v, z, gc, bt, gt, gcrow, btrow, gtrow, nw)


def _layer_norm(r, g, b):
    mu = jnp.mean(r, axis=-1, keepdims=True)
    rc = r - mu
    var = jnp.mean(rc * rc, axis=-1, keepdims=True)
    return rc * lax.rsqrt(var + LN_EPS) * g + b


def _dwconv_kernel(c_ref, w_ref, b_ref, g_ref, beta_ref, o_ref, buf, shifted, accbuf):
    i = pl.program_id(0)
    tm = CONV_TM
    ch = buf.shape[1]
    n_shift = HALO + tm - SUBLANES

    @pl.when(i == 0)
    def _():
        buf[0:HALO, :] = jnp.zeros((HALO, ch), F32)

    buf[HALO:HALO + tm, :] = c_ref[...]
    for b in range(1, SUBLANES):
        shifted[b - 1] = buf[pl.ds(b, n_shift), :]

    def row_body(rc, carry):
        r0 = pl.multiple_of(rc * CONV_RC, CONV_RC)
        for cc in range(ch // CONV_CC):
            cs = slice(cc * CONV_CC, (cc + 1) * CONV_CC)
            acc = buf[pl.ds(HALO + r0, CONV_RC), cs] * w_ref[CONV_WIDTH - 1:CONV_WIDTH, cs] + b_ref[:, cs]
            for s in range(CONV_WIDTH - 1):
                a, b = divmod(HALO - (CONV_WIDTH - 1) + s, SUBLANES)
                if b == 0:
                    src = buf[pl.ds(a * SUBLANES + r0, CONV_RC), cs]
                else:
                    src = shifted[b - 1, pl.ds(a * SUBLANES + r0, CONV_RC), cs]
                acc = acc + src * w_ref[s:s + 1, cs]
            accbuf[pl.ds(r0, CONV_RC), cs] = acc
        return carry

    lax.fori_loop(0, tm // CONV_RC, row_body, 0)
    buf[0:HALO, :] = buf[tm:tm + HALO, :]
    o_ref[...] = _silu(_layer_norm(accbuf[...], g_ref[...], beta_ref[...])).astype(o_ref.dtype)


def _dwconv_call(c, w, b, g, beta):
    t, ch = c.shape
    tm = CONV_TM
    vec = pl.BlockSpec((1, ch), lambda i: (0, 0))
    return pl.pallas_call(
        _dwconv_kernel,
        grid=(t // tm,),
        in_specs=[pl.BlockSpec((tm, ch), lambda i: (i, 0)),
                  pl.BlockSpec((CONV_WIDTH, ch), lambda i: (0, 0)), vec, vec, vec],
        out_specs=pl.BlockSpec((tm, ch), lambda i: (i, 0)),
        out_shape=jax.ShapeDtypeStruct((t, ch), BF16),
        scratch_shapes=[pltpu.VMEM((HALO + tm, ch), F32),
                        pltpu.VMEM((SUBLANES - 1, HALO + tm - SUBLANES, ch), F32),
                        pltpu.VMEM((tm, ch), F32)],
        compiler_params=_params(("arbitrary",)),
        name="dwconv_ln",
    )(c, w, b, g, beta)


def _outproj_kernel(m_ref, w_ref, x_ref, g_ref, b_ref, h_ref):
    y = _dot(m_ref[...], w_ref[...])
    h_ref[...] = _layer_norm(DN_ALPHA * x_ref[...] + y, g_ref[...], b_ref[...])


def _outproj_call(mixed, w16, x, g, b):
    t, d = x.shape
    tm = 256
    vec = pl.BlockSpec((1, d), lambda i: (0, 0))
    return pl.pallas_call(
        _outproj_kernel,
        grid=(t // tm,),
        in_specs=[pl.BlockSpec((tm, d), lambda i: (i, 0)), pl.BlockSpec((d, d), lambda i: (0, 0)),
                  pl.BlockSpec((tm, d), lambda i: (i, 0)), vec, vec],
        out_specs=pl.BlockSpec((tm, d), lambda i: (i, 0)),
        out_shape=jax.ShapeDtypeStruct((t, d), F32),
        compiler_params=_params(("parallel",)),
        name="outproj_ln1",
    )(mixed, w16, x, g, b)


def _route_kernel(h_ref, w_ref, bias_ref, idx_ref, wts_ref, rank_ref, cnt_ref, carry):
    @pl.when(pl.program_id(0) == 0)
    def _():
        carry[...] = jnp.zeros(carry.shape, F32)

    logits = jnp.dot(h_ref[...], w_ref[...], precision=lax.Precision.HIGHEST,
                     preferred_element_type=F32)
    scores = _sigmoid(logits)
    biased = scores + bias_ref[...]
    shape = biased.shape
    tm = shape[0]
    lane_i = lax.broadcasted_iota(I32, shape, 1)
    lane = lane_i.astype(F32)
    per_group = N_EXPERTS // N_GROUPS
    grp = (lane_i // per_group).astype(F32)
    neg = -jnp.inf
    big = 1e9

    def rmax(x):
        return jnp.max(x, axis=1, keepdims=True)

    def rmin(x):
        return jnp.min(x, axis=1, keepdims=True)

    def rsum(x):
        return jnp.sum(x, axis=1, keepdims=True)

    gs = jnp.zeros(shape, F32)
    for g in range(N_GROUPS):
        in_g = grp == float(g)
        m = jnp.where(in_g, biased, neg)
        m1 = rmax(m)
        i1 = rmin(jnp.where(m == m1, lane, big))
        m2 = rmax(jnp.where(lane == i1, neg, m))
        gs = jnp.where(in_g, m1 + m2, gs)
    sel_g = jnp.zeros(shape, jnp.bool_)
    cur = gs
    for _ in range(TOPK_GROUPS):
        mx = rmax(cur)
        gi = rmin(jnp.where(cur == mx, grp, big))
        hit = grp == gi
        sel_g = sel_g | hit
        cur = jnp.where(hit, neg, cur)
    masked = jnp.where(sel_g, biased, neg)
    ids, ws = [], []
    sel = jnp.zeros(shape, F32)
    for _ in range(TOP_K):
        mx = rmax(masked)
        ik = rmin(jnp.where(masked == mx, lane, big))
        hit = lane == ik
        ws.append(rsum(jnp.where(hit, scores, 0.0)))
        ids.append(ik)
        sel = jnp.where(hit, 1.0, sel)
        masked = jnp.where(hit, neg, masked)
    w = jnp.concatenate(ws, axis=1)
    wts_ref[...] = w / rsum(w) * ROUTE_SCALE
    idx_ref[...] = jnp.concatenate(ids, axis=1).astype(I32)

    earlier = (lax.broadcasted_iota(I32, (tm, tm), 1) < lax.broadcasted_iota(I32, (tm, tm), 0))
    before = _dot(earlier.astype(BF16), sel.astype(BF16)) + carry[...]
    ranks = [rsum(jnp.where(lane == ik, before, 0.0)) for ik in ids]
    rank_ref[...] = jnp.concatenate(ranks, axis=1).astype(I32)
    carry[...] = carry[...] + jnp.sum(sel, axis=0, keepdims=True)
    cnt_ref[...] = carry[...]


def _route_call(h, w_router, bias):
    t, d = h.shape
    tm = ROUTE_TM
    tok = pl.BlockSpec((tm, TOP_K), lambda i: (i, 0))
    return pl.pallas_call(
        _route_kernel,
        grid=(t // tm,),
        in_specs=[pl.BlockSpec((tm, d), lambda i: (i, 0)),
                  pl.BlockSpec((d, N_EXPERTS), lambda i: (0, 0)),
                  pl.BlockSpec((1, N_EXPERTS), lambda i: (0, 0))],
        out_specs=(tok, tok, tok, pl.BlockSpec((1, N_EXPERTS), lambda i: (0, 0))),
        out_shape=(jax.ShapeDtypeStruct((t, TOP_K), I32), jax.ShapeDtypeStruct((t, TOP_K), F32),
                   jax.ShapeDtypeStruct((t, TOP_K), I32), jax.ShapeDtypeStruct((1, N_EXPERTS), F32)),
        scratch_shapes=[pltpu.VMEM((1, N_EXPERTS), F32)],
        compiler_params=_params(("arbitrary",)),
        name="route",
    )(h, w_router, bias)


def _dest_kernel(idx_ref, rank_ref, start_ref, dest_ref):
    idx = idx_ref[...]
    shape = (idx.shape[0], N_EXPERTS)
    lane = lax.broadcasted_iota(I32, shape, 1)
    start = start_ref[...]
    cols = []
    for kk in range(TOP_K):
        hit = lane == idx[:, kk:kk + 1]
        cols.append(jnp.sum(jnp.where(hit, start, 0.0), axis=1, keepdims=True))
    dest_ref[...] = jnp.concatenate(cols, axis=1).astype(I32) + rank_ref[...]


def _dest_call(idx, rank, start_f32):
    t = idx.shape[0]
    tm = 512
    tok = pl.BlockSpec((tm, TOP_K), lambda i: (i, 0))
    return pl.pallas_call(
        _dest_kernel,
        grid=(t // tm,),
        in_specs=[tok, tok, pl.BlockSpec((1, N_EXPERTS), lambda i: (0, 0))],
        out_specs=tok,
        out_shape=jax.ShapeDtypeStruct((t, TOP_K), I32),
        compiler_params=_params(("parallel",)),
        name="dest",
    )(idx, rank, start_f32)


def _row_copy(src, src_row, dst, dst_row, sem):
    return pltpu.make_async_copy(src.at[pl.ds(src_row, 1), :], dst.at[pl.ds(dst_row, 1), :], sem)


def _pack_halves(x):
    half = x.shape[1] // 2
    lo = lax.bitcast_convert_type(x[:, :half].astype(BF16).astype(F32), U32)
    hi = lax.bitcast_convert_type(x[:, half:].astype(BF16).astype(F32), U32)
    return hi | (lo >> 16)


def _unpack_halves(w):
    lo = lax.bitcast_convert_type(w << 16, F32)
    hi = lax.bitcast_convert_type(w & jnp.uint32(0xFFFF0000), F32)
    return lo, hi


def _rows_wait(src, dst, dst_row, n_rows, sem):
    pltpu.make_async_copy(src.at[pl.ds(0, n_rows), :], dst.at[pl.ds(dst_row, n_rows), :], sem).wait()


def _scatter_kernel(dest_ref, h_ref, xs_hbm, packed, sem):
    packed[...] = _pack_halves(h_ref[...])

    def issue(tok, c):
        for kk in range(TOP_K):
            _row_copy(packed, tok, xs_hbm, dest_ref[0, 0, tok * TOP_K + kk], sem.at[tok]).start(priority=kk % 2)
        return c

    lax.fori_loop(0, SCATTER_TM, issue, 0, unroll=2)

    def wait(tok, c):
        _rows_wait(packed, xs_hbm, 0, TOP_K, sem.at[tok])
        return c

    lax.fori_loop(0, SCATTER_TM, wait, 0, unroll=8)


def _scatter_call(dest, h):
    t, d = h.shape
    tm = SCATTER_TM
    n = tm * TOP_K
    return pl.pallas_call(
        _scatter_kernel,
        grid=(t // tm,),
        in_specs=[pl.BlockSpec((1, 1, n), lambda i: (i, 0, 0), memory_space=pltpu.SMEM),
                  pl.BlockSpec((tm, d), lambda i: (i, 0))],
        out_specs=pl.BlockSpec(memory_space=pl.ANY),
        out_shape=jax.ShapeDtypeStruct((t * TOP_K, d // 2), U32),
        scratch_shapes=[pltpu.VMEM((tm, d // 2), U32), pltpu.SemaphoreType.DMA((tm,))],
        compiler_params=_params(("arbitrary",)),
        name="dispatch",
    )(dest.reshape(t // tm, 1, n), h)


def _weight_copies(w_hbm, wgu32, wd32, wsem, e, slot):
    wg_hbm, wu_hbm, wd_hbm = w_hbm
    return [pltpu.make_async_copy(wg_hbm.at[e], wgu32.at[slot, :, 0:D_EXPERT], wsem.at[slot, 0]),
            pltpu.make_async_copy(wu_hbm.at[e], wgu32.at[slot, :, D_EXPERT:2 * D_EXPERT], wsem.at[slot, 1]),
            pltpu.make_async_copy(wd_hbm.at[e], wd32.at[slot], wsem.at[slot, 2])]


def _expert_kernel(vb_ref, ve_ref, lo_ref, hi_ref, fb_ref, fe_ref, slot_ref, nxt1_ref, nxt2_ref, nv_ref,
                   x_ref, wg_hbm, wu_hbm, wd_hbm, y_ref, wgu32, wd32, wsem):
    v = pl.program_id(0)
    w_hbm = (wg_hbm, wu_hbm, wd_hbm)

    def start(e, slot):
        for cp in _weight_copies(w_hbm, wgu32, wd32, wsem, e, slot):
            cp.start(priority=WEIGHT_DMA_QUEUE)

    @pl.when(v < nv_ref[0])
    def _():
        slot = slot_ref[v]

        @pl.when(fe_ref[v] == 1)
        def _():
            @pl.when(v == 0)
            def _():
                start(ve_ref[v], slot)

                @pl.when(nxt1_ref[v] >= 0)
                def _():
                    start(nxt1_ref[v], (slot + 1) % WEIGHT_SLOTS)

            @pl.when(nxt2_ref[v] >= 0)
            def _():
                start(nxt2_ref[v], (slot + 2) % WEIGHT_SLOTS)

            for cp in _weight_copies(w_hbm, wgu32, wd32, wsem, ve_ref[v], slot):
                cp.wait()

        x_lo, x_hi = _unpack_halves(x_ref[...])
        x = jnp.concatenate([x_lo.astype(BF16), x_hi.astype(BF16)], axis=1)
        gu = _dot(x, wgu32[slot])
        act = _silu(gu[:, :D_EXPERT]) * gu[:, D_EXPERT:]
        y = _pack_halves(_dot(act.astype(BF16), wd32[slot]))
        rows = lax.broadcasted_iota(I32, y.shape, 0)
        mine = (rows >= lo_ref[v]) & (rows < hi_ref[v])

        @pl.when(fb_ref[v] == 1)
        def _():
            y_ref[...] = jnp.where(mine, y, jnp.uint32(0))

        @pl.when(fb_ref[v] == 0)
        def _():
            y_ref[...] = jnp.where(mine, y, y_ref[...])


def _expert_call(tables, xs, w_gate, w_up, w_down):
    n_rows, dp = xs.shape
    d = 2 * dp
    blk = EXPERT_BLOCK
    n_visits = tables[0].shape[0]
    any_spec = pl.BlockSpec(memory_space=pl.ANY)
    grid_spec = pltpu.PrefetchScalarGridSpec(
        num_scalar_prefetch=len(tables),
        grid=(n_visits,),
        in_specs=[pl.BlockSpec((blk, dp), lambda v, vb, *_: (vb[v], 0)), any_spec, any_spec, any_spec],
        out_specs=pl.BlockSpec((blk, dp), lambda v, vb, *_: (vb[v], 0)),
        scratch_shapes=[
            pltpu.VMEM((WEIGHT_SLOTS, d, 2 * D_EXPERT), F32),
            pltpu.VMEM((WEIGHT_SLOTS, D_EXPERT, d), F32),
            pltpu.SemaphoreType.DMA((WEIGHT_SLOTS, 3)),
        ],
    )
    return pl.pallas_call(
        _expert_kernel,
        grid_spec=grid_spec,
        out_shape=jax.ShapeDtypeStruct((n_rows, dp), U32),
        compiler_params=_params(("arbitrary",)),
        name="experts",
    )(*tables, xs, w_gate, w_up, w_down)


def _visit_tables(counts, n_rows):
    blk = EXPERT_BLOCK
    n_visits = n_rows // blk + N_EXPERTS - 1
    end = jnp.cumsum(counts)
    start = end - counts
    nonempty = counts > 0
    first_blk = start // blk
    last_blk = jnp.maximum(end - 1, 0) // blk
    nvis = jnp.where(nonempty, last_blk - first_blk + 1, 0)
    vis_end = jnp.cumsum(nvis)
    vis_start = vis_end - nvis
    n_used = vis_end[-1]
    ids = jnp.arange(N_EXPERTS, dtype=I32)
    ordinal = jnp.cumsum(nonempty.astype(I32)) - 1
    nxt_incl = lax.cummin(jnp.where(nonempty, ids, N_EXPERTS), axis=0, reverse=True)
    nxt_e = jnp.concatenate([nxt_incl[1:], jnp.full((1,), N_EXPERTS, I32)])
    nxt_e = jnp.where(nxt_e >= N_EXPERTS, -1, nxt_e)
    nxt2_e = jnp.sum(jnp.where(nxt_e[:, None] == ids[None, :], nxt_e[None, :], 0), axis=1)
    nxt2_e = jnp.where(nxt_e >= 0, nxt2_e, -1)
    v = jnp.arange(n_visits, dtype=I32)
    vc = jnp.clip(v, 0, jnp.maximum(n_used - 1, 0))
    ve = jnp.minimum(jnp.sum((vis_end[None, :] <= vc[:, None]).astype(I32), axis=1), N_EXPERTS - 1)
    onehot = ve[:, None] == ids[None, :]
    look = lambda tab: jnp.sum(jnp.where(onehot, tab.astype(I32)[None, :], 0), axis=1)
    v_first = look(vis_start)
    vb = look(first_blk) + (vc - v_first)
    lo = jnp.clip(look(start) - vb * blk, 0, blk)
    hi = jnp.clip(look(end) - vb * blk, 0, blk)
    fb = jnp.concatenate([jnp.ones((1,), I32), (vb[1:] != vb[:-1]).astype(I32)])
    fe = (vc == v_first).astype(I32)
    tables = (vb, ve, lo, hi, fb, fe, look(ordinal % WEIGHT_SLOTS), look(nxt_e), look(nxt2_e),
              n_used.astype(I32).reshape(1))
    return tables, start


def _combine_kernel(dest_ref, dnext_ref, ys_hbm, wts_ref, h_ref, wg_ref, wu_ref, wd_ref, g_ref, b_ref,
                    o_ref, buf, sem):
    tm = COMBINE_TM
    i = pl.program_id(0)
    slot = i % 2

    def issue(d_ref, s, kk, r):
        a = kk * tm + r
        _row_copy(ys_hbm, d_ref[0, 0, a], buf.at[s], a, sem.at[s, kk]).start(priority=kk % 2)

    @pl.when(i == 0)
    def _():
        def body(r, c):
            for kk in range(TOP_K):
                issue(dest_ref, slot, kk, r)
            return c

        lax.fori_loop(0, tm, body, 0, unroll=2)

    h = h_ref[...]
    x = h.astype(BF16)
    act = _silu(_dot(x, wg_ref[...])) * _dot(x, wu_ref[...])
    acc = DN_ALPHA * h + _dot(act.astype(BF16), wd_ref[...])
    half = acc.shape[1] // 2
    acc_lo, acc_hi = acc[:, :half], acc[:, half:]
    wts = wts_ref[...]
    for kk in range(TOP_K):
        for r in range(tm):
            issue(dnext_ref, 1 - slot, kk, r)
        _rows_wait(ys_hbm, buf.at[slot], kk * tm, tm, sem.at[slot, kk])
        y_lo, y_hi = _unpack_halves(buf[slot, kk * tm:(kk + 1) * tm, :])
        acc_lo = acc_lo + wts[:, kk:kk + 1] * y_lo
        acc_hi = acc_hi + wts[:, kk:kk + 1] * y_hi
    o_ref[...] = _layer_norm(jnp.concatenate([acc_lo, acc_hi], axis=1), g_ref[...], b_ref[...])

    @pl.when(i == pl.num_programs(0) - 1)
    def _():
        for kk in range(TOP_K):
            _rows_wait(ys_hbm, buf.at[1 - slot], kk * tm, tm, sem.at[1 - slot, kk])


def _combine_call(dest_km, ys, wts, h, w_sh_gate, w_sh_up, w_sh_down, g, b):
    t, d = h.shape
    ds = w_sh_gate.shape[1]
    tm = COMBINE_TM
    n = tm * TOP_K
    n_tiles = t // tm
    vec = pl.BlockSpec((1, d), lambda i: (0, 0))
    return pl.pallas_call(
        _combine_kernel,
        grid=(n_tiles,),
        in_specs=[
            pl.BlockSpec((1, 1, n), lambda i: (i, 0, 0), memory_space=pltpu.SMEM),
            pl.BlockSpec((1, 1, n), lambda i: (jnp.minimum(i + 1, n_tiles - 1), 0, 0),
                         memory_space=pltpu.SMEM),
            pl.BlockSpec(memory_space=pl.ANY),
            pl.BlockSpec((tm, TOP_K), lambda i: (i, 0)),
            pl.BlockSpec((tm, d), lambda i: (i, 0)),
            pl.BlockSpec((d, ds), lambda i: (0, 0)),
            pl.BlockSpec((d, ds), lambda i: (0, 0)),
            pl.BlockSpec((ds, d), lambda i: (0, 0)),
            vec, vec,
        ],
        out_specs=pl.BlockSpec((tm, d), lambda i: (i, 0)),
        out_shape=jax.ShapeDtypeStruct((t, d), F32),
        scratch_shapes=[pltpu.VMEM((2, n, d // 2), U32), pltpu.SemaphoreType.DMA((2, TOP_K))],
        compiler_params=_params(("arbitrary",)),
        name="combine_ln2",
    )(dest_km, dest_km, ys, wts, h, w_sh_gate, w_sh_up, w_sh_down, g, b)


def _token_mixing(x, w_in, w_short_conv, a_log, dt_bias, delta_norm_w, w_o_delta, conv_dw_w,
                  conv_dw_b, conv_ln_g, conv_ln_b, w_pw2, b_pw2, w_out, ln1_g, ln1_b):
    t, d = x.shape
    x16 = x.astype(BF16)
    w_t = w_in.T
    row = lambda a: a.reshape(1, -1)
    tile = pl.BlockSpec((TM, TN), lambda j, i: (i, j))
    wide = pl.BlockSpec((TM_WIDE, TN), lambda j, i: (i, j))
    xtile = pl.BlockSpec((TM_X, TN_X), lambda j, i: (i, j))
    conv_buf = pltpu.VMEM((TM_X + SUBLANES, TN_X), F32)

    scale = jnp.concatenate([jnp.full((1, QK_WIDTH), HEAD ** -0.5, F32), jnp.ones((1, QK_WIDTH), F32)], axis=1)
    qk = _proj_call(
        "proj_qk", x16, [w_t], [0], 2 * QK_WIDTH // TN_X, _qk_epilogue,
        [w_short_conv, scale],
        [pl.BlockSpec((SHORT_CONV, TN_X), lambda j, i: (0, j)), pl.BlockSpec((1, TN_X), lambda j, i: (0, j))],
        jax.ShapeDtypeStruct((t, 2 * QK_WIDTH), F32), xtile, scratch=[conv_buf], tm=TM_X, tn=TN_X,
        transposed=True)
    v_off = 2 * QK_WIDTH // TN_X
    v = _proj_call(
        "proj_v", x16, [w_t], [2 * QK_WIDTH], V_WIDTH // TN_X, _v_epilogue,
        [w_short_conv], [pl.BlockSpec((SHORT_CONV, TN_X), lambda j, i: (0, j + v_off))],
        jax.ShapeDtypeStruct((t, V_WIDTH), F32), xtile, scratch=[conv_buf], tm=TM_X, tn=TN_X,
        transposed=True)
    z = _proj_call(
        "proj_z", x16, [w_t], [COL_Z], V_WIDTH // TN_X, _z_epilogue, [], [],
        jax.ShapeDtypeStruct((t, V_WIDTH), F32), xtile, tm=TM_X, tn=TN_X, transposed=True)

    pad_h = lambda a: jnp.pad(a.reshape(1, -1), ((0, 0), (0, LANES - N_V_HEADS)))
    lane_vec = pl.BlockSpec((1, LANES), lambda j, i: (0, 0))
    lane_tile = pl.BlockSpec((TM, LANES), lambda j, i: (i, 0))
    lane_shape = jax.ShapeDtypeStruct((t, LANES), F32)
    gc, bt, gt = _proj_call(
        "proj_ab", x16, [w_t], [COL_AB], 1, _ab_epilogue, [pad_h(a_log), pad_h(dt_bias)],
        [lane_vec, lane_vec], (lane_shape, lane_shape, lane_shape), (lane_tile, lane_tile, lane_tile),
        tn=LANES, transposed=True)

    c = _proj_call(
        "proj_glu", x16, [w_t, w_t], [COL_GLU, COL_GLU + d], d // TN, _glu_epilogue, [], [],
        jax.ShapeDtypeStruct((t, d), F32), wide, tm=TM_WIDE, transposed=True)
    gates = _proj_call(
        "proj_gates", x16, [w_t], [COL_GATES], 2 * d // TN_X, _gates_epilogue, [], [],
        jax.ShapeDtypeStruct((t, 2 * d), F32), xtile, tm=TM_X, tn=TN_X, transposed=True)

    r = GDN_ROWS
    gcrow = gc[:, :N_V_HEADS].T.reshape(N_V_HEADS, t // r, 1, r)
    btrow = bt[:, N_V_HEADS:2 * N_V_HEADS].T.reshape(N_V_HEADS, t // r, 1, r)
    gt_chunk = gt[CHUNK - 1::CHUNK, :N_V_HEADS].T
    gtrow = jnp.broadcast_to(gt_chunk[:, :, None], (N_V_HEADS, t // CHUNK, LANES))
    gtrow = gtrow.reshape(N_V_HEADS, t // r, r // CHUNK, LANES)
    og = _gdn_call(qk, v, z, gc, bt, gt, gcrow, btrow, gtrow, row(delta_norm_w))

    ya = _proj_call(
        "proj_odelta", og, [w_o_delta], [0], d // TN, _odelta_epilogue, [gates], [tile],
        jax.ShapeDtypeStruct((t, d), F32), tile)

    c_act = _dwconv_call(c, conv_dw_w, row(conv_dw_b), row(conv_ln_g), row(conv_ln_b))
    g_off = d // TN_X
    mixed = _proj_call(
        "proj_pw2", c_act, [w_pw2], [0], d // TN_X, _pw2_epilogue, [row(b_pw2), gates, ya],
        [pl.BlockSpec((1, TN_X), lambda j, i: (0, j)),
         pl.BlockSpec((TM_X, TN_X), lambda j, i: (i, j + g_off)), xtile],
        jax.ShapeDtypeStruct((t, d), BF16), xtile, tm=TM_X, tn=TN_X)
    return _outproj_call(mixed, w_out.astype(BF16), x, row(ln1_g), row(ln1_b))


def _moe(h, w_router, router_bias, w_gate, w_up, w_down, w_sh_gate, w_sh_up, w_sh_down, ln2_g, ln2_b):
    t, d = h.shape
    idx, wts, rank, counts = _route_call(h, w_router, router_bias.reshape(1, -1))
    tables, start = _visit_tables(counts.reshape(-1).astype(I32), t * TOP_K)
    dest = _dest_call(idx, rank, start.astype(F32).reshape(1, -1))
    xs = _scatter_call(dest, h)
    ys = _expert_call(tables, xs, w_gate, w_up, w_down)
    tm = COMBINE_TM
    dest_km = dest.reshape(t // tm, tm, TOP_K).transpose(0, 2, 1).reshape(t // tm, 1, tm * TOP_K)
    return _combine_call(dest_km, ys, wts, h, w_sh_gate, w_sh_up, w_sh_down,
                         ln2_g.reshape(1, -1), ln2_b.reshape(1, -1))


def kernel(x, w_in, w_short_conv, a_log, dt_bias, delta_norm_w, w_o_delta, conv_dw_w, conv_dw_b,
           conv_ln_g, conv_ln_b, w_pw2, b_pw2, w_out, ln1_g, ln1_b, w_router, router_bias, w_gate,
           w_up, w_down, w_sh_gate, w_sh_up, w_sh_down, ln2_g, ln2_b):
    batch, seq, d = x.shape
    depth = w_in.shape[0]
    outs = []
    for bi in range(batch):
        h = x[bi]
        for li in range(depth):
            h = _token_mixing(h, w_in[li], w_short_conv[li], a_log[li], dt_bias[li], delta_norm_w[li],
                              w_o_delta[li], conv_dw_w[li], conv_dw_b[li], conv_ln_g[li], conv_ln_b[li],
                              w_pw2[li], b_pw2[li], w_out[li], ln1_g[li], ln1_b[li])
            h = _moe(h, w_router[li], router_bias[li], w_gate[li], w_up[li], w_down[li],
                     w_sh_gate[li], w_sh_up[li], w_sh_down[li], ln2_g[li], ln2_b[li])
        outs.append(h)
    return jnp.stack(outs, axis=0)
```

```python
import jax
import jax.numpy as jnp
from jax import lax
from jax.experimental import pallas as pl
from jax.experimental.pallas import tpu as pltpu

F32 = jnp.float32
BF16 = jnp.bfloat16
I32 = jnp.int32
U32 = jnp.uint32

D_MODEL = 2048
CHUNK = 64
N_QK_HEADS = 16
N_V_HEADS = 32
HEAD = 128
QK_WIDTH = N_QK_HEADS * HEAD
V_WIDTH = N_V_HEADS * HEAD
SHORT_CONV = 4
CONV_WIDTH = 31
N_EXPERTS = 256
TOP_K = 8
N_GROUPS = 8
TOPK_GROUPS = 4
D_EXPERT = 512
ROUTE_SCALE = 2.5
DN_ALPHA = 2.0 ** 0.25
LN_EPS = 1e-5
NORM_EPS = 1e-6
COL_Z = 2 * QK_WIDTH + V_WIDTH
COL_AB = COL_Z + V_WIDTH
COL_GLU = COL_AB + 2 * N_V_HEADS
COL_GATES = COL_GLU + 2 * D_MODEL

LANES = 128
SUBLANES = 8
VMEM_LIMIT = 56 * 1024 * 1024
TM = 512
TM_WIDE = 1024
TN = 512
TM_X = 512
TN_X = 1024
GDN_ROWS = 256
GDN_PAIRS = 4
CONV_TM = 256
HALO = 32
CONV_RC = 64
CONV_CC = 256
ROW_ALIGN = 64
ROUTE_TM = 256
EXPERT_BLOCK = 128
SCATTER_TM = 256
COMBINE_TM = 128
WEIGHT_DMA_QUEUE = 1
WEIGHT_SLOTS = 3


def _params(sem, vmem=VMEM_LIMIT):
    return pltpu.CompilerParams(dimension_semantics=sem, vmem_limit_bytes=vmem)


def _sigmoid(x):
    return jax.nn.sigmoid(x)


def _silu(x):
    return x * jax.nn.sigmoid(x)


def _softplus(x):
    return jnp.maximum(x, 0.0) + jnp.log1p(jnp.exp(-jnp.abs(x)))


def _dot(a, b):
    return jnp.dot(a, b, preferred_element_type=F32)


def _proj_call(name, x, ws, w_offs, n_tiles, epilogue, extras, extra_specs, out_shapes, out_specs,
               scratch=(), tm=TM, tn=TN, transposed=False):
    m, k = x.shape
    single_out = not isinstance(out_shapes, (tuple, list))
    if single_out:
        out_shapes, out_specs = (out_shapes,), (out_specs,)
    nw, ne, no = len(ws), len(extras), len(out_shapes)
    needs_cast = [w.dtype != BF16 for w in ws]
    contract = (((1,), (1,)), ((), ())) if transposed else (((1,), (0,)), ((), ()))

    def body(*refs):
        x_ref = refs[0]
        w_refs = refs[1:1 + nw]
        ex_refs = refs[1 + nw:1 + nw + ne]
        out_refs = refs[1 + nw + ne:1 + nw + ne + no]
        scr = refs[1 + nw + ne + no:]
        wb_refs = scr[:sum(needs_cast)]
        rest = scr[sum(needs_cast):]
        i = pl.program_id(1)
        wsrc, c = [], 0
        for kk in range(nw):
            if needs_cast[kk]:
                wsrc.append(wb_refs[c])
                c += 1
            else:
                wsrc.append(w_refs[kk])

        @pl.when(i == 0)
        def _():
            cc = 0
            for kk in range(nw):
                if needs_cast[kk]:
                    wb_refs[cc][...] = w_refs[kk][...].astype(BF16)
                    cc += 1

        xv = x_ref[...]
        accs = [lax.dot_general(xv, wr[...], contract, preferred_element_type=F32) for wr in wsrc]
        epilogue(i, accs, ex_refs, out_refs, rest)

    in_specs = [pl.BlockSpec((tm, k), lambda j, i: (i, 0))]
    for off in w_offs:
        if transposed:
            in_specs.append(pl.BlockSpec(
                (pl.Element(tn), pl.Element(k)),
                lambda j, i, off=off: (pl.multiple_of(off + tn * j, ROW_ALIGN), 0)))
        else:
            in_specs.append(pl.BlockSpec((k, tn), lambda j, i, off=off: (0, j + off)))
    in_specs += list(extra_specs)
    w_tile = (tn, k) if transposed else (k, tn)
    scratch_shapes = [pltpu.VMEM(w_tile, BF16) for c in needs_cast if c] + list(scratch)
    res = pl.pallas_call(
        body,
        grid=(n_tiles, m // tm),
        in_specs=in_specs,
        out_specs=tuple(out_specs),
        out_shape=tuple(out_shapes),
        scratch_shapes=scratch_shapes,
        compiler_params=_params(("parallel", "arbitrary")),
        name=name,
    )(x, *ws, *extras)
    return res[0] if single_out else res


def _short_conv_silu(i, acc, cw_ref, buf, tm):
    @pl.when(i == 0)
    def _():
        buf[0:SUBLANES, :] = jnp.zeros((SUBLANES, buf.shape[1]), F32)

    buf[SUBLANES:SUBLANES + tm, :] = acc
    cw = cw_ref[...]
    y = acc * cw[SHORT_CONV - 1:SHORT_CONV, :]
    for s in range(SHORT_CONV - 1):
        y = y + buf[pl.ds(SUBLANES - (SHORT_CONV - 1) + s, tm), :] * cw[s:s + 1, :]
    buf[0:SUBLANES, :] = buf[tm:tm + SUBLANES, :]
    return _silu(y)


def _qk_epilogue(i, accs, ex, outs, scr):
    cw_ref, scale_ref = ex
    (buf,) = scr
    y = _short_conv_silu(i, accs[0], cw_ref, buf, accs[0].shape[0])
    parts = []
    for g in range(y.shape[1] // HEAD):
        yg = y[:, g * HEAD:(g + 1) * HEAD]
        parts.append(yg * lax.rsqrt(jnp.sum(yg * yg, axis=-1, keepdims=True) + NORM_EPS))
    outs[0][...] = jnp.concatenate(parts, axis=1) * scale_ref[...]


def _v_epilogue(i, accs, ex, outs, scr):
    (cw_ref,) = ex
    (buf,) = scr
    outs[0][...] = _short_conv_silu(i, accs[0], cw_ref, buf, accs[0].shape[0])


def _z_epilogue(i, accs, ex, outs, scr):
    outs[0][...] = accs[0]


def _glu_epilogue(i, accs, ex, outs, scr):
    outs[0][...] = accs[0] * _sigmoid(accs[1])


def _gates_epilogue(i, accs, ex, outs, scr):
    outs[0][...] = _sigmoid(accs[0])


def _ab_epilogue(i, accs, ex, outs, scr):
    alog_ref, dtb_ref = ex
    gc_ref, bt_ref, gt_ref = outs
    acc = accs[0]
    g = -jnp.exp(alog_ref[...]) * _softplus(acc + dtb_ref[...])
    row = lax.broadcasted_iota(I32, g.shape, 0) % CHUNK
    s = 1
    while s < CHUNK:
        g = g + jnp.where(row >= s, pltpu.roll(g, s, axis=0), 0.0)
        s *= 2
    n_chunks = g.shape[0] // CHUNK
    tot = g.reshape(n_chunks, CHUNK, LANES)[:, CHUNK - 1:CHUNK, :]
    gc_ref[...] = g
    bt_ref[...] = _sigmoid(acc)
    gt_ref[...] = jnp.broadcast_to(tot, (n_chunks, CHUNK, LANES)).reshape(g.shape)


def _odelta_epilogue(i, accs, ex, outs, scr):
    (gate_ref,) = ex
    outs[0][...] = accs[0] * gate_ref[...]


def _pw2_epilogue(i, accs, ex, outs, scr):
    bias_ref, gate_ref, ya_ref = ex
    outs[0][...] = (ya_ref[...] + gate_ref[...] * (accs[0] + bias_ref[...])).astype(BF16)


class _Head:
    pass


def _gdn_kernel(q_ref, k_ref, v_ref, z_ref, gc_ref, bt_ref, gt_ref, gcrow_ref, btrow_ref, gtrow_ref,
                nw_ref, o_ref, s_ref):
    pp = pl.program_id(0)
    tb = pl.program_id(1)
    r = GDN_ROWS
    half = r // 2
    n_heads = 2 * GDN_PAIRS

    @pl.when(tb == 0)
    def _():
        s_ref[...] = jnp.zeros(s_ref.shape, F32)

    col_tok = lax.broadcasted_iota(I32, (half, r), 0)
    row_tok = lax.broadcasted_iota(I32, (half, r), 1)
    first_half = row_tok < half
    row_in_half = row_tok % half
    same = (row_in_half // CHUNK) == (col_tok // CHUNK)
    causal = same & (col_tok <= row_in_half)
    strict = same & (col_tok < row_in_half)
    lane = lax.broadcasted_iota(I32, (r, LANES), 1)
    gc_all = gc_ref[...]
    bt_all = bt_ref[...]
    gt_all = gt_ref[...]
    nt = (((1,), (1,)), ((), ()))
    tn = (((0,), (0,)), ((), ()))

    def transpose_bd(xc):
        return jnp.concatenate([jnp.where(first_half, xc, 0.0), jnp.where(first_half, 0.0, xc)], axis=0)

    heads = [_Head() for _ in range(n_heads)]
    for pi in range(GDN_PAIRS):
        q = q_ref[:, pi * HEAD:(pi + 1) * HEAD]
        k = k_ref[:, pi * HEAD:(pi + 1) * HEAD]
        k16 = k.astype(BF16)
        q16 = q.astype(BF16)
        kq = [lax.dot_general(k16[rs], jnp.concatenate([k16[rs], q16[rs]], axis=0), nt,
                              preferred_element_type=F32) for rs in (slice(0, half), slice(half, r))]
        kk = jnp.concatenate([kq[0][:, :half], kq[1][:, :half]], axis=1)
        qk = jnp.concatenate([kq[0][:, half:], kq[1][:, half:]], axis=1)
        for hd in heads[2 * pi:2 * pi + 2]:
            hd.q, hd.k, hd.kk, hd.qk = q, k, kk, qk

    for hl, hd in enumerate(heads):
        h = n_heads * pp + hl
        hd.cols = slice(hl * HEAD, (hl + 1) * HEAD)
        hd.gcol = jnp.sum(jnp.where(lane == h, gc_all, 0.0), axis=1, keepdims=True)
        hd.bcol = jnp.sum(jnp.where(lane == h + N_V_HEADS, bt_all, 0.0), axis=1, keepdims=True)
        hd.gtcol = jnp.sum(jnp.where(lane == h, gt_all, 0.0), axis=1, keepdims=True)
        gc_col_tok = jnp.concatenate([jnp.broadcast_to(hd.gcol[:half], (half, half)),
                                      jnp.broadcast_to(hd.gcol[half:], (half, half))], axis=1)
        hd.dm = jnp.exp(jnp.where(causal, gcrow_ref[hl, 0] - gc_col_tok, -jnp.inf))
        hd.p = jnp.where(strict, btrow_ref[hl, 0] * hd.kk * hd.dm, 0.0)
        hd.n = -hd.p

    for hd in heads:
        hd.res = _dot(hd.p.astype(BF16), transpose_bd(hd.p).astype(BF16))
    for hd in heads:
        hd.p = hd.res
    for _ in range(4):
        for hd in heads:
            rhs = jnp.concatenate([transpose_bd(hd.p), transpose_bd(hd.n)], axis=1).astype(BF16)
            hd.res = _dot(hd.p.astype(BF16), rhs)
        for hd in heads:
            hd.n = hd.n + hd.p + hd.res[:, r:]
            hd.p = hd.res[:, :r]
    for hd in heads:
        hd.res = _dot(hd.p.astype(BF16), transpose_bd(hd.n).astype(BF16))
    for hd in heads:
        hd.n = hd.n + hd.p + hd.res

    for hd in heads:
        hd.n16 = transpose_bd(hd.n).T.astype(BF16)
        hd.eg = jnp.exp(hd.gcol)
        hd.rhs = jnp.concatenate([v_ref[:, hd.cols] * hd.bcol, hd.k * (hd.bcol * hd.eg)], axis=1)
    for hd in heads:
        hd.sol16 = (hd.rhs + _dot(hd.n16, hd.rhs.astype(BF16))).astype(BF16)
    for hd in heads:
        hd.qkd16 = transpose_bd(hd.qk * hd.dm).T.astype(BF16)
    for hd in heads:
        hd.x = _dot(hd.qkd16, hd.sol16)
    for hl, hd in enumerate(heads):
        hd.qp16 = (hd.q * hd.eg - hd.x[:, HEAD:]).astype(BF16)
        hd.kd16 = (hd.k * jnp.exp(hd.gtcol - hd.gcol)).astype(BF16)
        hd.state = s_ref[hl]
        hd.o = []

    for j in range(r // CHUNK):
        sl = slice(j * CHUNK, (j + 1) * CHUNK)
        for hd in heads:
            hd.kc = lax.dot_general(hd.kd16[sl], hd.sol16[sl], tn, preferred_element_type=F32)
        for hd in heads:
            lhs = jnp.concatenate([hd.kc[:, HEAD:].astype(BF16), hd.qp16[sl]], axis=0)
            hd.ks_qs = _dot(lhs, hd.state.astype(BF16))
        for hl, hd in enumerate(heads):
            hd.o.append(hd.ks_qs[HEAD:] + hd.x[sl, :HEAD])
            hd.state = (hd.state * jnp.exp(gtrow_ref[hl, 0, j:j + 1, :]) + hd.kc[:, :HEAD]
                        - hd.ks_qs[:HEAD])

    for hl, hd in enumerate(heads):
        s_ref[hl] = hd.state
        o = jnp.concatenate(hd.o, axis=0)
        o = o * lax.rsqrt(jnp.mean(o * o, axis=-1, keepdims=True) + NORM_EPS) * nw_ref[...]
        o_ref[:, hd.cols] = (o * _silu(z_ref[:, hd.cols])).astype(o_ref.dtype)


def _gdn_call(qk, v, z, gc, bt, gt, gcrow, btrow, gtrow, nw):
    t = qk.shape[0]
    r = GDN_ROWS
    p = GDN_PAIRS
    k_off = N_QK_HEADS // p
    return pl.pallas_call(
        _gdn_kernel,
        grid=(N_QK_HEADS // p, t // r),
        in_specs=[
            pl.BlockSpec((r, p * HEAD), lambda pp, tb: (tb, pp)),
            pl.BlockSpec((r, p * HEAD), lambda pp, tb: (tb, k_off + pp)),
            pl.BlockSpec((r, 2 * p * HEAD), lambda pp, tb: (tb, pp)),
            pl.BlockSpec((r, 2 * p * HEAD), lambda pp, tb: (tb, pp)),
            pl.BlockSpec((r, LANES), lambda pp, tb: (tb, 0)),
            pl.BlockSpec((r, LANES), lambda pp, tb: (tb, 0)),
            pl.BlockSpec((r, LANES), lambda pp, tb: (tb, 0)),
            pl.BlockSpec((2 * p, 1, 1, r), lambda pp, tb: (pp, tb, 0, 0)),
            pl.BlockSpec((2 * p, 1, 1, r), lambda pp, tb: (pp, tb, 0, 0)),
            pl.BlockSpec((2 * p, 1, r // CHUNK, LANES), lambda pp, tb: (pp, tb, 0, 0)),
            pl.BlockSpec((1, HEAD), lambda pp, tb: (0, 0)),
        ],
        out_specs=pl.BlockSpec((r, 2 * p * HEAD), lambda pp, tb: (tb, pp)),
        out_shape=jax.ShapeDtypeStruct((t, V_WIDTH), BF16),
        scratch_shapes=[pltpu.VMEM((2 * p, HEAD, HEAD), F32)],
        compiler_params=_params(("parallel", "arbitrary")),
        name="gdn",
    )(qk, qk, v, z, gc, bt, gt, gcrow, btrow, gtrow, nw)


def _layer_norm(r, g, b):
    mu = jnp.mean(r, axis=-1, keepdims=True)
    rc = r - mu
    var = jnp.mean(rc * rc, axis=-1, keepdims=True)
    return rc * lax.rsqrt(var + LN_EPS) * g + b


def _dwconv_kernel(c_ref, w_ref, b_ref, g_ref, beta_ref, o_ref, buf, shifted, accbuf):
    i = pl.program_id(0)
    tm = CONV_TM
    ch = buf.shape[1]
    n_shift = HALO + tm - SUBLANES

    @pl.when(i == 0)
    def _():
        buf[0:HALO, :] = jnp.zeros((HALO, ch), F32)

    buf[HALO:HALO + tm, :] = c_ref[...]
    for b in range(1, SUBLANES):
        shifted[b - 1] = buf[pl.ds(b, n_shift), :]

    def row_body(rc, carry):
        r0 = pl.multiple_of(rc * CONV_RC, CONV_RC)
        for cc in range(ch // CONV_CC):
            cs = slice(cc * CONV_CC, (cc + 1) * CONV_CC)
            acc = buf[pl.ds(HALO + r0, CONV_RC), cs] * w_ref[CONV_WIDTH - 1:CONV_WIDTH, cs] + b_ref[:, cs]
            for s in range(CONV_WIDTH - 1):
                a, b = divmod(HALO - (CONV_WIDTH - 1) + s, SUBLANES)
                if b == 0:
                    src = buf[pl.ds(a * SUBLANES + r0, CONV_RC), cs]
                else:
                    src = shifted[b - 1, pl.ds(a * SUBLANES + r0, CONV_RC), cs]
                acc = acc + src * w_ref[s:s + 1, cs]
            accbuf[pl.ds(r0, CONV_RC), cs] = acc
        return carry

    lax.fori_loop(0, tm // CONV_RC, row_body, 0)
    buf[0:HALO, :] = buf[tm:tm + HALO, :]
    o_ref[...] = _silu(_layer_norm(accbuf[...], g_ref[...], beta_ref[...])).astype(o_ref.dtype)


def _dwconv_call(c, w, b, g, beta):
    t, ch = c.shape
    tm = CONV_TM
    vec = pl.BlockSpec((1, ch), lambda i: (0, 0))
    return pl.pallas_call(
        _dwconv_kernel,
        grid=(t // tm,),
        in_specs=[pl.BlockSpec((tm, ch), lambda i: (i, 0)),
                  pl.BlockSpec((CONV_WIDTH, ch), lambda i: (0, 0)), vec, vec, vec],
        out_specs=pl.BlockSpec((tm, ch), lambda i: (i, 0)),
        out_shape=jax.ShapeDtypeStruct((t, ch), BF16),
        scratch_shapes=[pltpu.VMEM((HALO + tm, ch), F32),
                        pltpu.VMEM((SUBLANES - 1, HALO + tm - SUBLANES, ch), F32),
                        pltpu.VMEM((tm, ch), F32)],
        compiler_params=_params(("arbitrary",)),
        name="dwconv_ln",
    )(c, w, b, g, beta)


def _outproj_kernel(m_ref, w_ref, x_ref, g_ref, b_ref, h_ref):
    y = _dot(m_ref[...], w_ref[...])
    h_ref[...] = _layer_norm(DN_ALPHA * x_ref[...] + y, g_ref[...], b_ref[...])


def _outproj_call(mixed, w16, x, g, b):
    t, d = x.shape
    tm = 256
    vec = pl.BlockSpec((1, d), lambda i: (0, 0))
    return pl.pallas_call(
        _outproj_kernel,
        grid=(t // tm,),
        in_specs=[pl.BlockSpec((tm, d), lambda i: (i, 0)), pl.BlockSpec((d, d), lambda i: (0, 0)),
                  pl.BlockSpec((tm, d), lambda i: (i, 0)), vec, vec],
        out_specs=pl.BlockSpec((tm, d), lambda i: (i, 0)),
        out_shape=jax.ShapeDtypeStruct((t, d), F32),
        compiler_params=_params(("parallel",)),
        name="outproj_ln1",
    )(mixed, w16, x, g, b)


def _route_kernel(h_ref, w_ref, bias_ref, idx_ref, wts_ref, rank_ref, cnt_ref, carry):
    @pl.when(pl.program_id(0) == 0)
    def _():
        carry[...] = jnp.zeros(carry.shape, F32)

    logits = _dot(h_ref[...], w_ref[...])
    scores = _sigmoid(logits)
    biased = scores + bias_ref[...]
    shape = biased.shape
    tm = shape[0]
    lane_i = lax.broadcasted_iota(I32, shape, 1)
    lane = lane_i.astype(F32)
    per_group = N_EXPERTS // N_GROUPS
    grp = (lane_i // per_group).astype(F32)
    neg = -jnp.inf
    big = 1e9

    def rmax(x):
        return jnp.max(x, axis=1, keepdims=True)

    def rmin(x):
        return jnp.min(x, axis=1, keepdims=True)

    def rsum(x):
        return jnp.sum(x, axis=1, keepdims=True)

    gs = jnp.zeros(shape, F32)
    for g in range(N_GROUPS):
        in_g = grp == float(g)
        m = jnp.where(in_g, biased, neg)
        m1 = rmax(m)
        i1 = rmin(jnp.where(m == m1, lane, big))
        m2 = rmax(jnp.where(lane == i1, neg, m))
        gs = jnp.where(in_g, m1 + m2, gs)
    sel_g = jnp.zeros(shape, jnp.bool_)
    cur = gs
    for _ in range(TOPK_GROUPS):
        mx = rmax(cur)
        gi = rmin(jnp.where(cur == mx, grp, big))
        hit = grp == gi
        sel_g = sel_g | hit
        cur = jnp.where(hit, neg, cur)
    masked = jnp.where(sel_g, biased, neg)
    ids, ws = [], []
    sel = jnp.zeros(shape, F32)
    for _ in range(TOP_K):
        mx = rmax(masked)
        ik = rmin(jnp.where(masked == mx, lane, big))
        hit = lane == ik
        ws.append(rsum(jnp.where(hit, scores, 0.0)))
        ids.append(ik)
        sel = jnp.where(hit, 1.0, sel)
        masked = jnp.where(hit, neg, masked)
    w = jnp.concatenate(ws, axis=1)
    wts_ref[...] = w / rsum(w) * ROUTE_SCALE
    idx_ref[...] = jnp.concatenate(ids, axis=1).astype(I32)

    earlier = (lax.broadcasted_iota(I32, (tm, tm), 1) < lax.broadcasted_iota(I32, (tm, tm), 0))
    before = _dot(earlier.astype(BF16), sel.astype(BF16)) + carry[...]
    ranks = [rsum(jnp.where(lane == ik, before, 0.0)) for ik in ids]
    rank_ref[...] = jnp.concatenate(ranks, axis=1).astype(I32)
    carry[...] = carry[...] + jnp.sum(sel, axis=0, keepdims=True)
    cnt_ref[...] = carry[...]


def _route_call(h, w_router, bias):
    t, d = h.shape
    tm = ROUTE_TM
    tok = pl.BlockSpec((tm, TOP_K), lambda i: (i, 0))
    return pl.pallas_call(
        _route_kernel,
        grid=(t // tm,),
        in_specs=[pl.BlockSpec((tm, d), lambda i: (i, 0)),
                  pl.BlockSpec((d, N_EXPERTS), lambda i: (0, 0)),
                  pl.BlockSpec((1, N_EXPERTS), lambda i: (0, 0))],
        out_specs=(tok, tok, tok, pl.BlockSpec((1, N_EXPERTS), lambda i: (0, 0))),
        out_shape=(jax.ShapeDtypeStruct((t, TOP_K), I32), jax.ShapeDtypeStruct((t, TOP_K), F32),
                   jax.ShapeDtypeStruct((t, TOP_K), I32), jax.ShapeDtypeStruct((1, N_EXPERTS), F32)),
        scratch_shapes=[pltpu.VMEM((1, N_EXPERTS), F32)],
        compiler_params=_params(("arbitrary",)),
        name="route",
    )(h, w_router, bias)


def _dest_kernel(idx_ref, rank_ref, start_ref, dest_ref):
    idx = idx_ref[...]
    shape = (idx.shape[0], N_EXPERTS)
    lane = lax.broadcasted_iota(I32, shape, 1)
    start = start_ref[...]
    cols = []
    for kk in range(TOP_K):
        hit = lane == idx[:, kk:kk + 1]
        cols.append(jnp.sum(jnp.where(hit, start, 0.0), axis=1, keepdims=True))
    dest_ref[...] = jnp.concatenate(cols, axis=1).astype(I32) + rank_ref[...]


def _dest_call(idx, rank, start_f32):
    t = idx.shape[0]
    tm = 512
    tok = pl.BlockSpec((tm, TOP_K), lambda i: (i, 0))
    return pl.pallas_call(
        _dest_kernel,
        grid=(t // tm,),
        in_specs=[tok, tok, pl.BlockSpec((1, N_EXPERTS), lambda i: (0, 0))],
        out_specs=tok,
        out_shape=jax.ShapeDtypeStruct((t, TOP_K), I32),
        compiler_params=_params(("parallel",)),
        name="dest",
    )(idx, rank, start_f32)


def _row_copy(src, src_row, dst, dst_row, sem):
    return pltpu.make_async_copy(src.at[pl.ds(src_row, 1), :], dst.at[pl.ds(dst_row, 1), :], sem)


def _pack_halves(x):
    half = x.shape[1] // 2
    lo = lax.bitcast_convert_type(x[:, :half].astype(BF16).astype(F32), U32)
    hi = lax.bitcast_convert_type(x[:, half:].astype(BF16).astype(F32), U32)
    return hi | (lo >> 16)


def _unpack_halves(w):
    lo = lax.bitcast_convert_type(w << 16, F32)
    hi = lax.bitcast_convert_type(w & jnp.uint32(0xFFFF0000), F32)
    return lo, hi


def _rows_wait(src, dst, dst_row, n_rows, sem):
    pltpu.make_async_copy(src.at[pl.ds(0, n_rows), :], dst.at[pl.ds(dst_row, n_rows), :], sem).wait()


def _scatter_kernel(dest_ref, h_ref, xs_hbm, packed, sem):
    packed[...] = _pack_halves(h_ref[...])

    def issue(tok, c):
        for kk in range(TOP_K):
            _row_copy(packed, tok, xs_hbm, dest_ref[0, 0, tok * TOP_K + kk], sem.at[tok]).start(priority=kk % 2)
        return c

    lax.fori_loop(0, SCATTER_TM, issue, 0, unroll=2)

    def wait(tok, c):
        _rows_wait(packed, xs_hbm, 0, TOP_K, sem.at[tok])
        return c

    lax.fori_loop(0, SCATTER_TM, wait, 0, unroll=8)


def _scatter_call(dest, h):
    t, d = h.shape
    tm = SCATTER_TM
    n = tm * TOP_K
    return pl.pallas_call(
        _scatter_kernel,
        grid=(t // tm,),
        in_specs=[pl.BlockSpec((1, 1, n), lambda i: (i, 0, 0), memory_space=pltpu.SMEM),
                  pl.BlockSpec((tm, d), lambda i: (i, 0))],
        out_specs=pl.BlockSpec(memory_space=pl.ANY),
        out_shape=jax.ShapeDtypeStruct((t * TOP_K, d // 2), U32),
        scratch_shapes=[pltpu.VMEM((tm, d // 2), U32), pltpu.SemaphoreType.DMA((tm,))],
        compiler_params=_params(("arbitrary",)),
        name="dispatch",
    )(dest.reshape(t // tm, 1, n), h)


def _weight_copies(w_hbm, wgu32, wd32, wsem, e, slot):
    wg_hbm, wu_hbm, wd_hbm = w_hbm
    return [pltpu.make_async_copy(wg_hbm.at[e], wgu32.at[slot, :, 0:D_EXPERT], wsem.at[slot, 0]),
            pltpu.make_async_copy(wu_hbm.at[e], wgu32.at[slot, :, D_EXPERT:2 * D_EXPERT], wsem.at[slot, 1]),
            pltpu.make_async_copy(wd_hbm.at[e], wd32.at[slot], wsem.at[slot, 2])]


def _expert_kernel(vb_ref, ve_ref, lo_ref, hi_ref, fb_ref, fe_ref, slot_ref, nxt1_ref, nxt2_ref, nv_ref,
                   x_ref, wg_hbm, wu_hbm, wd_hbm, y_ref, wgu32, wd32, wsem):
    v = pl.program_id(0)
    w_hbm = (wg_hbm, wu_hbm, wd_hbm)

    def start(e, slot):
        for cp in _weight_copies(w_hbm, wgu32, wd32, wsem, e, slot):
            cp.start(priority=WEIGHT_DMA_QUEUE)

    @pl.when(v < nv_ref[0])
    def _():
        slot = slot_ref[v]

        @pl.when(fe_ref[v] == 1)
        def _():
            @pl.when(v == 0)
            def _():
                start(ve_ref[v], slot)

                @pl.when(nxt1_ref[v] >= 0)
                def _():
                    start(nxt1_ref[v], (slot + 1) % WEIGHT_SLOTS)

            @pl.when(nxt2_ref[v] >= 0)
            def _():
                start(nxt2_ref[v], (slot + 2) % WEIGHT_SLOTS)

            for cp in _weight_copies(w_hbm, wgu32, wd32, wsem, ve_ref[v], slot):
                cp.wait()

        x_lo, x_hi = _unpack_halves(x_ref[...])
        x = jnp.concatenate([x_lo.astype(BF16), x_hi.astype(BF16)], axis=1)
        gu = _dot(x, wgu32[slot])
        act = _silu(gu[:, :D_EXPERT]) * gu[:, D_EXPERT:]
        y = _pack_halves(_dot(act.astype(BF16), wd32[slot]))
        rows = lax.broadcasted_iota(I32, y.shape, 0)
        mine = (rows >= lo_ref[v]) & (rows < hi_ref[v])

        @pl.when(fb_ref[v] == 1)
        def _():
            y_ref[...] = jnp.where(mine, y, jnp.uint32(0))

        @pl.when(fb_ref[v] == 0)
        def _():
            y_ref[...] = jnp.where(mine, y, y_ref[...])


def _expert_call(tables, xs, w_gate, w_up, w_down):
    n_rows, dp = xs.shape
    d = 2 * dp
    blk = EXPERT_BLOCK
    n_visits = tables[0].shape[0]
    any_spec = pl.BlockSpec(memory_space=pl.ANY)
    grid_spec = pltpu.PrefetchScalarGridSpec(
        num_scalar_prefetch=len(tables),
        grid=(n_visits,),
        in_specs=[pl.BlockSpec((blk, dp), lambda v, vb, *_: (vb[v], 0)), any_spec, any_spec, any_spec],
        out_specs=pl.BlockSpec((blk, dp), lambda v, vb, *_: (vb[v], 0)),
        scratch_shapes=[
            pltpu.VMEM((WEIGHT_SLOTS, d, 2 * D_EXPERT), F32),
            pltpu.VMEM((WEIGHT_SLOTS, D_EXPERT, d), F32),
            pltpu.SemaphoreType.DMA((WEIGHT_SLOTS, 3)),
        ],
    )
    return pl.pallas_call(
        _expert_kernel,
        grid_spec=grid_spec,
        out_shape=jax.ShapeDtypeStruct((n_rows, dp), U32),
        compiler_params=_params(("arbitrary",)),
        name="experts",
    )(*tables, xs, w_gate, w_up, w_down)


def _visit_tables(counts, n_rows):
    blk = EXPERT_BLOCK
    n_visits = n_rows // blk + N_EXPERTS - 1
    end = jnp.cumsum(counts)
    start = end - counts
    nonempty = counts > 0
    first_blk = start // blk
    last_blk = jnp.maximum(end - 1, 0) // blk
    nvis = jnp.where(nonempty, last_blk - first_blk + 1, 0)
    vis_end = jnp.cumsum(nvis)
    vis_start = vis_end - nvis
    n_used = vis_end[-1]
    ids = jnp.arange(N_EXPERTS, dtype=I32)
    ordinal = jnp.cumsum(nonempty.astype(I32)) - 1
    nxt_incl = lax.cummin(jnp.where(nonempty, ids, N_EXPERTS), axis=0, reverse=True)
    nxt_e = jnp.concatenate([nxt_incl[1:], jnp.full((1,), N_EXPERTS, I32)])
    nxt_e = jnp.where(nxt_e >= N_EXPERTS, -1, nxt_e)
    nxt2_e = jnp.sum(jnp.where(nxt_e[:, None] == ids[None, :], nxt_e[None, :], 0), axis=1)
    nxt2_e = jnp.where(nxt_e >= 0, nxt2_e, -1)
    v = jnp.arange(n_visits, dtype=I32)
    vc = jnp.clip(v, 0, jnp.maximum(n_used - 1, 0))
    ve = jnp.minimum(jnp.sum((vis_end[None, :] <= vc[:, None]).astype(I32), axis=1), N_EXPERTS - 1)
    onehot = ve[:, None] == ids[None, :]
    look = lambda tab: jnp.sum(jnp.where(onehot, tab.astype(I32)[None, :], 0), axis=1)
    v_first = look(vis_start)
    vb = look(first_blk) + (vc - v_first)
    lo = jnp.clip(look(start) - vb * blk, 0, blk)
    hi = jnp.clip(look(end) - vb * blk, 0, blk)
    fb = jnp.concatenate([jnp.ones((1,), I32), (vb[1:] != vb[:-1]).astype(I32)])
    fe = (vc == v_first).astype(I32)
    tables = (vb, ve, lo, hi, fb, fe, look(ordinal % WEIGHT_SLOTS), look(nxt_e), look(nxt2_e),
              n_used.astype(I32).reshape(1))
    return tables, start


def _combine_kernel(dest_ref, dnext_ref, ys_hbm, wts_ref, h_ref, wg_ref, wu_ref, wd_ref, g_ref, b_ref,
                    o_ref, buf, sem):
    tm = COMBINE_TM
    i = pl.program_id(0)
    slot = i % 2

    def issue(d_ref, s, kk, r):
        a = kk * tm + r
        _row_copy(ys_hbm, d_ref[0, 0, a], buf.at[s], a, sem.at[s, kk]).start(priority=kk % 2)

    @pl.when(i == 0)
    def _():
        def body(r, c):
            for kk in range(TOP_K):
                issue(dest_ref, slot, kk, r)
            return c

        lax.fori_loop(0, tm, body, 0, unroll=2)

    h = h_ref[...]
    x = h.astype(BF16)
    act = _silu(_dot(x, wg_ref[...])) * _dot(x, wu_ref[...])
    acc = DN_ALPHA * h + _dot(act.astype(BF16), wd_ref[...])
    half = acc.shape[1] // 2
    acc_lo, acc_hi = acc[:, :half], acc[:, half:]
    wts = wts_ref[...]
    for kk in range(TOP_K):
        for r in range(tm):
            issue(dnext_ref, 1 - slot, kk, r)
        _rows_wait(ys_hbm, buf.at[slot], kk * tm, tm, sem.at[slot, kk])
        y_lo, y_hi = _unpack_halves(buf[slot, kk * tm:(kk + 1) * tm, :])
        acc_lo = acc_lo + wts[:, kk:kk + 1] * y_lo
        acc_hi = acc_hi + wts[:, kk:kk + 1] * y_hi
    o_ref[...] = _layer_norm(jnp.concatenate([acc_lo, acc_hi], axis=1), g_ref[...], b_ref[...])

    @pl.when(i == pl.num_programs(0) - 1)
    def _():
        for kk in range(TOP_K):
            _rows_wait(ys_hbm, buf.at[1 - slot], kk * tm, tm, sem.at[1 - slot, kk])


def _combine_call(dest_km, ys, wts, h, w_sh_gate, w_sh_up, w_sh_down, g, b):
    t, d = h.shape
    ds = w_sh_gate.shape[1]
    tm = COMBINE_TM
    n = tm * TOP_K
    n_tiles = t // tm
    vec = pl.BlockSpec((1, d), lambda i: (0, 0))
    return pl.pallas_call(
        _combine_kernel,
        grid=(n_tiles,),
        in_specs=[
            pl.BlockSpec((1, 1, n), lambda i: (i, 0, 0), memory_space=pltpu.SMEM),
            pl.BlockSpec((1, 1, n), lambda i: (jnp.minimum(i + 1, n_tiles - 1), 0, 0),
                         memory_space=pltpu.SMEM),
            pl.BlockSpec(memory_space=pl.ANY),
            pl.BlockSpec((tm, TOP_K), lambda i: (i, 0)),
            pl.BlockSpec((tm, d), lambda i: (i, 0)),
            pl.BlockSpec((d, ds), lambda i: (0, 0)),
            pl.BlockSpec((d, ds), lambda i: (0, 0)),
            pl.BlockSpec((ds, d), lambda i: (0, 0)),
            vec, vec,
        ],
        out_specs=pl.BlockSpec((tm, d), lambda i: (i, 0)),
        out_shape=jax.ShapeDtypeStruct((t, d), F32),
        scratch_shapes=[pltpu.VMEM((2, n, d // 2), U32), pltpu.SemaphoreType.DMA((2, TOP_K))],
        compiler_params=_params(("arbitrary",)),
        name="combine_ln2",
    )(dest_km, dest_km, ys, wts, h, w_sh_gate, w_sh_up, w_sh_down, g, b)


def _token_mixing(x, w_in, w_short_conv, a_log, dt_bias, delta_norm_w, w_o_delta, conv_dw_w,
                  conv_dw_b, conv_ln_g, conv_ln_b, w_pw2, b_pw2, w_out, ln1_g, ln1_b):
    t, d = x.shape
    x16 = x.astype(BF16)
    w_t = w_in.T
    row = lambda a: a.reshape(1, -1)
    tile = pl.BlockSpec((TM, TN), lambda j, i: (i, j))
    wide = pl.BlockSpec((TM_WIDE, TN), lambda j, i: (i, j))
    xtile = pl.BlockSpec((TM_X, TN_X), lambda j, i: (i, j))
    conv_buf = pltpu.VMEM((TM_X + SUBLANES, TN_X), F32)

    scale = jnp.concatenate([jnp.full((1, QK_WIDTH), HEAD ** -0.5, F32), jnp.ones((1, QK_WIDTH), F32)], axis=1)
    qk = _proj_call(
        "proj_qk", x16, [w_t], [0], 2 * QK_WIDTH // TN_X, _qk_epilogue,
        [w_short_conv, scale],
        [pl.BlockSpec((SHORT_CONV, TN_X), lambda j, i: (0, j)), pl.BlockSpec((1, TN_X), lambda j, i: (0, j))],
        jax.ShapeDtypeStruct((t, 2 * QK_WIDTH), F32), xtile, scratch=[conv_buf], tm=TM_X, tn=TN_X,
        transposed=True)
    v_off = 2 * QK_WIDTH // TN_X
    v = _proj_call(
        "proj_v", x16, [w_t], [2 * QK_WIDTH], V_WIDTH // TN_X, _v_epilogue,
        [w_short_conv], [pl.BlockSpec((SHORT_CONV, TN_X), lambda j, i: (0, j + v_off))],
        jax.ShapeDtypeStruct((t, V_WIDTH), F32), xtile, scratch=[conv_buf], tm=TM_X, tn=TN_X,
        transposed=True)
    z = _proj_call(
        "proj_z", x16, [w_t], [COL_Z], V_WIDTH // TN_X, _z_epilogue, [], [],
        jax.ShapeDtypeStruct((t, V_WIDTH), F32), xtile, tm=TM_X, tn=TN_X, transposed=True)

    pad_h = lambda a: jnp.pad(a.reshape(1, -1), ((0, 0), (0, LANES - N_V_HEADS)))
    lane_vec = pl.BlockSpec((1, LANES), lambda j, i: (0, 0))
    lane_tile = pl.BlockSpec((TM, LANES), lambda j, i: (i, 0))
    lane_shape = jax.ShapeDtypeStruct((t, LANES), F32)
    gc, bt, gt = _proj_call(
        "proj_ab", x16, [w_t], [COL_AB], 1, _ab_epilogue, [pad_h(a_log), pad_h(dt_bias)],
        [lane_vec, lane_vec], (lane_shape, lane_shape, lane_shape), (lane_tile, lane_tile, lane_tile),
        tn=LANES, transposed=True)

    c = _proj_call(
        "proj_glu", x16, [w_t, w_t], [COL_GLU, COL_GLU + d], d // TN, _glu_epilogue, [], [],
        jax.ShapeDtypeStruct((t, d), F32), wide, tm=TM_WIDE, transposed=True)
    gates = _proj_call(
        "proj_gates", x16, [w_t], [COL_GATES], 2 * d // TN_X, _gates_epilogue, [], [],
        jax.ShapeDtypeStruct((t, 2 * d), F32), xtile, tm=TM_X, tn=TN_X, transposed=True)

    r = GDN_ROWS
    gcrow = gc[:, :N_V_HEADS].T.reshape(N_V_HEADS, t // r, 1, r)
    btrow = bt[:, N_V_HEADS:2 * N_V_HEADS].T.reshape(N_V_HEADS, t // r, 1, r)
    gt_chunk = gt[CHUNK - 1::CHUNK, :N_V_HEADS].T
    gtrow = jnp.broadcast_to(gt_chunk[:, :, None], (N_V_HEADS, t // CHUNK, LANES))
    gtrow = gtrow.reshape(N_V_HEADS, t // r, r // CHUNK, LANES)
    og = _gdn_call(qk, v, z, gc, bt, gt, gcrow, btrow, gtrow, row(delta_norm_w))

    ya = _proj_call(
        "proj_odelta", og, [w_o_delta], [0], d // TN, _odelta_epilogue, [gates], [tile],
        jax.ShapeDtypeStruct((t, d), F32), tile)

    c_act = _dwconv_call(c, conv_dw_w, row(conv_dw_b), row(conv_ln_g), row(conv_ln_b))
    g_off = d // TN_X
    mixed = _proj_call(
        "proj_pw2", c_act, [w_pw2], [0], d // TN_X, _pw2_epilogue, [row(b_pw2), gates, ya],
        [pl.BlockSpec((1, TN_X), lambda j, i: (0, j)),
         pl.BlockSpec((TM_X, TN_X), lambda j, i: (i, j + g_off)), xtile],
        jax.ShapeDtypeStruct((t, d), BF16), xtile, tm=TM_X, tn=TN_X)
    return _outproj_call(mixed, w_out.astype(BF16), x, row(ln1_g), row(ln1_b))


def _moe(h, w_router, router_bias, w_gate, w_up, w_down, w_sh_gate, w_sh_up, w_sh_down, ln2_g, ln2_b):
    t, d = h.shape
    idx, wts, rank, counts = _route_call(h, w_router, router_bias.reshape(1, -1))
    tables, start = _visit_tables(counts.reshape(-1).astype(I32), t * TOP_K)
    dest = _dest_call(idx, rank, start.astype(F32).reshape(1, -1))
    xs = _scatter_call(dest, h)
    ys = _expert_call(tables, xs, w_gate, w_up, w_down)
    tm = COMBINE_TM
    dest_km = dest.reshape(t // tm, tm, TOP_K).transpose(0, 2, 1).reshape(t // tm, 1, tm * TOP_K)
    return _combine_call(dest_km, ys, wts, h, w_sh_gate, w_sh_up, w_sh_down,
                         ln2_g.reshape(1, -1), ln2_b.reshape(1, -1))


def kernel(x, w_in, w_short_conv, a_log, dt_bias, delta_norm_w, w_o_delta, conv_dw_w, conv_dw_b,
           conv_ln_g, conv_ln_b, w_pw2, b_pw2, w_out, ln1_g, ln1_b, w_router, router_bias, w_gate,
           w_up, w_down, w_sh_gate, w_sh_up, w_sh_down, ln2_g, ln2_b):
    batch, seq, d = x.shape
    depth = w_in.shape[0]
    outs = []
    for bi in range(batch):
        h = x[bi]
        for li in range(depth):
            h = _token_mixing(h, w_in[li], w_short_conv[li], a_log[li], dt_bias[li], delta_norm_w[li],
                              w_o_delta[li], conv_dw_w[li], conv_dw_b[li], conv_ln_g[li], conv_ln_b[li],
                              w_pw2[li], b_pw2[li], w_out[li], ln1_g[li], ln1_b[li])
            h = _moe(h, w_router[li], router_bias[li], w_gate[li], w_up[li], w_down[li],
                     w_sh_gate[li], w_sh_up[li], w_sh_down[li], ln2_g[li], ln2_b[li])
        outs.append(h)
    return jnp.stack(outs, axis=0)
```
